```python
import jax, jax.numpy as jnp
from jax import lax
import numpy as np

D_MODEL = 1024
BATCH = 4
SEQ = 4096
DEPTH = 2
DEC_BATCH = 32
DEC_SEQ = 8
PAST_LEN = 8192
PAGE_SIZE = 128

N_REC_LAYERS = (DEPTH + 1) // 2
N_ATT_LAYERS = DEPTH // 2
EPS = 1e-6
RET_HEADS = 4
RET_DK = 64
RET_DV = 128
RET_CHUNK = 128
LRU_WIDTH = 512
LRU_BLOCKS = 4
LRU_BLOCK = LRU_WIDTH // LRU_BLOCKS
CONV_WIDTH = 4
LRU_C = 8.0
REC_SPLITS = (RET_HEADS * RET_DK, 2 * RET_HEADS * RET_DK, 2 * RET_HEADS * RET_DK + RET_HEADS * RET_DV,
              2 * RET_HEADS * RET_DK + 2 * RET_HEADS * RET_DV,
              2 * RET_HEADS * RET_DK + 2 * RET_HEADS * RET_DV + LRU_WIDTH)
REC_IN = 2 * RET_HEADS * RET_DK + 2 * RET_HEADS * RET_DV + 2 * LRU_WIDTH
REC_OUT = RET_HEADS * RET_DV + LRU_WIDTH
ATT_HEADS = 16
ATT_DH = D_MODEL // ATT_HEADS
PATTERNS = ((128, 1), (512, 4), (2048, 16))
MAX_WINDOW = 2048
ATT_BLOCK = 128
D_FF = -(-8 * D_MODEL // (3 * 256)) * 256

kernel_name = "retnet_griffin_longnet_hybrid_step"


def rmsnorm(x, g):
    xf = x.astype(jnp.float32)
    y = xf * lax.rsqrt(jnp.mean(xf * xf, axis=-1, keepdims=True) + EPS)
    return (y * g.astype(jnp.float32)).astype(x.dtype)


def head_rms(a, g):
    return a * lax.rsqrt(jnp.mean(a * a, axis=-1, keepdims=True) + EPS) * g.astype(jnp.float32)


def swiglu_block(x, g, w_gate, w_up, w_down):
    h = rmsnorm(x, g)
    return x + (jax.nn.silu(h @ w_gate) * (h @ w_up)) @ w_down


def retention(q, k, v, s0):
    B, T = q.shape[:2]
    C = RET_CHUNK if T % RET_CHUNK == 0 else T
    n = T // C
    log_g = jnp.log1p(-jnp.power(2.0, -5.0 - jnp.arange(RET_HEADS, dtype=jnp.float32)))
    idx = jnp.arange(C, dtype=jnp.float32)
    diff = idx[:, None] - idx[None, :]
    decay = jnp.where(diff >= 0, jnp.exp(log_g[:, None, None] * jnp.maximum(diff, 0.0)), 0.0)
    q_dec = jnp.exp(log_g[:, None] * (idx + 1.0))
    k_dec = jnp.exp(log_g[:, None] * (C - 1.0 - idx))
    c_dec = jnp.exp(log_g * C)

    def blocks(a):
        return a.reshape(B, n, C, RET_HEADS, a.shape[-1]).transpose(1, 0, 3, 2, 4)

    def step(S, inp):
        qc, kc, vc = inp
        att = jnp.einsum('bhqd,bhkd->bhqk', qc, kc) * decay
        o = (jnp.einsum('bhqk,bhkv->bhqv', att, vc)
             + jnp.einsum('bhqd,bhdv->bhqv', qc * q_dec[..., None], S))
        S = S * c_dec[:, None, None] + jnp.einsum('bhkd,bhkv->bhdv', kc * k_dec[..., None], vc)
        return S, o

    S, o = lax.scan(step, s0, (blocks(q), blocks(k), blocks(v)))
    o = o.transpose(1, 0, 3, 2, 4).reshape(B, T, RET_HEADS, RET_DV)
    return o, S


def causal_conv(x, buf, w, b):
    T = x.shape[1]
    xp = jnp.concatenate([buf.astype(x.dtype), x], axis=1)
    y = b + xp[:, 0:T] * w[0]
    for j in range(1, CONV_WIDTH):
        y = y + xp[:, j:j + T] * w[j]
    return y, xp[:, -(CONV_WIDTH - 1):]


def rg_lru(x, w_rg, b_rg, w_ig, b_ig, lam, h0):
    B, T, W = x.shape
    f32 = jnp.float32
    xb = x.reshape(B, T, LRU_BLOCKS, LRU_BLOCK)
    r = jax.nn.sigmoid(jnp.einsum('btni,nij->btnj', xb, w_rg.astype(f32)).reshape(B, T, W) + b_rg.astype(f32))
    i = jax.nn.sigmoid(jnp.einsum('btni,nij->btnj', xb, w_ig.astype(f32)).reshape(B, T, W) + b_ig.astype(f32))
    log_a = -LRU_C * r * jax.nn.softplus(-lam.astype(f32))
    a = jnp.exp(log_a)
    bterm = jnp.sqrt(-jnp.expm1(2.0 * log_a)) * (i * x)
    bterm = bterm.at[:, 0].add(a[:, 0] * h0)

    def comb(lhs, rhs):
        a1, b1 = lhs
        a2, b2 = rhs
        return a1 * a2, a2 * b1 + b2

    _, h = lax.associative_scan(comb, (a, bterm), axis=1)
    return h, h[:, -1]


def rec_mixer(h, s_ret0, h_lru0, conv0, w_in, w_out, ret_g, conv_w, conv_b, w_rg, b_rg, w_ig, b_ig, lam):
    B, T, _ = h.shape
    f32 = jnp.float32
    proj = h @ w_in
    q, k, v, g, xl, yl = jnp.split(proj, list(REC_SPLITS), axis=-1)
    q = q.astype(f32).reshape(B, T, RET_HEADS, RET_DK)
    k = k.astype(f32).reshape(B, T, RET_HEADS, RET_DK) * (RET_DK ** -0.5)
    v = v.astype(f32).reshape(B, T, RET_HEADS, RET_DV)
    o, s_ret = retention(q, k, v, s_ret0.astype(f32))
    ret_out = head_rms(o, ret_g).reshape(B, T, RET_HEADS * RET_DV) * jax.nn.silu(g.astype(f32))
    xc, conv_new = causal_conv(xl, conv0, conv_w, conv_b)
    hl, h_last = rg_lru(xc.astype(f32), w_rg, b_rg, w_ig, b_ig, lam, h_lru0.astype(f32))
    lru_out = hl * jax.nn.gelu(yl.astype(f32))
    mix = jnp.concatenate([ret_out, lru_out], axis=-1).astype(h.dtype)
    return mix @ w_out, s_ret.astype(s_ret0.dtype), h_last.astype(h_lru0.dtype), conv_new


def alibi_slopes():
    return jnp.power(2.0, -8.0 * jnp.arange(1, ATT_HEADS + 1, dtype=jnp.float32) / ATT_HEADS)


def att_qkv(h, w_in, qg, kg):
    B, T, _ = h.shape
    q, k, v = jnp.split((h @ w_in).astype(jnp.float32), 3, axis=-1)
    q = head_rms(q.reshape(B, T, ATT_HEADS, ATT_DH), qg)
    k = head_rms(k.reshape(B, T, ATT_HEADS, ATT_DH), kg)
    v = v.reshape(B, T, ATT_HEADS, ATT_DH)
    return q, k, v


def dilated_band(q, k, v, d, span):
    B, T, H, Dh = q.shape
    L = T // d
    Lp = -(-L // ATT_BLOCK) * ATT_BLOCK
    nb = Lp // ATT_BLOCK
    Z = B * d

    def sub(a):
        a = a.reshape(B, L, d, H, Dh).transpose(0, 2, 1, 3, 4).reshape(Z, L, H, Dh)
        a = jnp.pad(a, ((0, 0), (0, Lp - L), (0, 0), (0, 0)))
        return a.reshape(Z, nb, ATT_BLOCK, H, Dh)

    def with_prev(a):
        prev = jnp.pad(a[:, :-1], ((0, 0), (1, 0), (0, 0), (0, 0), (0, 0)))
        return jnp.concatenate([prev, a], axis=2)

    qb = sub(q)
    kc = with_prev(sub(k))
    vc = with_prev(sub(v))
    s = jnp.einsum('znqhd,znkhd->znhqk', qb, kc) * (Dh ** -0.5)
    qq = jnp.arange(ATT_BLOCK)
    kk = jnp.arange(2 * ATT_BLOCK)
    diff = ATT_BLOCK + qq[:, None] - kk[None, :]
    first = (jnp.arange(nb) == 0)[:, None, None]
    valid = (diff >= 0) & (diff <= span) & (~first | (kk >= ATT_BLOCK)[None, None, :])
    s = s - alibi_slopes()[:, None, None] * (d * diff)
    s = jnp.where(valid[:, None], s, -jnp.inf)
    m = jnp.max(s, axis=-1, keepdims=True)
    p = jnp.exp(s - m)
    den = jnp.sum(p, axis=-1)
    o = jnp.einsum('znhqk,znkhd->znqhd', p, vc) / den.transpose(0, 1, 3, 2)[..., None]
    lse = (m[..., 0] + jnp.log(den)).transpose(0, 1, 3, 2)
    o = o.reshape(Z, Lp, H, Dh)[:, :L].reshape(B, d, L, H, Dh).transpose(0, 2, 1, 3, 4).reshape(B, T, H, Dh)
    lse = lse.reshape(Z, Lp, H)[:, :L].reshape(B, d, L, H).transpose(0, 2, 1, 3).reshape(B, T, H)
    return o, lse


def dilated_gather(q, kall, vall, d, span):
    S = q.shape[1]
    NP = kall.shape[1] - S
    j = jnp.arange(span + 1)
    kidx = NP + jnp.arange(S)[:, None] - d * j[None, :]
    valid = kidx >= 0
    kidx = jnp.maximum(kidx, 0)
    kg = kall[:, kidx]
    vg = vall[:, kidx]
    s = jnp.einsum('bshd,bsjhd->bhsj', q, kg) * (q.shape[-1] ** -0.5)
    s = s - alibi_slopes()[:, None, None] * (d * j)[None, None, :]
    s = jnp.where(valid[None, None], s, -jnp.inf)
    m = jnp.max(s, axis=-1, keepdims=True)
    p = jnp.exp(s - m)
    den = jnp.sum(p, axis=-1)
    o = jnp.einsum('bhsj,bsjhd->bshd', p, vg) / den.transpose(0, 2, 1)[..., None]
    lse = (m[..., 0] + jnp.log(den)).transpose(0, 2, 1)
    return o, lse


def combine_patterns(os, lses):
    w = jax.nn.softmax(jnp.stack(lses, axis=0), axis=0)
    return jnp.sum(w[..., None] * jnp.stack(os, axis=0), axis=0)


def setup_inputs(seed: int = 0) -> dict:
    key = jax.random.key(seed)
    ks = jax.random.split(key, 32)
    f32 = jnp.float32

    def nrm(k, shape, scale):
        return jax.random.normal(k, shape, f32) * scale

    win = min(MAX_WINDOW, PAST_LEN)
    u = jax.random.uniform(ks[14], (N_REC_LAYERS, LRU_WIDTH), f32, minval=0.9, maxval=0.999)
    a0 = jnp.power(u, 1.0 / LRU_C)
    lru_lambda = jnp.log(a0) - jnp.log1p(-a0)
    return {
        "x_prompt": nrm(ks[0], (BATCH, SEQ, D_MODEL), 1.0),
        "x_sample": nrm(ks[1], (DEC_BATCH, DEC_SEQ, D_MODEL), 1.0),
        "state_ret": nrm(ks[2], (N_REC_LAYERS, DEC_BATCH, RET_HEADS, RET_DK, RET_DV), 0.1),
        "state_lru": nrm(ks[3], (N_REC_LAYERS, DEC_BATCH, LRU_WIDTH), 0.5),
        "state_conv": nrm(ks[4], (N_REC_LAYERS, DEC_BATCH, CONV_WIDTH - 1, LRU_WIDTH), 1.0),
        "cache_k": nrm(ks[5], (N_ATT_LAYERS, DEC_BATCH, win, ATT_HEADS, ATT_DH), 1.0),
        "cache_v": nrm(ks[6], (N_ATT_LAYERS, DEC_BATCH, win, ATT_HEADS, ATT_DH), 1.0),
        "norm_mix": 1.0 + nrm(ks[7], (DEPTH, D_MODEL), 0.01),
        "norm_ffn": 1.0 + nrm(ks[8], (DEPTH, D_MODEL), 0.01),
        "w_in_rec": nrm(ks[9], (N_REC_LAYERS, D_MODEL, REC_IN), D_MODEL ** -0.5),
        "w_out_rec": nrm(ks[10], (N_REC_LAYERS, REC_OUT, D_MODEL), REC_OUT ** -0.5),
        "ret_norm_g": 1.0 + nrm(ks[11], (N_REC_LAYERS, RET_HEADS, RET_DV), 0.01),
        "conv_w": nrm(ks[12], (N_REC_LAYERS, CONV_WIDTH, LRU_WIDTH), CONV_WIDTH ** -0.5),
        "conv_b": nrm(ks[13], (N_REC_LAYERS, LRU_WIDTH), 0.01),
        "w_rgate": nrm(ks[15], (N_REC_LAYERS, LRU_BLOCKS, LRU_BLOCK, LRU_BLOCK), LRU_BLOCK ** -0.5),
        "b_rgate": nrm(ks[16], (N_REC_LAYERS, LRU_WIDTH), 0.01),
        "w_igate": nrm(ks[17], (N_REC_LAYERS, LRU_BLOCKS, LRU_BLOCK, LRU_BLOCK), LRU_BLOCK ** -0.5),
        "b_igate": nrm(ks[18], (N_REC_LAYERS, LRU_WIDTH), 0.01),
        "lru_lambda": lru_lambda,
        "w_in_att": nrm(ks[19], (N_ATT_LAYERS, D_MODEL, 3 * ATT_HEADS * ATT_DH), D_MODEL ** -0.5),
        "w_out_att": nrm(ks[20], (N_ATT_LAYERS, ATT_HEADS * ATT_DH, D_MODEL), (ATT_HEADS * ATT_DH) ** -0.5),
        "q_norm_g": 1.0 + nrm(ks[21], (N_ATT_LAYERS, ATT_DH), 0.01),
        "k_norm_g": 1.0 + nrm(ks[22], (N_ATT_LAYERS, ATT_DH), 0.01),
        "w_ffn_gate": nrm(ks[23], (DEPTH, D_MODEL, D_FF), D_MODEL ** -0.5),
        "w_ffn_up": nrm(ks[24], (DEPTH, D_MODEL, D_FF), D_MODEL ** -0.5),
        "w_ffn_down": nrm(ks[25], (DEPTH, D_FF, D_MODEL), D_FF ** -0.5),
    }


def reference(x_prompt, x_sample, state_ret, state_lru, state_conv, cache_k, cache_v,
              norm_mix, norm_ffn, w_in_rec, w_out_rec, ret_norm_g, conv_w, conv_b,
              w_rgate, b_rgate, w_igate, b_igate, lru_lambda,
              w_in_att, w_out_att, q_norm_g, k_norm_g, w_ffn_gate, w_ffn_up, w_ffn_down):
    yp, ys = x_prompt, x_sample
    B, T, _ = x_prompt.shape
    ret_p, lru_p, conv_p, kp, vp = [], [], [], [], []
    ret_s, lru_s, conv_s, ksl, vsl = [], [], [], [], []
    for l in range(DEPTH):
        i = l // 2
        if l % 2 == 0:
            prm = (w_in_rec[i], w_out_rec[i], ret_norm_g[i], conv_w[i], conv_b[i],
                   w_rgate[i], b_rgate[i], w_igate[i], b_igate[i], lru_lambda[i])
            out_p, sr, sl, sc = rec_mixer(
                rmsnorm(yp, norm_mix[l]),
                jnp.zeros((B, RET_HEADS, RET_DK, RET_DV), jnp.float32),
                jnp.zeros((B, LRU_WIDTH), jnp.float32),
                jnp.zeros((B, CONV_WIDTH - 1, LRU_WIDTH), yp.dtype), *prm)
            out_s, sr2, sl2, sc2 = rec_mixer(
                rmsnorm(ys, norm_mix[l]), state_ret[i], state_lru[i], state_conv[i], *prm)
            ret_p.append(sr); lru_p.append(sl); conv_p.append(sc)
            ret_s.append(sr2); lru_s.append(sl2); conv_s.append(sc2)
        else:
            q, k, v = att_qkv(rmsnorm(yp, norm_mix[l]), w_in_att[i], q_norm_g[i], k_norm_g[i])
            res = [dilated_band(q, k, v, d, w // d) for (w, d) in PATTERNS]
            o = combine_patterns([r[0] for r in res], [r[1] for r in res])
            out_p = o.reshape(B, T, ATT_HEADS * ATT_DH).astype(yp.dtype) @ w_out_att[i]
            keep = min(MAX_WINDOW, T)
            kp.append(k[:, T - keep:].astype(cache_k.dtype)); vp.append(v[:, T - keep:].astype(cache_v.dtype))
            qs, kn, vn = att_qkv(rmsnorm(ys, norm_mix[l]), w_in_att[i], q_norm_g[i], k_norm_g[i])
            kall = jnp.concatenate([cache_k[i].astype(jnp.float32), kn], axis=1)
            vall = jnp.concatenate([cache_v[i].astype(jnp.float32), vn], axis=1)
            res = [dilated_gather(qs, kall, vall, d, w // d) for (w, d) in PATTERNS]
            os_ = combine_patterns([r[0] for r in res], [r[1] for r in res])
            out_s = os_.reshape(ys.shape[0], ys.shape[1], ATT_HEADS * ATT_DH).astype(ys.dtype) @ w_out_att[i]
            ksl.append(kn.astype(cache_k.dtype)); vsl.append(vn.astype(cache_v.dtype))
        yp = swiglu_block(yp + out_p, norm_ffn[l], w_ffn_gate[l], w_ffn_up[l], w_ffn_down[l])
        ys = swiglu_block(ys + out_s, norm_ffn[l], w_ffn_gate[l], w_ffn_up[l], w_ffn_down[l])
    return (yp, ys,
            jnp.stack(ret_p), jnp.stack(lru_p), jnp.stack(conv_p), jnp.stack(kp), jnp.stack(vp),
            jnp.stack(ret_s), jnp.stack(lru_s), jnp.stack(conv_s), jnp.stack(ksl), jnp.stack(vsl))
```

```python
import functools

import jax
import jax.numpy as jnp
from jax import lax
from jax.experimental import pallas as pl
from jax.experimental.pallas import tpu as pltpu

F32 = jnp.float32
BF16 = jnp.bfloat16

EPS = 1e-6
D_MODEL = 1024
RET_HEADS = 4
RET_DK = 64
RET_DV = 128
RET_CHUNK = 128
LRU_WIDTH = 512
LRU_BLOCKS = 4
LRU_BLOCK = LRU_WIDTH // LRU_BLOCKS
CONV_WIDTH = 4
LRU_C = 8.0
REC_IN = 2 * RET_HEADS * RET_DK + 2 * RET_HEADS * RET_DV + 2 * LRU_WIDTH
ATT_HEADS = 16
ATT_DH = 64
PATTERNS = ((128, 1), (512, 4), (2048, 16))
ATT_BLOCK = 128
SUPER = 2048
MAX_DIL = 16
NEG = -1e30

LANES = 128
SUBLANES = 8
VMEM_LIMIT = 56 * 1024 * 1024

FF_CHUNK = 256
TOKEN_TILE = 512


def _cparams(sem):
    return pltpu.CompilerParams(dimension_semantics=sem, vmem_limit_bytes=VMEM_LIMIT)


def _const_spec(shape):
    nd = len(shape)
    return pl.BlockSpec(shape, lambda *_: (0,) * nd, pipeline_mode=pl.Buffered(1))


def _rmsnorm(x, g):
    return x * lax.rsqrt(jnp.mean(x * x, axis=-1, keepdims=True) + EPS) * g


def _load_rows(ref, nsl):
    if nsl == 1:
        return ref[...]
    return jnp.concatenate([ref[:, i * D_MODEL:(i + 1) * D_MODEL] for i in range(nsl)], axis=0)


def _store_rows(ref, val, nsl):
    if nsl == 1:
        ref[...] = val
        return
    rows = val.shape[0] // nsl
    for i in range(nsl):
        ref[:, i * D_MODEL:(i + 1) * D_MODEL] = val[i * rows:(i + 1) * rows]


def _norm_proj_kernel(x_ref, g_ref, w_ref, o_ref):
    h = _rmsnorm(x_ref[...], g_ref[...]).astype(BF16)
    o_ref[...] = jnp.dot(h, w_ref[...], preferred_element_type=F32)


def _norm_proj(x2d, g, w_bf):
    m, d = x2d.shape
    n = w_bf.shape[1]
    tm = min(TOKEN_TILE, m)
    return pl.pallas_call(
        _norm_proj_kernel,
        grid=(m // tm,),
        in_specs=[pl.BlockSpec((tm, d), lambda i: (i, 0)), _const_spec((1, d)), _const_spec((d, n))],
        out_specs=pl.BlockSpec((tm, n), lambda i: (i, 0)),
        out_shape=jax.ShapeDtypeStruct((m, n), F32),
        compiler_params=_cparams(("arbitrary",)),
        name="norm_proj",
    )(x2d, g.reshape(1, d), w_bf)


def _shift_rows(x, s, fill):
    rows = lax.broadcasted_iota(jnp.int32, x.shape, 0)
    return jnp.where(rows < s, fill, pltpu.roll(x, s, axis=0))


def _pad_rows(x, rows):
    if x.shape[0] == rows:
        return x
    return jnp.concatenate([x, jnp.zeros((rows - x.shape[0], x.shape[1]), x.dtype)], axis=0)


def _rec_mix_kernel(proj_ref, s0_ref, h0_ref, c0_ref, dec_ref, qdec_ref, kdec_ref, cdec_ref,
                    retg_ref, convw_ref, convb_ref, wrg_ref, brg_ref, wig_ref, big_ref, lam_ref,
                    mix_ref, sret_ref, hlast_ref, cnew_ref,
                    s_scr, h_scr, xe_scr, *, rows):
    c = pl.program_id(1)
    cp = RET_CHUNK
    hk = RET_HEADS * RET_DK
    v_off, g_off = 2 * hk, 2 * hk + RET_HEADS * RET_DV
    xl_off = g_off + RET_HEADS * RET_DV
    yl_off = xl_off + LRU_WIDTH

    @pl.when(c == 0)
    def _():
        s_scr[...] = s0_ref[...]
        h_scr[...] = h0_ref[...]
        xe_scr[0:SUBLANES, :] = jnp.zeros((SUBLANES, LRU_WIDTH), F32)
        xe_scr[SUBLANES - (CONV_WIDTH - 1):SUBLANES, :] = c0_ref[...]

    lane = lax.broadcasted_iota(jnp.int32, (cp, LANES), 1)
    first = lane < RET_DK
    srow_first = lax.broadcasted_iota(jnp.int32, (2 * RET_DK, RET_DV), 0) < RET_DK

    for p in range(RET_HEADS // 2):
        q2 = _pad_rows(proj_ref[:, p * LANES:(p + 1) * LANES], cp)
        k2 = _pad_rows(proj_ref[:, hk + p * LANES:hk + (p + 1) * LANES], cp) * (RET_DK ** -0.5)
        kb = k2.astype(BF16)
        kdt = jnp.transpose(k2 * kdec_ref[p]).astype(BF16)
        s2 = s_scr[p]
        s2b = s2.astype(BF16)
        new_s = []
        for hh in range(2):
            h = 2 * p + hh
            sel = first if hh == 0 else jnp.logical_not(first)
            qh = jnp.where(sel, q2, 0.0)
            v = _pad_rows(proj_ref[:, v_off + h * RET_DV:v_off + (h + 1) * RET_DV], cp)
            vb = v.astype(BF16)
            att = lax.dot_general(qh.astype(BF16), kb, (((1,), (1,)), ((), ())),
                                  preferred_element_type=F32) * dec_ref[h]
            o = (jnp.dot(att.astype(BF16), vb, preferred_element_type=F32)
                 + jnp.dot((qh * qdec_ref[p]).astype(BF16), s2b, preferred_element_type=F32))
            new_s.append(jnp.dot(kdt, vb, preferred_element_type=F32))
            o = o[:rows]
            o = o * lax.rsqrt(jnp.mean(o * o, axis=-1, keepdims=True) + EPS) * retg_ref[h:h + 1, :]
            gate = proj_ref[:, g_off + h * RET_DV:g_off + (h + 1) * RET_DV]
            mix_ref[:, h * RET_DV:(h + 1) * RET_DV] = (o * (gate * jax.nn.sigmoid(gate))).astype(mix_ref.dtype)
        s_scr[p] = s2 * cdec_ref[p] + jnp.where(srow_first, new_s[0], new_s[1])

    x = proj_ref[:, xl_off:xl_off + LRU_WIDTH]
    xe_scr[SUBLANES:SUBLANES + rows, :] = x
    xc = convb_ref[...] + xe_scr[pl.ds(SUBLANES - 3, rows), :] * convw_ref[0:1, :]
    for j in range(1, CONV_WIDTH):
        xc = xc + xe_scr[pl.ds(SUBLANES - 3 + j, rows), :] * convw_ref[j:j + 1, :]
    tail = xe_scr[rows:rows + SUBLANES, :]
    xe_scr[0:SUBLANES, :] = tail
    cnew_ref[...] = xe_scr[pl.ds(SUBLANES - (CONV_WIDTH - 1), CONV_WIDTH - 1), :]

    xcb = _pad_rows(xc, max(rows, 2 * SUBLANES)).astype(BF16)
    r_parts, i_parts = [], []
    for n in range(LRU_BLOCKS):
        xb = xcb[:, n * LRU_BLOCK:(n + 1) * LRU_BLOCK]
        r_parts.append(jnp.dot(xb, wrg_ref[n], preferred_element_type=F32)[:rows])
        i_parts.append(jnp.dot(xb, wig_ref[n], preferred_element_type=F32)[:rows])
    r = jax.nn.sigmoid(jnp.concatenate(r_parts, axis=1) + brg_ref[...])
    ig = jax.nn.sigmoid(jnp.concatenate(i_parts, axis=1) + big_ref[...])
    nl = -lam_ref[...]
    softplus = jnp.maximum(nl, 0.0) + jnp.log1p(jnp.exp(-jnp.abs(nl)))
    log_a = (-LRU_C) * r * softplus
    a = jnp.exp(log_a)
    b = jnp.sqrt(1.0 - jnp.exp(2.0 * log_a)) * (ig * xc)

    s = 1
    while s < rows:
        b = a * _shift_rows(b, s, 0.0) + b
        a = a * _shift_rows(a, s, 1.0)
        s *= 2
    hseq = b + a * h_scr[...]
    h_scr[...] = hseq[rows - 1:rows, :]
    hlast_ref[...] = hseq[rows - 1:rows, :]
    yl = proj_ref[:, yl_off:yl_off + LRU_WIDTH]
    gelu = 0.5 * yl * (1.0 + jnp.tanh(0.7978845608028654 * (yl + 0.044715 * (yl * yl * yl))))
    mix_ref[:, RET_HEADS * RET_DV:] = (hseq * gelu).astype(mix_ref.dtype)
    sret_ref[...] = s_scr[...]


def _ret_tables(t_eff):
    cp = RET_CHUNK
    log_g = jnp.log1p(-jnp.power(2.0, -5.0 - jnp.arange(RET_HEADS, dtype=F32)))
    idx = jnp.arange(cp, dtype=F32)
    live = idx < t_eff
    diff = idx[:, None] - idx[None, :]
    ok = (diff >= 0) & live[:, None] & live[None, :]
    decay = jnp.where(ok, jnp.exp(log_g[:, None, None] * jnp.maximum(diff, 0.0)), 0.0)
    q_dec = jnp.where(live, jnp.exp(log_g[:, None] * (idx + 1.0)), 0.0)
    k_dec = jnp.where(live, jnp.exp(log_g[:, None] * (t_eff - 1.0 - idx)), 0.0)
    c_dec = jnp.exp(log_g * t_eff)

    def pair_lanes(t):
        t = t.reshape(RET_HEADS // 2, 2, cp)
        return jnp.repeat(jnp.transpose(t, (0, 2, 1)), RET_DK, axis=2)

    c_rows = jnp.repeat(c_dec.reshape(RET_HEADS // 2, 2), RET_DK, axis=1)
    c_rows = jnp.broadcast_to(c_rows[:, :, None], (RET_HEADS // 2, 2 * RET_DK, RET_DV))
    return decay, pair_lanes(q_dec), pair_lanes(k_dec), c_rows


def _rec_mix(proj, s0, h0, c0, ret_g, conv_w, conv_b, w_rg, b_rg, w_ig, b_ig, lam):
    b, t, _ = proj.shape
    rows = RET_CHUNK if t % RET_CHUNK == 0 else t
    assert t % rows == 0 and rows % SUBLANES == 0 and (t == rows or rows == RET_CHUNK)
    n = t // rows
    dec, qdec, kdec, cdec = _ret_tables(rows)
    hp = RET_HEADS // 2
    w = LRU_WIDTH
    per_b3 = lambda i, c: (i, 0, 0)
    outs = pl.pallas_call(
        functools.partial(_rec_mix_kernel, rows=rows),
        grid=(b, n),
        in_specs=[
            pl.BlockSpec((None, rows, REC_IN), lambda i, c: (i, c, 0)),
            pl.BlockSpec((None, hp, 2 * RET_DK, RET_DV), lambda i, c: (i, 0, 0, 0)),
            pl.BlockSpec((None, 1, w), per_b3),
            pl.BlockSpec((None, CONV_WIDTH - 1, w), per_b3),
            _const_spec(dec.shape), _const_spec(qdec.shape), _const_spec(kdec.shape), _const_spec(cdec.shape),
            _const_spec((RET_HEADS, RET_DV)), _const_spec((CONV_WIDTH, w)), _const_spec((1, w)),
            _const_spec((LRU_BLOCKS, LRU_BLOCK, LRU_BLOCK)), _const_spec((1, w)),
            _const_spec((LRU_BLOCKS, LRU_BLOCK, LRU_BLOCK)), _const_spec((1, w)), _const_spec((1, w)),
        ],
        out_specs=[
            pl.BlockSpec((None, rows, D_MODEL), lambda i, c: (i, c, 0)),
            pl.BlockSpec((None, hp, 2 * RET_DK, RET_DV), lambda i, c: (i, 0, 0, 0)),
            pl.BlockSpec((None, 1, w), per_b3),
            pl.BlockSpec((None, CONV_WIDTH - 1, w), per_b3),
        ],
        out_shape=[
            jax.ShapeDtypeStruct((b, t, D_MODEL), BF16),
            jax.ShapeDtypeStruct((b, hp, 2 * RET_DK, RET_DV), F32),
            jax.ShapeDtypeStruct((b, 1, w), F32),
            jax.ShapeDtypeStruct((b, CONV_WIDTH - 1, w), F32),
        ],
        scratch_shapes=[
            pltpu.VMEM((hp, 2 * RET_DK, RET_DV), F32),
            pltpu.VMEM((1, w), F32),
            pltpu.VMEM((rows + SUBLANES, w), F32),
        ],
        compiler_params=_cparams(("arbitrary", "arbitrary")),
        name="rec_mix",
    )(proj, s0.reshape(b, hp, 2 * RET_DK, RET_DV), h0.reshape(b, 1, w), c0,
      dec, qdec, kdec, cdec, ret_g, conv_w, conv_b.reshape(1, w),
      w_rg.astype(BF16), b_rg.reshape(1, w), w_ig.astype(BF16), b_ig.reshape(1, w), lam.reshape(1, w))
    mix, s_ret, h_last, c_new = outs
    return mix, s_ret.reshape(b, RET_HEADS, RET_DK, RET_DV), h_last.reshape(b, w), c_new


def _out_ffn_kernel(mix_ref, x_ref, wo_ref, g_ref, wg_ref, wu_ref, wd_ref, o_ref, *, nsl):
    x = _load_rows(x_ref, nsl)
    y = x + jnp.dot(mix_ref[...], wo_ref[...], preferred_element_type=F32)
    h = _rmsnorm(y, g_ref[...]).astype(BF16)

    def body(c, acc):
        gate = jnp.dot(h, wg_ref[c], preferred_element_type=F32)
        up = jnp.dot(h, wu_ref[c], preferred_element_type=F32)
        act = (gate * jax.nn.sigmoid(gate) * up).astype(BF16)
        return acc + jnp.dot(act, wd_ref[c], preferred_element_type=F32)

    ff = lax.fori_loop(0, wg_ref.shape[0], body, jnp.zeros_like(y))
    _store_rows(o_ref, y + ff, nsl)


def _ffn_weights(w_out, g, w_gate, w_up, w_down):
    d, f = w_gate.shape
    nc = f // FF_CHUNK
    col_chunks = lambda w: jnp.transpose(w.astype(BF16).reshape(d, nc, FF_CHUNK), (1, 0, 2))
    return (w_out.astype(BF16), g.reshape(1, d), col_chunks(w_gate), col_chunks(w_up),
            w_down.astype(BF16).reshape(nc, FF_CHUNK, d))


def _ffn_specs(wts):
    return [_const_spec(w.shape) for w in wts]


def _out_ffn(mix2d, x2d, wts):
    m, d = x2d.shape
    tm = min(TOKEN_TILE, m)
    row = lambda i: (i, 0)
    return pl.pallas_call(
        functools.partial(_out_ffn_kernel, nsl=1),
        grid=(m // tm,),
        in_specs=[pl.BlockSpec((tm, d), row), pl.BlockSpec((tm, d), row)] + _ffn_specs(wts),
        out_specs=pl.BlockSpec((tm, d), row),
        out_shape=jax.ShapeDtypeStruct((m, d), F32),
        compiler_params=_cparams(("arbitrary",)),
        name="out_ffn",
    )(mix2d, x2d, *wts)


PERM_SL = 4


def _out_ffn_perm(mix, x, wts):
    b, t, d = x.shape
    ns = t // SUPER
    xv = x.reshape(b, t // MAX_DIL, MAX_DIL * d)
    tm = PERM_SL * ATT_BLOCK
    nq = MAX_DIL // PERM_SL
    view = lambda i, s, r: (i, s, r)
    out = pl.pallas_call(
        functools.partial(_out_ffn_kernel, nsl=PERM_SL),
        grid=(b, ns, nq),
        in_specs=[pl.BlockSpec((None, tm, d), lambda i, s, r: (i, s * nq + r, 0)),
                  pl.BlockSpec((None, ATT_BLOCK, PERM_SL * d), view)] + _ffn_specs(wts),
        out_specs=pl.BlockSpec((None, ATT_BLOCK, PERM_SL * d), view),
        out_shape=jax.ShapeDtypeStruct(xv.shape, F32),
        compiler_params=_cparams(("arbitrary",) * 3),
        name="out_ffn_perm",
    )(mix, xv, *wts)
    return out.reshape(b, t, d)


def _pair_rms(a, g2, first):
    sq = a * a
    s_a = jnp.sum(jnp.where(first, sq, 0.0), axis=-1, keepdims=True)
    s_b = jnp.sum(jnp.where(first, 0.0, sq), axis=-1, keepdims=True)
    ms = jnp.where(first, s_a, s_b) * (1.0 / ATT_DH)
    return a * lax.rsqrt(ms + EPS) * g2


def _qkv_kernel(x_ref, g_ref, w_ref, qg_ref, kg_ref, *out_refs, nsl, pair_major):
    x = _load_rows(x_ref, nsl)
    h = _rmsnorm(x, g_ref[...]).astype(BF16)
    qkv = jnp.dot(h, w_ref[...], preferred_element_type=F32)
    rows = x.shape[0]
    first = lax.broadcasted_iota(jnp.int32, (rows, LANES), 1) < ATT_DH
    npair = D_MODEL // LANES
    q = [_pair_rms(qkv[:, p * LANES:(p + 1) * LANES], qg_ref[...], first) for p in range(npair)]
    k = [_pair_rms(qkv[:, D_MODEL + p * LANES:D_MODEL + (p + 1) * LANES], kg_ref[...], first) for p in range(npair)]
    v = [qkv[:, 2 * D_MODEL + p * LANES:2 * D_MODEL + (p + 1) * LANES] for p in range(npair)]
    if pair_major:
        q_ref, k_ref, v_ref, kc_ref, vc_ref = out_refs
        for p in range(npair):
            q_ref[p] = q[p]
            k_ref[p] = k[p]
            v_ref[p] = v[p]
        _store_rows(kc_ref, jnp.concatenate(k, axis=1), nsl)
        _store_rows(vc_ref, jnp.concatenate(v, axis=1), nsl)
    else:
        q_ref, k_ref, v_ref = out_refs
        q_ref[...] = jnp.concatenate(q, axis=1)
        k_ref[...] = jnp.concatenate(k, axis=1)
        v_ref[...] = jnp.concatenate(v, axis=1)


def _head_gain(g):
    return jnp.tile(g.reshape(1, ATT_DH), (1, LANES // ATT_DH))


def _qkv_flat(x2d, g, w_bf, qg, kg):
    m, d = x2d.shape
    tm = min(TOKEN_TILE, m)
    row = lambda i: (i, 0)
    return pl.pallas_call(
        functools.partial(_qkv_kernel, nsl=1, pair_major=False),
        grid=(m // tm,),
        in_specs=[pl.BlockSpec((tm, d), row), _const_spec((1, d)), _const_spec(w_bf.shape),
                  _const_spec((1, LANES)), _const_spec((1, LANES))],
        out_specs=[pl.BlockSpec((tm, d), row)] * 3,
        out_shape=[jax.ShapeDtypeStruct((m, d), F32)] * 3,
        compiler_params=_cparams(("arbitrary",)),
        name="qkv_flat",
    )(x2d, g.reshape(1, d), w_bf, _head_gain(qg), _head_gain(kg))


def _qkv_perm(x, g, w_bf, qg, kg):
    b, t, d = x.shape
    ns = t // SUPER
    xv = x.reshape(b, t // MAX_DIL, MAX_DIL * d)
    tm = PERM_SL * ATT_BLOCK
    nq = MAX_DIL // PERM_SL
    npair = d // LANES
    pm_spec = pl.BlockSpec((None, npair, tm, LANES), lambda i, s, r: (i, 0, s * nq + r, 0))
    cache_spec = pl.BlockSpec((None, ATT_BLOCK, PERM_SL * d), lambda i, s, r: (i, s, r))
    pm_shape = jax.ShapeDtypeStruct((b, npair, t, LANES), F32)
    cache_shape = jax.ShapeDtypeStruct(xv.shape, F32)
    q, k, v, kc, vc = pl.pallas_call(
        functools.partial(_qkv_kernel, nsl=PERM_SL, pair_major=True),
        grid=(b, ns, nq),
        in_specs=[pl.BlockSpec((None, ATT_BLOCK, PERM_SL * d), lambda i, s, r: (i, s, r)),
                  _const_spec((1, d)), _const_spec(w_bf.shape), _const_spec((1, LANES)), _const_spec((1, LANES))],
        out_specs=[pm_spec, pm_spec, pm_spec, cache_spec, cache_spec],
        out_shape=[pm_shape, pm_shape, pm_shape, cache_shape, cache_shape],
        compiler_params=_cparams(("arbitrary",) * 3),
        name="qkv_perm",
    )(xv, g.reshape(1, d), w_bf, _head_gain(qg), _head_gain(kg))
    return q, k, v, kc.reshape(b, t, d)[:, t - SUPER:], vc.reshape(b, t, d)[:, t - SUPER:]


def _alibi_slopes():
    return jnp.power(2.0, -8.0 * jnp.arange(1, ATT_HEADS + 1, dtype=F32) / ATT_HEADS)


def _band_bias():
    i = jnp.arange(ATT_BLOCK)
    orders = {1: MAX_DIL * (i % SUBLANES) + i // SUBLANES,
              4: 4 * (i % 32) + i // 32,
              16: i}
    out = []
    for (w, d) in PATTERNS:
        span = w // d
        m = orders[d]
        km = jnp.concatenate([m - ATT_BLOCK, m])
        diff = m[:, None] - km[None, :]
        valid = (diff >= 0) & (diff <= span)
        bias = -_alibi_slopes()[:, None, None] * (d * diff).astype(F32)[None]
        out.append(jnp.where(valid[None], bias, NEG))
    return jnp.stack(out)


def _attend_pair(q, k, v, bias_a, bias_b, first):
    kb = k.astype(BF16)
    vb = v.astype(BF16)
    res = []
    for qh, bias in ((jnp.where(first, q, 0.0), bias_a), (jnp.where(first, 0.0, q), bias_b)):
        s = lax.dot_general(qh.astype(BF16), kb, (((1,), (1,)), ((), ())), preferred_element_type=F32) + bias
        m = jnp.max(s, axis=-1, keepdims=True)
        e = jnp.exp(s - m)
        den = jnp.sum(e, axis=-1, keepdims=True)
        o = jnp.dot(e.astype(BF16), vb, preferred_element_type=F32) / den
        res.append((o, m + jnp.log(den)))
    o = jnp.where(first, res[0][0], res[1][0])
    lse = jnp.where(first, res[0][1], res[1][1])
    return o, lse


def _attn_prompt_kernel(q_ref, kc_ref, kp_ref, vc_ref, vp_ref, bias_ref, o_ref, oacc, lacc):
    sb = pl.program_id(1)
    neg_first = jnp.where(sb == 0, NEG, 0.0).astype(F32)
    first = lax.broadcasted_iota(jnp.int32, (ATT_BLOCK, LANES), 1) < ATT_DH
    prev_half = lax.broadcasted_iota(jnp.int32, (ATT_BLOCK, 2 * ATT_BLOCK), 1) < ATT_BLOCK
    scale = ATT_DH ** -0.5

    def gather(ref, starts, n):
        return jnp.concatenate([ref[pl.ds(st, n), :] for st in starts], axis=0)

    def run_block(p, q_starts, prev_ref, prev_starts, n, masked_prev):
        q = gather(q_ref, q_starts, n) * scale
        k = jnp.concatenate([gather(prev_ref, prev_starts, n), gather(kc_ref, q_starts, n)], axis=0)
        vprev_ref = vp_ref if prev_ref is kp_ref else vc_ref
        v = jnp.concatenate([gather(vprev_ref, prev_starts, n), gather(vc_ref, q_starts, n)], axis=0)
        ba, bb = bias_ref[p, 0], bias_ref[p, 1]
        if masked_prev:
            off = jnp.where(prev_half, neg_first, 0.0)
            ba, bb = ba + off, bb + off
        o, lse = _attend_pair(q, k, v, ba, bb, first)
        for ci, st in enumerate(q_starts):
            oacc[p, pl.ds(st, n), :] = o[ci * n:(ci + 1) * n]
            lacc[p, pl.ds(st, n), :] = lse[ci * n:(ci + 1) * n]

    n0 = ATT_BLOCK // MAX_DIL
    run_block(0, [r * ATT_BLOCK for r in range(MAX_DIL)], kp_ref,
              [r * ATT_BLOCK + ATT_BLOCK - n0 for r in range(MAX_DIL)], n0, True)

    def body0(bi, carry):
        base = pl.multiple_of(bi * n0, n0)
        run_block(0, [r * ATT_BLOCK + base for r in range(MAX_DIL)], kc_ref,
                  [r * ATT_BLOCK + base - n0 for r in range(MAX_DIL)], n0, False)
        return carry

    lax.fori_loop(1, MAX_DIL, body0, 0)

    d1 = PATTERNS[1][1]
    n1 = ATT_BLOCK // (MAX_DIL // d1)
    for r4 in range(d1):
        chunk_rows = [(r4 + d1 * c) * ATT_BLOCK for c in range(MAX_DIL // d1)]
        run_block(1, chunk_rows, kp_ref, [st + ATT_BLOCK - n1 for st in chunk_rows], n1, True)

        def body1(mb, carry, chunk_rows=chunk_rows):
            base = pl.multiple_of(mb * n1, n1)
            run_block(1, [st + base for st in chunk_rows], kc_ref,
                      [st + base - n1 for st in chunk_rows], n1, False)
            return carry

        lax.fori_loop(1, ATT_BLOCK // n1, body1, 0)

    def body2(r, carry):
        st = pl.multiple_of(r * ATT_BLOCK, ATT_BLOCK)
        run_block(2, [st], kp_ref, [st], ATT_BLOCK, True)
        return carry

    lax.fori_loop(0, MAX_DIL, body2, 0)

    def comb(r, carry):
        sl = pl.ds(pl.multiple_of(r * ATT_BLOCK, ATT_BLOCK), ATT_BLOCK)
        l0, l1, l2 = lacc[0, sl, :], lacc[1, sl, :], lacc[2, sl, :]
        mx = jnp.maximum(jnp.maximum(l0, l1), l2)
        w0, w1, w2 = jnp.exp(l0 - mx), jnp.exp(l1 - mx), jnp.exp(l2 - mx)
        num = w0 * oacc[0, sl, :] + w1 * oacc[1, sl, :] + w2 * oacc[2, sl, :]
        o_ref[sl, :] = (num / (w0 + w1 + w2)).astype(o_ref.dtype)
        return carry

    lax.fori_loop(0, SUPER // ATT_BLOCK, comb, 0)


def _attn_prompt(q, k, v):
    b, npair, t, _ = q.shape
    ns = t // SUPER
    bias = _band_bias()
    cur = pl.BlockSpec((None, None, SUPER, LANES), lambda i, s, p: (i, p, s, 0))
    prev = pl.BlockSpec((None, None, SUPER, LANES), lambda i, s, p: (i, p, jnp.maximum(s - 1, 0), 0))
    return pl.pallas_call(
        _attn_prompt_kernel,
        grid=(b, ns, npair),
        in_specs=[cur, cur, prev, cur, prev,
                  pl.BlockSpec((len(PATTERNS), 2, ATT_BLOCK, 2 * ATT_BLOCK), lambda i, s, p: (0, p, 0, 0))],
        out_specs=pl.BlockSpec((None, SUPER, LANES), lambda i, s, p: (i, s, p)),
        out_shape=jax.ShapeDtypeStruct((b, t, npair * LANES), BF16),
        scratch_shapes=[pltpu.VMEM((len(PATTERNS), SUPER, LANES), F32),
                        pltpu.VMEM((len(PATTERNS), SUPER, LANES), F32)],
        compiler_params=_cparams(("arbitrary",) * 3),
        name="attn_prompt",
    )(q, k, k, v, v, bias)


QUAD = 4
NEW_PAD = 128


def _sample_bias(s_len, n_past):
    s = jnp.arange(s_len)
    col = jnp.arange(n_past + NEW_PAD)
    pos = jnp.where(col < n_past, col, n_past + (col - n_past))
    dist = (n_past + s)[:, None] - pos[None, :]
    real = (col < n_past + s_len)[None, :]
    slopes = _alibi_slopes().reshape(ATT_HEADS // QUAD, QUAD, 1, 1)
    out = []
    for (w, d) in PATTERNS:
        valid = (dist >= 0) & (dist % d == 0) & (dist <= w) & real
        bias = jnp.where(valid[None, None], -slopes * dist.astype(F32)[None, None], NEG)
        out.append(bias.reshape(ATT_HEADS // QUAD, QUAD * s_len, n_past + NEW_PAD))
    return jnp.stack(out)


def _attn_sample_kernel(q_ref, kn_ref, vn_ref, ck_ref, cv_ref, bias_ref, o_ref, *, s_len):
    n_past = ck_ref.shape[1]
    width = QUAD * ATT_DH
    rows = QUAD * s_len
    q = q_ref[...] * (ATT_DH ** -0.5)
    q4 = jnp.concatenate([q] * QUAD, axis=0)
    own = (lax.broadcasted_iota(jnp.int32, (rows, width), 0) // s_len
           == lax.broadcasted_iota(jnp.int32, (rows, width), 1) // ATT_DH)
    qm = jnp.where(own, q4, 0.0).astype(BF16)
    nt = (((1,), (1,)), ((), ()))
    kn = _pad_rows(kn_ref[...], NEW_PAD).astype(BF16)
    vn = _pad_rows(vn_ref[...], NEW_PAD).astype(BF16)
    s_c = jnp.dot(qm, ck_ref[...].astype(BF16), preferred_element_type=F32)
    s_n = lax.dot_general(qm, kn, nt, preferred_element_type=F32)
    es_c, es_n, dens, lses = [], [], [], []
    for p in range(len(PATTERNS)):
        b_c = bias_ref[p, :, 0:n_past]
        b_n = bias_ref[p, :, n_past:n_past + NEW_PAD]
        sc, sn = s_c + b_c, s_n + b_n
        m = jnp.maximum(jnp.max(sc, axis=-1, keepdims=True), jnp.max(sn, axis=-1, keepdims=True))
        ec, en = jnp.exp(sc - m), jnp.exp(sn - m)
        den = jnp.sum(ec, axis=-1, keepdims=True) + jnp.sum(en, axis=-1, keepdims=True)
        es_c.append(ec)
        es_n.append(en)
        dens.append(den)
        lses.append(m + jnp.log(den))
    mx = jnp.maximum(jnp.maximum(lses[0], lses[1]), lses[2])
    ws = [jnp.exp(l - mx) for l in lses]
    wsum = ws[0] + ws[1] + ws[2]
    coef = [w / (wsum * d) for w, d in zip(ws, dens)]
    c_c = coef[0] * es_c[0] + coef[1] * es_c[1] + coef[2] * es_c[2]
    c_n = coef[0] * es_n[0] + coef[1] * es_n[1] + coef[2] * es_n[2]
    o = (lax.dot_general(c_c.astype(BF16), cv_ref[...].astype(BF16), nt, preferred_element_type=F32)
         + jnp.dot(c_n.astype(BF16), vn, preferred_element_type=F32))
    o = jnp.where(own, o, 0.0)
    acc = o[0:s_len]
    for hh in range(1, QUAD):
        acc = acc + o[hh * s_len:(hh + 1) * s_len]
    o_ref[...] = acc.astype(o_ref.dtype)


def _attn_sample(q, kn, vn, ck, cv):
    b, s_len, d = q.shape
    n_past = ck.shape[2]
    width = QUAD * ATT_DH
    nquad = d // width
    bias = _sample_bias(s_len, n_past)
    new = pl.BlockSpec((None, s_len, width), lambda i, h: (i, 0, h))
    old = pl.BlockSpec((None, width, n_past), lambda i, h: (i, h, 0))
    return pl.pallas_call(
        functools.partial(_attn_sample_kernel, s_len=s_len),
        grid=(b, nquad),
        in_specs=[new, new, new, old, old,
                  pl.BlockSpec((len(PATTERNS), None, QUAD * s_len, n_past + NEW_PAD), lambda i, h: (0, h, 0, 0))],
        out_specs=new,
        out_shape=jax.ShapeDtypeStruct((b, s_len, d), BF16),
        compiler_params=_cparams(("arbitrary",) * 2),
        name="attn_sample",
    )(q, kn, vn, ck, cv, bias)


def kernel(x_prompt, x_sample, state_ret, state_lru, state_conv, cache_k, cache_v, norm_mix, norm_ffn, w_in_rec, w_out_rec, ret_norm_g, conv_w, conv_b, w_rgate, b_rgate, w_igate, b_igate, lru_lambda, w_in_att, w_out_att, q_norm_g, k_norm_g, w_ffn_gate, w_ffn_up, w_ffn_down):
    bp, tp, d = x_prompt.shape
    bs, ts, _ = x_sample.shape
    depth = norm_mix.shape[0]
    assert tp % SUPER == 0 and d == D_MODEL
    yp, ys = x_prompt, x_sample
    ret_p, lru_p, conv_p, kp, vp = [], [], [], [], []
    ret_s, lru_s, conv_s, ksl, vsl = [], [], [], [], []
    for l in range(depth):
        i = l // 2
        if l % 2 == 0:
            ffn = _ffn_weights(w_out_rec[i], norm_ffn[l], w_ffn_gate[l], w_ffn_up[l], w_ffn_down[l])
            w_in = w_in_rec[i].astype(BF16)
            rec_prm = (ret_norm_g[i], conv_w[i], conv_b[i], w_rgate[i], b_rgate[i], w_igate[i], b_igate[i],
                       lru_lambda[i])
            proj = _norm_proj(yp.reshape(bp * tp, d), norm_mix[l], w_in).reshape(bp, tp, REC_IN)
            mix, sr, sl, sc = _rec_mix(proj, jnp.zeros((bp, RET_HEADS, RET_DK, RET_DV), F32),
                                       jnp.zeros((bp, LRU_WIDTH), F32),
                                       jnp.zeros((bp, CONV_WIDTH - 1, LRU_WIDTH), F32), *rec_prm)
            yp = _out_ffn(mix.reshape(bp * tp, d), yp.reshape(bp * tp, d), ffn).reshape(bp, tp, d)
            ret_p.append(sr); lru_p.append(sl); conv_p.append(sc)
            proj = _norm_proj(ys.reshape(bs * ts, d), norm_mix[l], w_in).reshape(bs, ts, REC_IN)
            mix, sr, sl, sc = _rec_mix(proj, state_ret[i], state_lru[i], state_conv[i], *rec_prm)
            ys = _out_ffn(mix.reshape(bs * ts, d), ys.reshape(bs * ts, d), ffn).reshape(bs, ts, d)
            ret_s.append(sr); lru_s.append(sl); conv_s.append(sc)
        else:
            ffn = _ffn_weights(w_out_att[i], norm_ffn[l], w_ffn_gate[l], w_ffn_up[l], w_ffn_down[l])
            w_in = w_in_att[i].astype(BF16)
            q, k, v, kc, vc = _qkv_perm(yp, norm_mix[l], w_in, q_norm_g[i], k_norm_g[i])
            o = _attn_prompt(q, k, v)
            yp = _out_ffn_perm(o, yp, ffn)
            kp.append(kc.reshape(bp, SUPER, ATT_HEADS, ATT_DH)); vp.append(vc.reshape(bp, SUPER, ATT_HEADS, ATT_DH))
            qs, kn, vn = _qkv_flat(ys.reshape(bs * ts, d), norm_mix[l], w_in, q_norm_g[i], k_norm_g[i])
            n_past = cache_k.shape[2]
            to_slabs = lambda c: jnp.transpose(c, (0, 2, 3, 1)).reshape(bs, d, n_past)
            o = _attn_sample(qs.reshape(bs, ts, d), kn.reshape(bs, ts, d), vn.reshape(bs, ts, d),
                             to_slabs(cache_k[i]), to_slabs(cache_v[i]))
            ys = _out_ffn(o.reshape(bs * ts, d), ys.reshape(bs * ts, d), ffn).reshape(bs, ts, d)
            ksl.append(kn.reshape(bs, ts, ATT_HEADS, ATT_DH)); vsl.append(vn.reshape(bs, ts, ATT_HEADS, ATT_DH))
    return (yp, ys,
            jnp.stack(ret_p), jnp.stack(lru_p), jnp.stack(conv_p), jnp.stack(kp), jnp.stack(vp),
            jnp.stack(ret_s), jnp.stack(lru_s), jnp.stack(conv_s), jnp.stack(ksl), jnp.stack(vsl))
```

```python
import functools

import jax
import jax.numpy as jnp
from jax import lax
from jax.experimental import pallas as pl
from jax.experimental.pallas import tpu as pltpu

F32 = jnp.float32
BF16 = jnp.bfloat16

EPS = 1e-6
D_MODEL = 1024
RET_HEADS = 4
RET_DK = 64
RET_DV = 128
RET_CHUNK = 128
LRU_WIDTH = 512
LRU_BLOCKS = 4
LRU_BLOCK = LRU_WIDTH // LRU_BLOCKS
CONV_WIDTH = 4
LRU_C = 8.0
REC_IN = 2 * RET_HEADS * RET_DK + 2 * RET_HEADS * RET_DV + 2 * LRU_WIDTH
ATT_HEADS = 16
ATT_DH = 64
PATTERNS = ((128, 1), (512, 4), (2048, 16))
ATT_BLOCK = 128
SUPER = 2048
MAX_DIL = 16
NEG = -1e30

LANES = 128
SUBLANES = 8
VMEM_LIMIT = 56 * 1024 * 1024

FF_CHUNK = 256
TOKEN_TILE = 512


def _cparams(sem):
    return pltpu.CompilerParams(dimension_semantics=sem, vmem_limit_bytes=VMEM_LIMIT)


def _const_spec(shape):
    nd = len(shape)
    return pl.BlockSpec(shape, lambda *_: (0,) * nd, pipeline_mode=pl.Buffered(1))


def _rmsnorm(x, g):
    return x * lax.rsqrt(jnp.mean(x * x, axis=-1, keepdims=True) + EPS) * g


def _load_rows(ref, nsl):
    if nsl == 1:
        return ref[...]
    return jnp.concatenate([ref[:, i * D_MODEL:(i + 1) * D_MODEL] for i in range(nsl)], axis=0)


def _store_rows(ref, val, nsl):
    if nsl == 1:
        ref[...] = val
        return
    rows = val.shape[0] // nsl
    for i in range(nsl):
        ref[:, i * D_MODEL:(i + 1) * D_MODEL] = val[i * rows:(i + 1) * rows]


def _norm_proj_kernel(x_ref, g_ref, w_ref, o_ref):
    h = _rmsnorm(x_ref[...], g_ref[...]).astype(BF16)
    o_ref[...] = jnp.dot(h, w_ref[...], preferred_element_type=F32)


def _norm_proj(x2d, g, w_bf):
    m, d = x2d.shape
    n = w_bf.shape[1]
    tm = min(TOKEN_TILE, m)
    return pl.pallas_call(
        _norm_proj_kernel,
        grid=(m // tm,),
        in_specs=[pl.BlockSpec((tm, d), lambda i: (i, 0)), _const_spec((1, d)), _const_spec((d, n))],
        out_specs=pl.BlockSpec((tm, n), lambda i: (i, 0)),
        out_shape=jax.ShapeDtypeStruct((m, n), F32),
        compiler_params=_cparams(("arbitrary",)),
        name="norm_proj",
    )(x2d, g.reshape(1, d), w_bf)


def _shift_rows(x, s, fill):
    rows = lax.broadcasted_iota(jnp.int32, x.shape, 0)
    return jnp.where(rows < s, fill, pltpu.roll(x, s, axis=0))


def _pad_rows(x, rows):
    if x.shape[0] == rows:
        return x
    return jnp.concatenate([x, jnp.zeros((rows - x.shape[0], x.shape[1]), x.dtype)], axis=0)


def _rec_mix_kernel(proj_ref, s0_ref, h0_ref, c0_ref, dec_ref, qdec_ref, kdec_ref, cdec_ref,
                    retg_ref, convw_ref, convb_ref, wrg_ref, brg_ref, wig_ref, big_ref, lam_ref,
                    mix_ref, sret_ref, hlast_ref, cnew_ref,
                    s_scr, h_scr, xe_scr, *, rows):
    c = pl.program_id(1)
    cp = RET_CHUNK
    hk = RET_HEADS * RET_DK
    v_off, g_off = 2 * hk, 2 * hk + RET_HEADS * RET_DV
    xl_off = g_off + RET_HEADS * RET_DV
    yl_off = xl_off + LRU_WIDTH

    @pl.when(c == 0)
    def _():
        s_scr[...] = s0_ref[...]
        h_scr[...] = h0_ref[...]
        xe_scr[0:SUBLANES, :] = jnp.zeros((SUBLANES, LRU_WIDTH), F32)
        xe_scr[SUBLANES - (CONV_WIDTH - 1):SUBLANES, :] = c0_ref[...]

    lane = lax.broadcasted_iota(jnp.int32, (cp, LANES), 1)
    first = lane < RET_DK
    srow_first = lax.broadcasted_iota(jnp.int32, (2 * RET_DK, RET_DV), 0) < RET_DK

    for p in range(RET_HEADS // 2):
        q2 = _pad_rows(proj_ref[:, p * LANES:(p + 1) * LANES], cp)
        k2 = _pad_rows(proj_ref[:, hk + p * LANES:hk + (p + 1) * LANES], cp) * (RET_DK ** -0.5)
        kb = k2.astype(BF16)
        kdt = jnp.transpose(k2 * kdec_ref[p]).astype(BF16)
        s2 = s_scr[p]
        s2b = s2.astype(BF16)
        new_s = []
        for hh in range(2):
            h = 2 * p + hh
            sel = first if hh == 0 else jnp.logical_not(first)
            qh = jnp.where(sel, q2, 0.0)
            v = _pad_rows(proj_ref[:, v_off + h * RET_DV:v_off + (h + 1) * RET_DV], cp)
            vb = v.astype(BF16)
            att = lax.dot_general(qh.astype(BF16), kb, (((1,), (1,)), ((), ())),
                                  preferred_element_type=F32) * dec_ref[h]
            o = (jnp.dot(att.astype(BF16), vb, preferred_element_type=F32)
                 + jnp.dot((qh * qdec_ref[p]).astype(BF16), s2b, preferred_element_type=F32))
            new_s.append(jnp.dot(kdt, vb, preferred_element_type=F32))
            o = o[:rows]
            o = o * lax.rsqrt(jnp.mean(o * o, axis=-1, keepdims=True) + EPS) * retg_ref[h:h + 1, :]
            gate = proj_ref[:, g_off + h * RET_DV:g_off + (h + 1) * RET_DV]
            mix_ref[:, h * RET_DV:(h + 1) * RET_DV] = (o * (gate * jax.nn.sigmoid(gate))).astype(mix_ref.dtype)
        s_scr[p] = s2 * cdec_ref[p] + jnp.where(srow_first, new_s[0], new_s[1])

    x = proj_ref[:, xl_off:xl_off + LRU_WIDTH]
    xe_scr[SUBLANES:SUBLANES + rows, :] = x
    xc = convb_ref[...] + xe_scr[pl.ds(SUBLANES - 3, rows), :] * convw_ref[0:1, :]
    for j in range(1, CONV_WIDTH):
        xc = xc + xe_scr[pl.ds(SUBLANES - 3 + j, rows), :] * convw_ref[j:j + 1, :]
    tail = xe_scr[rows:rows + SUBLANES, :]
    xe_scr[0:SUBLANES, :] = tail
    cnew_ref[...] = xe_scr[pl.ds(SUBLANES - (CONV_WIDTH - 1), CONV_WIDTH - 1), :]

    xcb = _pad_rows(xc, max(rows, 2 * SUBLANES)).astype(BF16)
    r_parts, i_parts = [], []
    for n in range(LRU_BLOCKS):
        xb = xcb[:, n * LRU_BLOCK:(n + 1) * LRU_BLOCK]
        r_parts.append(jnp.dot(xb, wrg_ref[n], preferred_element_type=F32)[:rows])
        i_parts.append(jnp.dot(xb, wig_ref[n], preferred_element_type=F32)[:rows])
    r = jax.nn.sigmoid(jnp.concatenate(r_parts, axis=1) + brg_ref[...])
    ig = jax.nn.sigmoid(jnp.concatenate(i_parts, axis=1) + big_ref[...])
    nl = -lam_ref[...]
    softplus = jnp.maximum(nl, 0.0) + jnp.log1p(jnp.exp(-jnp.abs(nl)))
    log_a = (-LRU_C) * r * softplus
    a = jnp.exp(log_a)
    b = jnp.sqrt(1.0 - jnp.exp(2.0 * log_a)) * (ig * xc)

    s = 1
    while s < rows:
        b = a * _shift_rows(b, s, 0.0) + b
        a = a * _shift_rows(a, s, 1.0)
        s *= 2
    hseq = b + a * h_scr[...]
    h_scr[...] = hseq[rows - 1:rows, :]
    hlast_ref[...] = hseq[rows - 1:rows, :]
    yl = proj_ref[:, yl_off:yl_off + LRU_WIDTH]
    gelu = 0.5 * yl * (1.0 + jnp.tanh(0.7978845608028654 * (yl + 0.044715 * (yl * yl * yl))))
    mix_ref[:, RET_HEADS * RET_DV:] = (hseq * gelu).astype(mix_ref.dtype)
    sret_ref[...] = s_scr[...]


def _ret_tables(t_eff):
    cp = RET_CHUNK
    log_g = jnp.log1p(-jnp.power(2.0, -5.0 - jnp.arange(RET_HEADS, dtype=F32)))
    idx = jnp.arange(cp, dtype=F32)
    live = idx < t_eff
    diff = idx[:, None] - idx[None, :]
    ok = (diff >= 0) & live[:, None] & live[None, :]
    decay = jnp.where(ok, jnp.exp(log_g[:, None, None] * jnp.maximum(diff, 0.0)), 0.0)
    q_dec = jnp.where(live, jnp.exp(log_g[:, None] * (idx + 1.0)), 0.0)
    k_dec = jnp.where(live, jnp.exp(log_g[:, None] * (t_eff - 1.0 - idx)), 0.0)
    c_dec = jnp.exp(log_g * t_eff)

    def pair_lanes(t):
        t = t.reshape(RET_HEADS // 2, 2, cp)
        return jnp.repeat(jnp.transpose(t, (0, 2, 1)), RET_DK, axis=2)

    c_rows = jnp.repeat(c_dec.reshape(RET_HEADS // 2, 2), RET_DK, axis=1)
    c_rows = jnp.broadcast_to(c_rows[:, :, None], (RET_HEADS // 2, 2 * RET_DK, RET_DV))
    return decay, pair_lanes(q_dec), pair_lanes(k_dec), c_rows


def _rec_mix(proj, s0, h0, c0, ret_g, conv_w, conv_b, w_rg, b_rg, w_ig, b_ig, lam):
    b, t, _ = proj.shape
    rows = RET_CHUNK if t % RET_CHUNK == 0 else t
    assert t % rows == 0 and rows % SUBLANES == 0 and (t == rows or rows == RET_CHUNK)
    n = t // rows
    dec, qdec, kdec, cdec = _ret_tables(rows)
    hp = RET_HEADS // 2
    w = LRU_WIDTH
    per_b3 = lambda i, c: (i, 0, 0)
    outs = pl.pallas_call(
        functools.partial(_rec_mix_kernel, rows=rows),
        grid=(b, n),
        in_specs=[
            pl.BlockSpec((None, rows, REC_IN), lambda i, c: (i, c, 0)),
            pl.BlockSpec((None, hp, 2 * RET_DK, RET_DV), lambda i, c: (i, 0, 0, 0)),
            pl.BlockSpec((None, 1, w), per_b3),
            pl.BlockSpec((None, CONV_WIDTH - 1, w), per_b3),
            _const_spec(dec.shape), _const_spec(qdec.shape), _const_spec(kdec.shape), _const_spec(cdec.shape),
            _const_spec((RET_HEADS, RET_DV)), _const_spec((CONV_WIDTH, w)), _const_spec((1, w)),
            _const_spec((LRU_BLOCKS, LRU_BLOCK, LRU_BLOCK)), _const_spec((1, w)),
            _const_spec((LRU_BLOCKS, LRU_BLOCK, LRU_BLOCK)), _const_spec((1, w)), _const_spec((1, w)),
        ],
        out_specs=[
            pl.BlockSpec((None, rows, D_MODEL), lambda i, c: (i, c, 0)),
            pl.BlockSpec((None, hp, 2 * RET_DK, RET_DV), lambda i, c: (i, 0, 0, 0)),
            pl.BlockSpec((None, 1, w), per_b3),
            pl.BlockSpec((None, CONV_WIDTH - 1, w), per_b3),
        ],
        out_shape=[
            jax.ShapeDtypeStruct((b, t, D_MODEL), BF16),
            jax.ShapeDtypeStruct((b, hp, 2 * RET_DK, RET_DV), F32),
            jax.ShapeDtypeStruct((b, 1, w), F32),
            jax.ShapeDtypeStruct((b, CONV_WIDTH - 1, w), F32),
        ],
        scratch_shapes=[
            pltpu.VMEM((hp, 2 * RET_DK, RET_DV), F32),
            pltpu.VMEM((1, w), F32),
            pltpu.VMEM((rows + SUBLANES, w), F32),
        ],
        compiler_params=_cparams(("arbitrary", "arbitrary")),
        name="rec_mix",
    )(proj, s0.reshape(b, hp, 2 * RET_DK, RET_DV), h0.reshape(b, 1, w), c0,
      dec, qdec, kdec, cdec, ret_g, conv_w, conv_b.reshape(1, w),
      w_rg.astype(BF16), b_rg.reshape(1, w), w_ig.astype(BF16), b_ig.reshape(1, w), lam.reshape(1, w))
    mix, s_ret, h_last, c_new = outs
    return mix, s_ret.reshape(b, RET_HEADS, RET_DK, RET_DV), h_last.reshape(b, w), c_new


def _out_ffn_kernel(mix_ref, x_ref, wo_ref, g_ref, wg_ref, wu_ref, wd_ref, o_ref, *, nsl):
    x = _load_rows(x_ref, nsl)
    y = x + jnp.dot(mix_ref[...], wo_ref[...], preferred_element_type=F32)
    h = _rmsnorm(y, g_ref[...]).astype(BF16)

    def body(c, acc):
        gate = jnp.dot(h, wg_ref[c], preferred_element_type=F32)
        up = jnp.dot(h, wu_ref[c], preferred_element_type=F32)
        act = (gate * jax.nn.sigmoid(gate) * up).astype(BF16)
        return acc + jnp.dot(act, wd_ref[c], preferred_element_type=F32)

    ff = lax.fori_loop(0, wg_ref.shape[0], body, jnp.zeros_like(y))
    _store_rows(o_ref, y + ff, nsl)


def _ffn_weights(w_out, g, w_gate, w_up, w_down):
    d, f = w_gate.shape
    nc = f // FF_CHUNK
    col_chunks = lambda w: jnp.transpose(w.astype(BF16).reshape(d, nc, FF_CHUNK), (1, 0, 2))
    return (w_out.astype(BF16), g.reshape(1, d), col_chunks(w_gate), col_chunks(w_up),
            w_down.astype(BF16).reshape(nc, FF_CHUNK, d))


def _ffn_specs(wts):
    return [_const_spec(w.shape) for w in wts]


def _out_ffn(mix2d, x2d, wts):
    m, d = x2d.shape
    tm = min(TOKEN_TILE, m)
    row = lambda i: (i, 0)
    return pl.pallas_call(
        functools.partial(_out_ffn_kernel, nsl=1),
        grid=(m // tm,),
        in_specs=[pl.BlockSpec((tm, d), row), pl.BlockSpec((tm, d), row)] + _ffn_specs(wts),
        out_specs=pl.BlockSpec((tm, d), row),
        out_shape=jax.ShapeDtypeStruct((m, d), F32),
        compiler_params=_cparams(("arbitrary",)),
        name="out_ffn",
    )(mix2d, x2d, *wts)


PERM_SL = 4


def _out_ffn_perm(mix, x, wts):
    b, t, d = x.shape
    ns = t // SUPER
    xv = x.reshape(b, t // MAX_DIL, MAX_DIL * d)
    tm = PERM_SL * ATT_BLOCK
    nq = MAX_DIL // PERM_SL
    view = lambda i, s, r: (i, s, r)
    out = pl.pallas_call(
        functools.partial(_out_ffn_kernel, nsl=PERM_SL),
        grid=(b, ns, nq),
        in_specs=[pl.BlockSpec((None, tm, d), lambda i, s, r: (i, s * nq + r, 0)),
                  pl.BlockSpec((None, ATT_BLOCK, PERM_SL * d), view)] + _ffn_specs(wts),
        out_specs=pl.BlockSpec((None, ATT_BLOCK, PERM_SL * d), view),
        out_shape=jax.ShapeDtypeStruct(xv.shape, F32),
        compiler_params=_cparams(("arbitrary",) * 3),
        name="out_ffn_perm",
    )(mix, xv, *wts)
    return out.reshape(b, t, d)


def _pair_rms(a, g2, first):
    sq = a * a
    s_a = jnp.sum(jnp.where(first, sq, 0.0), axis=-1, keepdims=True)
    s_b = jnp.sum(jnp.where(first, 0.0, sq), axis=-1, keepdims=True)
    ms = jnp.where(first, s_a, s_b) * (1.0 / ATT_DH)
    return a * lax.rsqrt(ms + EPS) * g2


def _qkv_kernel(x_ref, g_ref, w_ref, qg_ref, kg_ref, *out_refs, nsl, pair_major):
    x = _load_rows(x_ref, nsl)
    h = _rmsnorm(x, g_ref[...]).astype(BF16)
    qkv = jnp.dot(h, w_ref[...], preferred_element_type=F32)
    rows = x.shape[0]
    first = lax.broadcasted_iota(jnp.int32, (rows, LANES), 1) < ATT_DH
    npair = D_MODEL // LANES
    q = [_pair_rms(qkv[:, p * LANES:(p + 1) * LANES], qg_ref[...], first) for p in range(npair)]
    k = [_pair_rms(qkv[:, D_MODEL + p * LANES:D_MODEL + (p + 1) * LANES], kg_ref[...], first) for p in range(npair)]
    v = [qkv[:, 2 * D_MODEL + p * LANES:2 * D_MODEL + (p + 1) * LANES] for p in range(npair)]
    if pair_major:
        q_ref, k_ref, v_ref, kc_ref, vc_ref = out_refs
        for p in range(npair):
            q_ref[p] = q[p]
            k_ref[p] = k[p]
            v_ref[p] = v[p]
        _store_rows(kc_ref, jnp.concatenate(k, axis=1), nsl)
        _store_rows(vc_ref, jnp.concatenate(v, axis=1), nsl)
    else:
        q_ref, k_ref, v_ref = out_refs
        q_ref[...] = jnp.concatenate(q, axis=1)
        k_ref[...] = jnp.concatenate(k, axis=1)
        v_ref[...] = jnp.concatenate(v, axis=1)


def _head_gain(g):
    return jnp.tile(g.reshape(1, ATT_DH), (1, LANES // ATT_DH))


def _qkv_flat(x2d, g, w_bf, qg, kg):
    m, d = x2d.shape
    tm = min(TOKEN_TILE, m)
    row = lambda i: (i, 0)
    return pl.pallas_call(
        functools.partial(_qkv_kernel, nsl=1, pair_major=False),
        grid=(m // tm,),
        in_specs=[pl.BlockSpec((tm, d), row), _const_spec((1, d)), _const_spec(w_bf.shape),
                  _const_spec((1, LANES)), _const_spec((1, LANES))],
        out_specs=[pl.BlockSpec((tm, d), row)] * 3,
        out_shape=[jax.ShapeDtypeStruct((m, d), F32)] * 3,
        compiler_params=_cparams(("arbitrary",)),
        name="qkv_flat",
    )(x2d, g.reshape(1, d), w_bf, _head_gain(qg), _head_gain(kg))


def _qkv_perm(x, g, w_bf, qg, kg):
    b, t, d = x.shape
    ns = t // SUPER
    xv = x.reshape(b, t // MAX_DIL, MAX_DIL * d)
    tm = PERM_SL * ATT_BLOCK
    nq = MAX_DIL // PERM_SL
    npair = d // LANES
    pm_spec = pl.BlockSpec((None, npair, tm, LANES), lambda i, s, r: (i, 0, s * nq + r, 0))
    cache_spec = pl.BlockSpec((None, ATT_BLOCK, PERM_SL * d), lambda i, s, r: (i, s, r))
    pm_shape = jax.ShapeDtypeStruct((b, npair, t, LANES), F32)
    cache_shape = jax.ShapeDtypeStruct(xv.shape, F32)
    q, k, v, kc, vc = pl.pallas_call(
        functools.partial(_qkv_kernel, nsl=PERM_SL, pair_major=True),
        grid=(b, ns, nq),
        in_specs=[pl.BlockSpec((None, ATT_BLOCK, PERM_SL * d), lambda i, s, r: (i, s, r)),
                  _const_spec((1, d)), _const_spec(w_bf.shape), _const_spec((1, LANES)), _const_spec((1, LANES))],
        out_specs=[pm_spec, pm_spec, pm_spec, cache_spec, cache_spec],
        out_shape=[pm_shape, pm_shape, pm_shape, cache_shape, cache_shape],
        compiler_params=_cparams(("arbitrary",) * 3),
        name="qkv_perm",
    )(xv, g.reshape(1, d), w_bf, _head_gain(qg), _head_gain(kg))
    return q, k, v, kc.reshape(b, t, d)[:, t - SUPER:], vc.reshape(b, t, d)[:, t - SUPER:]


def _alibi_slopes():
    return jnp.power(2.0, -8.0 * jnp.arange(1, ATT_HEADS + 1, dtype=F32) / ATT_HEADS)


PREV_SLOTS = 32
ATT_UNROLL = 8
LOG2E = 1.4426950408889634


def _band_bias():
    i = jnp.arange(ATT_BLOCK)
    c = jnp.arange(2 * ATT_BLOCK)
    q_idx = {1: MAX_DIL * (i % 8) + i // 8,
             4: 4 * (i % 32) + i // 32,
             16: i}
    k_idx = {1: (MAX_DIL * (c % 8) + c // 16, (c // 8) % 2 == 0),
             4: (4 * (c % 32) + c // 64, (c // 32) % 2 == 0),
             16: (c % ATT_BLOCK, c < ATT_BLOCK)}
    slopes = _alibi_slopes().reshape(ATT_HEADS // 2, 2, 1, 1)
    out = []
    for (w, d) in PATTERNS:
        span = w // d
        km, is_prev = k_idx[d]
        diff = q_idx[d][:, None] - (km - jnp.where(is_prev, ATT_BLOCK, 0))[None, :]
        valid = (diff >= 0) & (diff <= span)
        bias = -slopes * (d * diff).astype(F32)[None, None]
        bias = bias * LOG2E
        variants = [jnp.where(valid, bias, NEG), jnp.where(valid & ~is_prev[None, :], bias, NEG)]
        out.append(jnp.stack([v.reshape(ATT_HEADS // 2, 2 * ATT_BLOCK, 2 * ATT_BLOCK) for v in variants]))
    return jnp.stack(out)


def _attend_pair(q, k, v, bias, first):
    qs = jnp.concatenate([jnp.where(first, q, 0.0), jnp.where(first, 0.0, q)], axis=0).astype(BF16)
    s = lax.dot_general(qs, k.astype(BF16), (((1,), (1,)), ((), ())), preferred_element_type=F32) + bias
    m = jnp.max(s, axis=-1, keepdims=True)
    e = jnp.exp2(s - m)
    den = jnp.sum(e, axis=-1, keepdims=True)
    o = jnp.dot(e.astype(BF16), v.astype(BF16), preferred_element_type=F32)
    halves = lambda a: jnp.where(first, a[:ATT_BLOCK], a[ATT_BLOCK:])
    return halves(o), halves(m), halves(den)


def _attn_prompt_kernel(q_ref, kc_ref, kp_ref, vc_ref, vp_ref, bias_ref, o_ref, kext, vext, oacc, macc, dacc):
    first_sb = (pl.program_id(1) == 0).astype(jnp.int32)
    first = lax.broadcasted_iota(jnp.int32, (ATT_BLOCK, LANES), 1) < ATT_DH
    scale = (ATT_DH ** -0.5) * LOG2E
    ps = PREV_SLOTS

    for src_p, src_c, ext in ((kp_ref, kc_ref, kext), (vp_ref, vc_ref, vext)):
        for r in range(MAX_DIL):
            ext[r, 0:ps, :] = src_p[r * ATT_BLOCK + ATT_BLOCK - ps:(r + 1) * ATT_BLOCK, :]
            ext[r, ps:ps + ATT_BLOCK, :] = src_c[r * ATT_BLOCK:(r + 1) * ATT_BLOCK, :]

    def attend_store(p, variant, q_chunks, k, v):
        n = ATT_BLOCK // len(q_chunks)
        q = jnp.concatenate([q_ref[sl, :] for sl in q_chunks], axis=0) * scale
        o, m, den = _attend_pair(q, k, v, bias_ref[p, variant], first)
        for ci, sl in enumerate(q_chunks):
            oacc[p, sl, :] = o[ci * n:(ci + 1) * n]
            macc[p, sl, :] = m[ci * n:(ci + 1) * n]
            dacc[p, sl, :] = den[ci * n:(ci + 1) * n]

    n0 = ATT_BLOCK // MAX_DIL

    def block0(bi):
        base = pl.multiple_of(bi * n0, n0)
        ksl = pl.ds(base + ps - n0, 2 * n0)
        k = jnp.concatenate([kext[r, ksl, :] for r in range(MAX_DIL)], axis=0)
        v = jnp.concatenate([vext[r, ksl, :] for r in range(MAX_DIL)], axis=0)
        attend_store(0, first_sb * (bi == 0).astype(jnp.int32),
                     [pl.ds(r * ATT_BLOCK + base, n0) for r in range(MAX_DIL)], k, v)

    d1 = PATTERNS[1][1]
    nc1 = MAX_DIL // d1
    n1 = ATT_BLOCK // nc1

    def block1(idx):
        r4, mb = idx // (ATT_BLOCK // n1), idx % (ATT_BLOCK // n1)
        base = pl.multiple_of(mb * n1, n1)
        ksl = pl.ds(base + ps - n1, 2 * n1)
        k = jnp.concatenate([kext[r4 + d1 * c, ksl, :] for c in range(nc1)], axis=0)
        v = jnp.concatenate([vext[r4 + d1 * c, ksl, :] for c in range(nc1)], axis=0)
        attend_store(1, first_sb * (mb == 0).astype(jnp.int32),
                     [pl.ds(pl.multiple_of((r4 + d1 * c) * ATT_BLOCK + base, n1), n1) for c in range(nc1)], k, v)

    def block2(r):
        sl = pl.ds(pl.multiple_of(r * ATT_BLOCK, ATT_BLOCK), ATT_BLOCK)
        k = jnp.concatenate([kp_ref[sl, :], kc_ref[sl, :]], axis=0)
        v = jnp.concatenate([vp_ref[sl, :], vc_ref[sl, :]], axis=0)
        attend_store(2, first_sb, [sl], k, v)

    for block in (block0, block1, block2):
        def body(it, carry, block=block):
            for u in range(ATT_UNROLL):
                block(it * ATT_UNROLL + u)
            return carry

        lax.fori_loop(0, MAX_DIL // ATT_UNROLL, body, 0)

    def comb(r, carry):
        sl = pl.ds(pl.multiple_of(r * ATT_BLOCK, ATT_BLOCK), ATT_BLOCK)
        m0, m1, m2 = macc[0, sl, :], macc[1, sl, :], macc[2, sl, :]
        mx = jnp.maximum(jnp.maximum(m0, m1), m2)
        w0, w1, w2 = jnp.exp2(m0 - mx), jnp.exp2(m1 - mx), jnp.exp2(m2 - mx)
        num = w0 * oacc[0, sl, :] + w1 * oacc[1, sl, :] + w2 * oacc[2, sl, :]
        den = w0 * dacc[0, sl, :] + w1 * dacc[1, sl, :] + w2 * dacc[2, sl, :]
        o_ref[sl, :] = (num / den).astype(o_ref.dtype)
        return carry

    lax.fori_loop(0, SUPER // ATT_BLOCK, comb, 0)


def _attn_prompt(q, k, v):
    b, npair, t, _ = q.shape
    ns = t // SUPER
    bias = _band_bias()
    cur = pl.BlockSpec((None, None, SUPER, LANES), lambda i, s, p: (i, p, s, 0))
    prev = pl.BlockSpec((None, None, SUPER, LANES), lambda i, s, p: (i, p, jnp.maximum(s - 1, 0), 0))
    return pl.pallas_call(
        _attn_prompt_kernel,
        grid=(b, ns, npair),
        in_specs=[cur, cur, prev, cur, prev,
                  pl.BlockSpec((len(PATTERNS), 2, None, 2 * ATT_BLOCK, 2 * ATT_BLOCK),
                               lambda i, s, p: (0, 0, p, 0, 0))],
        out_specs=pl.BlockSpec((None, SUPER, LANES), lambda i, s, p: (i, s, p)),
        out_shape=jax.ShapeDtypeStruct((b, t, npair * LANES), BF16),
        scratch_shapes=[pltpu.VMEM((MAX_DIL, PREV_SLOTS + ATT_BLOCK, LANES), F32),
                        pltpu.VMEM((MAX_DIL, PREV_SLOTS + ATT_BLOCK, LANES), F32),
                        pltpu.VMEM((len(PATTERNS), SUPER, LANES), F32),
                        pltpu.VMEM((len(PATTERNS), SUPER, LANES), F32),
                        pltpu.VMEM((len(PATTERNS), SUPER, LANES), F32)],
        compiler_params=_cparams(("arbitrary",) * 3),
        name="attn_prompt",
    )(q, k, k, v, v, bias)


QUAD = 4
NEW_PAD = 128


def _sample_bias(s_len, n_past):
    s = jnp.arange(s_len)
    col = jnp.arange(n_past + NEW_PAD)
    pos = jnp.where(col < n_past, col, n_past + (col - n_past))
    dist = (n_past + s)[:, None] - pos[None, :]
    real = (col < n_past + s_len)[None, :]
    slopes = _alibi_slopes().reshape(ATT_HEADS // QUAD, QUAD, 1, 1)
    out = []
    for (w, d) in PATTERNS:
        valid = (dist >= 0) & (dist % d == 0) & (dist <= w) & real
        bias = jnp.where(valid[None, None], -slopes * dist.astype(F32)[None, None], NEG)
        out.append(bias.reshape(ATT_HEADS // QUAD, QUAD * s_len, n_past + NEW_PAD))
    return jnp.stack(out)


def _attn_sample_kernel(q_ref, kn_ref, vn_ref, ck_ref, cv_ref, bias_ref, o_ref, *, s_len):
    n_past = ck_ref.shape[1]
    width = QUAD * ATT_DH
    rows = QUAD * s_len
    q = q_ref[...] * (ATT_DH ** -0.5)
    q4 = jnp.concatenate([q] * QUAD, axis=0)
    own = (lax.broadcasted_iota(jnp.int32, (rows, width), 0) // s_len
           == lax.broadcasted_iota(jnp.int32, (rows, width), 1) // ATT_DH)
    qm = jnp.where(own, q4, 0.0).astype(BF16)
    nt = (((1,), (1,)), ((), ()))
    kn = _pad_rows(kn_ref[...], NEW_PAD).astype(BF16)
    vn = _pad_rows(vn_ref[...], NEW_PAD).astype(BF16)
    s_c = jnp.dot(qm, ck_ref[...].astype(BF16), preferred_element_type=F32)
    s_n = lax.dot_general(qm, kn, nt, preferred_element_type=F32)
    es_c, es_n, dens, lses = [], [], [], []
    for p in range(len(PATTERNS)):
        b_c = bias_ref[p, :, 0:n_past]
        b_n = bias_ref[p, :, n_past:n_past + NEW_PAD]
        sc, sn = s_c + b_c, s_n + b_n
        m = jnp.maximum(jnp.max(sc, axis=-1, keepdims=True), jnp.max(sn, axis=-1, keepdims=True))
        ec, en = jnp.exp(sc - m), jnp.exp(sn - m)
        den = jnp.sum(ec, axis=-1, keepdims=True) + jnp.sum(en, axis=-1, keepdims=True)
        es_c.append(ec)
        es_n.append(en)
        dens.append(den)
        lses.append(m + jnp.log(den))
    mx = jnp.maximum(jnp.maximum(lses[0], lses[1]), lses[2])
    ws = [jnp.exp(l - mx) for l in lses]
    wsum = ws[0] + ws[1] + ws[2]
    coef = [w / (wsum * d) for w, d in zip(ws, dens)]
    c_c = coef[0] * es_c[0] + coef[1] * es_c[1] + coef[2] * es_c[2]
    c_n = coef[0] * es_n[0] + coef[1] * es_n[1] + coef[2] * es_n[2]
    o = (lax.dot_general(c_c.astype(BF16), cv_ref[...].astype(BF16), nt, preferred_element_type=F32)
         + jnp.dot(c_n.astype(BF16), vn, preferred_element_type=F32))
    o = jnp.where(own, o, 0.0)
    acc = o[0:s_len]
    for hh in range(1, QUAD):
        acc = acc + o[hh * s_len:(hh + 1) * s_len]
    o_ref[...] = acc.astype(o_ref.dtype)


def _attn_sample(q, kn, vn, ck, cv):
    b, s_len, d = q.shape
    n_past = ck.shape[2]
    width = QUAD * ATT_DH
    nquad = d // width
    bias = _sample_bias(s_len, n_past)
    new = pl.BlockSpec((None, s_len, width), lambda i, h: (i, 0, h))
    old = pl.BlockSpec((None, width, n_past), lambda i, h: (i, h, 0))
    return pl.pallas_call(
        functools.partial(_attn_sample_kernel, s_len=s_len),
        grid=(b, nquad),
        in_specs=[new, new, new, old, old,
                  pl.BlockSpec((len(PATTERNS), None, QUAD * s_len, n_past + NEW_PAD), lambda i, h: (0, h, 0, 0))],
        out_specs=new,
        out_shape=jax.ShapeDtypeStruct((b, s_len, d), BF16),
        compiler_params=_cparams(("arbitrary",) * 2),
        name="attn_sample",
    )(q, kn, vn, ck, cv, bias)


def kernel(x_prompt, x_sample, state_ret, state_lru, state_conv, cache_k, cache_v, norm_mix, norm_ffn, w_in_rec, w_out_rec, ret_norm_g, conv_w, conv_b, w_rgate, b_rgate, w_igate, b_igate, lru_lambda, w_in_att, w_out_att, q_norm_g, k_norm_g, w_ffn_gate, w_ffn_up, w_ffn_down):
    bp, tp, d = x_prompt.shape
    bs, ts, _ = x_sample.shape
    depth = norm_mix.shape[0]
    assert tp % SUPER == 0 and d == D_MODEL
    yp, ys = x_prompt, x_sample
    ret_p, lru_p, conv_p, kp, vp = [], [], [], [], []
    ret_s, lru_s, conv_s, ksl, vsl = [], [], [], [], []
    for l in range(depth):
        i = l // 2
        if l % 2 == 0:
            ffn = _ffn_weights(w_out_rec[i], norm_ffn[l], w_ffn_gate[l], w_ffn_up[l], w_ffn_down[l])
            w_in = w_in_rec[i].astype(BF16)
            rec_prm = (ret_norm_g[i], conv_w[i], conv_b[i], w_rgate[i], b_rgate[i], w_igate[i], b_igate[i],
                       lru_lambda[i])
            proj = _norm_proj(yp.reshape(bp * tp, d), norm_mix[l], w_in).reshape(bp, tp, REC_IN)
            mix, sr, sl, sc = _rec_mix(proj, jnp.zeros((bp, RET_HEADS, RET_DK, RET_DV), F32),
                                       jnp.zeros((bp, LRU_WIDTH), F32),
                                       jnp.zeros((bp, CONV_WIDTH - 1, LRU_WIDTH), F32), *rec_prm)
            yp = _out_ffn(mix.reshape(bp * tp, d), yp.reshape(bp * tp, d), ffn).reshape(bp, tp, d)
            ret_p.append(sr); lru_p.append(sl); conv_p.append(sc)
            proj = _norm_proj(ys.reshape(bs * ts, d), norm_mix[l], w_in).reshape(bs, ts, REC_IN)
            mix, sr, sl, sc = _rec_mix(proj, state_ret[i], state_lru[i], state_conv[i], *rec_prm)
            ys = _out_ffn(mix.reshape(bs * ts, d), ys.reshape(bs * ts, d), ffn).reshape(bs, ts, d)
            ret_s.append(sr); lru_s.append(sl); conv_s.append(sc)
        else:
            ffn = _ffn_weights(w_out_att[i], norm_ffn[l], w_ffn_gate[l], w_ffn_up[l], w_ffn_down[l])
            w_in = w_in_att[i].astype(BF16)
            q, k, v, kc, vc = _qkv_perm(yp, norm_mix[l], w_in, q_norm_g[i], k_norm_g[i])
            o = _attn_prompt(q, k, v)
            yp = _out_ffn_perm(o, yp, ffn)
            kp.append(kc.reshape(bp, SUPER, ATT_HEADS, ATT_DH)); vp.append(vc.reshape(bp, SUPER, ATT_HEADS, ATT_DH))
            qs, kn, vn = _qkv_flat(ys.reshape(bs * ts, d), norm_mix[l], w_in, q_norm_g[i], k_norm_g[i])
            n_past = cache_k.shape[2]
            to_slabs = lambda c: jnp.transpose(c, (0, 2, 3, 1)).reshape(bs, d, n_past)
            o = _attn_sample(qs.reshape(bs, ts, d), kn.reshape(bs, ts, d), vn.reshape(bs, ts, d),
                             to_slabs(cache_k[i]), to_slabs(cache_v[i]))
            ys = _out_ffn(o.reshape(bs * ts, d), ys.reshape(bs * ts, d), ffn).reshape(bs, ts, d)
            ksl.append(kn.reshape(bs, ts, ATT_HEADS, ATT_DH)); vsl.append(vn.reshape(bs, ts, ATT_HEADS, ATT_DH))
    return (yp, ys,
            jnp.stack(ret_p), jnp.stack(lru_p), jnp.stack(conv_p), jnp.stack(kp), jnp.stack(vp),
            jnp.stack(ret_s), jnp.stack(lru_s), jnp.stack(conv_s), jnp.stack(ksl), jnp.stack(vsl))
```

```python
import functools

import jax
import jax.numpy as jnp
from jax import lax
from jax.experimental import pallas as pl
from jax.experimental.pallas import tpu as pltpu

F32 = jnp.float32
BF16 = jnp.bfloat16

EPS = 1e-6
D_MODEL = 1024
RET_HEADS = 4
RET_DK = 64
RET_DV = 128
RET_CHUNK = 128
LRU_WIDTH = 512
LRU_BLOCKS = 4
LRU_BLOCK = LRU_WIDTH // LRU_BLOCKS
CONV_WIDTH = 4
LRU_C = 8.0
REC_IN = 2 * RET_HEADS * RET_DK + 2 * RET_HEADS * RET_DV + 2 * LRU_WIDTH
ATT_HEADS = 16
ATT_DH = 64
PATTERNS = ((128, 1), (512, 4), (2048, 16))
ATT_BLOCK = 128
SUPER = 2048
MAX_DIL = 16
NEG = -1e30

LANES = 128
SUBLANES = 8
VMEM_LIMIT = 56 * 1024 * 1024

FF_CHUNK = 256
TOKEN_TILE = 512


def _cparams(sem):
    return pltpu.CompilerParams(dimension_semantics=sem, vmem_limit_bytes=VMEM_LIMIT)


def _const_spec(shape):
    nd = len(shape)
    return pl.BlockSpec(shape, lambda *_: (0,) * nd, pipeline_mode=pl.Buffered(1))


def _rmsnorm(x, g):
    return x * lax.rsqrt(jnp.mean(x * x, axis=-1, keepdims=True) + EPS) * g


def _load_rows(ref, nsl):
    if nsl == 1:
        return ref[...]
    return jnp.concatenate([ref[:, i * D_MODEL:(i + 1) * D_MODEL] for i in range(nsl)], axis=0)


def _store_rows(ref, val, nsl):
    if nsl == 1:
        ref[...] = val
        return
    rows = val.shape[0] // nsl
    for i in range(nsl):
        ref[:, i * D_MODEL:(i + 1) * D_MODEL] = val[i * rows:(i + 1) * rows]


def _norm_proj_kernel(x_ref, g_ref, w_ref, o_ref):
    h = _rmsnorm(x_ref[...], g_ref[...]).astype(BF16)
    o_ref[...] = jnp.dot(h, w_ref[...], preferred_element_type=F32)


def _norm_proj(x2d, g, w_bf):
    m, d = x2d.shape
    n = w_bf.shape[1]
    tm = min(TOKEN_TILE, m)
    return pl.pallas_call(
        _norm_proj_kernel,
        grid=(m // tm,),
        in_specs=[pl.BlockSpec((tm, d), lambda i: (i, 0)), _const_spec((1, d)), _const_spec((d, n))],
        out_specs=pl.BlockSpec((tm, n), lambda i: (i, 0)),
        out_shape=jax.ShapeDtypeStruct((m, n), F32),
        compiler_params=_cparams(("arbitrary",)),
        name="norm_proj",
    )(x2d, g.reshape(1, d), w_bf)


def _shift_rows(x, s, fill):
    rows = lax.broadcasted_iota(jnp.int32, x.shape, 0)
    return jnp.where(rows < s, fill, pltpu.roll(x, s, axis=0))


def _pad_rows(x, rows):
    if x.shape[0] == rows:
        return x
    return jnp.concatenate([x, jnp.zeros((rows - x.shape[0], x.shape[1]), x.dtype)], axis=0)


def _rec_mix_kernel(proj_ref, s0_ref, h0_ref, c0_ref, dec_ref, qdec_ref, kdec_ref, cdec_ref,
                    retg_ref, convw_ref, convb_ref, wrg_ref, brg_ref, wig_ref, big_ref, lam_ref,
                    mix_ref, sret_ref, hlast_ref, cnew_ref,
                    s_scr, h_scr, xe_scr, *, rows):
    c = pl.program_id(1)
    cp = RET_CHUNK
    hk = RET_HEADS * RET_DK
    v_off, g_off = 2 * hk, 2 * hk + RET_HEADS * RET_DV
    xl_off = g_off + RET_HEADS * RET_DV
    yl_off = xl_off + LRU_WIDTH

    @pl.when(c == 0)
    def _():
        s_scr[...] = s0_ref[...]
        h_scr[...] = h0_ref[...]
        xe_scr[0:SUBLANES, :] = jnp.zeros((SUBLANES, LRU_WIDTH), F32)
        xe_scr[SUBLANES - (CONV_WIDTH - 1):SUBLANES, :] = c0_ref[...]

    lane = lax.broadcasted_iota(jnp.int32, (cp, LANES), 1)
    first = lane < RET_DK
    srow_first = lax.broadcasted_iota(jnp.int32, (2 * RET_DK, RET_DV), 0) < RET_DK

    for p in range(RET_HEADS // 2):
        q2 = _pad_rows(proj_ref[:, p * LANES:(p + 1) * LANES], cp)
        k2 = _pad_rows(proj_ref[:, hk + p * LANES:hk + (p + 1) * LANES], cp) * (RET_DK ** -0.5)
        kb = k2.astype(BF16)
        kdt = jnp.transpose(k2 * kdec_ref[p]).astype(BF16)
        s2 = s_scr[p]
        s2b = s2.astype(BF16)
        new_s = []
        for hh in range(2):
            h = 2 * p + hh
            sel = first if hh == 0 else jnp.logical_not(first)
            qh = jnp.where(sel, q2, 0.0)
            v = _pad_rows(proj_ref[:, v_off + h * RET_DV:v_off + (h + 1) * RET_DV], cp)
            vb = v.astype(BF16)
            att = lax.dot_general(qh.astype(BF16), kb, (((1,), (1,)), ((), ())),
                                  preferred_element_type=F32) * dec_ref[h]
            o = (jnp.dot(att.astype(BF16), vb, preferred_element_type=F32)
                 + jnp.dot((qh * qdec_ref[p]).astype(BF16), s2b, preferred_element_type=F32))
            new_s.append(jnp.dot(kdt, vb, preferred_element_type=F32))
            o = o[:rows]
            o = o * lax.rsqrt(jnp.mean(o * o, axis=-1, keepdims=True) + EPS) * retg_ref[h:h + 1, :]
            gate = proj_ref[:, g_off + h * RET_DV:g_off + (h + 1) * RET_DV]
            mix_ref[:, h * RET_DV:(h + 1) * RET_DV] = (o * (gate * jax.nn.sigmoid(gate))).astype(mix_ref.dtype)
        s_scr[p] = s2 * cdec_ref[p] + jnp.where(srow_first, new_s[0], new_s[1])

    x = proj_ref[:, xl_off:xl_off + LRU_WIDTH]
    xe_scr[SUBLANES:SUBLANES + rows, :] = x
    xc = convb_ref[...] + xe_scr[pl.ds(SUBLANES - 3, rows), :] * convw_ref[0:1, :]
    for j in range(1, CONV_WIDTH):
        xc = xc + xe_scr[pl.ds(SUBLANES - 3 + j, rows), :] * convw_ref[j:j + 1, :]
    tail = xe_scr[rows:rows + SUBLANES, :]
    xe_scr[0:SUBLANES, :] = tail
    cnew_ref[...] = xe_scr[pl.ds(SUBLANES - (CONV_WIDTH - 1), CONV_WIDTH - 1), :]

    xcb = _pad_rows(xc, max(rows, 2 * SUBLANES)).astype(BF16)
    r_parts, i_parts = [], []
    for n in range(LRU_BLOCKS):
        xb = xcb[:, n * LRU_BLOCK:(n + 1) * LRU_BLOCK]
        r_parts.append(jnp.dot(xb, wrg_ref[n], preferred_element_type=F32)[:rows])
        i_parts.append(jnp.dot(xb, wig_ref[n], preferred_element_type=F32)[:rows])
    r = jax.nn.sigmoid(jnp.concatenate(r_parts, axis=1) + brg_ref[...])
    ig = jax.nn.sigmoid(jnp.concatenate(i_parts, axis=1) + big_ref[...])
    nl = -lam_ref[...]
    softplus = jnp.maximum(nl, 0.0) + jnp.log1p(jnp.exp(-jnp.abs(nl)))
    log_a = (-LRU_C) * r * softplus
    a = jnp.exp(log_a)
    b = jnp.sqrt(1.0 - jnp.exp(2.0 * log_a)) * (ig * xc)

    s = 1
    while s < rows:
        b = a * _shift_rows(b, s, 0.0) + b
        a = a * _shift_rows(a, s, 1.0)
        s *= 2
    hseq = b + a * h_scr[...]
    h_scr[...] = hseq[rows - 1:rows, :]
    hlast_ref[...] = hseq[rows - 1:rows, :]
    yl = proj_ref[:, yl_off:yl_off + LRU_WIDTH]
    gelu = 0.5 * yl * (1.0 + jnp.tanh(0.7978845608028654 * (yl + 0.044715 * (yl * yl * yl))))
    mix_ref[:, RET_HEADS * RET_DV:] = (hseq * gelu).astype(mix_ref.dtype)
    sret_ref[...] = s_scr[...]


def _ret_tables(t_eff):
    cp = RET_CHUNK
    log_g = jnp.log1p(-jnp.power(2.0, -5.0 - jnp.arange(RET_HEADS, dtype=F32)))
    idx = jnp.arange(cp, dtype=F32)
    live = idx < t_eff
    diff = idx[:, None] - idx[None, :]
    ok = (diff >= 0) & live[:, None] & live[None, :]
    decay = jnp.where(ok, jnp.exp(log_g[:, None, None] * jnp.maximum(diff, 0.0)), 0.0)
    q_dec = jnp.where(live, jnp.exp(log_g[:, None] * (idx + 1.0)), 0.0)
    k_dec = jnp.where(live, jnp.exp(log_g[:, None] * (t_eff - 1.0 - idx)), 0.0)
    c_dec = jnp.exp(log_g * t_eff)

    def pair_lanes(t):
        t = t.reshape(RET_HEADS // 2, 2, cp)
        return jnp.repeat(jnp.transpose(t, (0, 2, 1)), RET_DK, axis=2)

    c_rows = jnp.repeat(c_dec.reshape(RET_HEADS // 2, 2), RET_DK, axis=1)
    c_rows = jnp.broadcast_to(c_rows[:, :, None], (RET_HEADS // 2, 2 * RET_DK, RET_DV))
    return decay, pair_lanes(q_dec), pair_lanes(k_dec), c_rows


def _rec_mix(proj, s0, h0, c0, ret_g, conv_w, conv_b, w_rg, b_rg, w_ig, b_ig, lam):
    b, t, _ = proj.shape
    rows = RET_CHUNK if t % RET_CHUNK == 0 else t
    assert t % rows == 0 and rows % SUBLANES == 0 and (t == rows or rows == RET_CHUNK)
    n = t // rows
    dec, qdec, kdec, cdec = _ret_tables(rows)
    hp = RET_HEADS // 2
    w = LRU_WIDTH
    per_b3 = lambda i, c: (i, 0, 0)
    outs = pl.pallas_call(
        functools.partial(_rec_mix_kernel, rows=rows),
        grid=(b, n),
        in_specs=[
            pl.BlockSpec((None, rows, REC_IN), lambda i, c: (i, c, 0)),
            pl.BlockSpec((None, hp, 2 * RET_DK, RET_DV), lambda i, c: (i, 0, 0, 0)),
            pl.BlockSpec((None, 1, w), per_b3),
            pl.BlockSpec((None, CONV_WIDTH - 1, w), per_b3),
            _const_spec(dec.shape), _const_spec(qdec.shape), _const_spec(kdec.shape), _const_spec(cdec.shape),
            _const_spec((RET_HEADS, RET_DV)), _const_spec((CONV_WIDTH, w)), _const_spec((1, w)),
            _const_spec((LRU_BLOCKS, LRU_BLOCK, LRU_BLOCK)), _const_spec((1, w)),
            _const_spec((LRU_BLOCKS, LRU_BLOCK, LRU_BLOCK)), _const_spec((1, w)), _const_spec((1, w)),
        ],
        out_specs=[
            pl.BlockSpec((None, rows, D_MODEL), lambda i, c: (i, c, 0)),
            pl.BlockSpec((None, hp, 2 * RET_DK, RET_DV), lambda i, c: (i, 0, 0, 0)),
            pl.BlockSpec((None, 1, w), per_b3),
            pl.BlockSpec((None, CONV_WIDTH - 1, w), per_b3),
        ],
        out_shape=[
            jax.ShapeDtypeStruct((b, t, D_MODEL), BF16),
            jax.ShapeDtypeStruct((b, hp, 2 * RET_DK, RET_DV), F32),
            jax.ShapeDtypeStruct((b, 1, w), F32),
            jax.ShapeDtypeStruct((b, CONV_WIDTH - 1, w), F32),
        ],
        scratch_shapes=[
            pltpu.VMEM((hp, 2 * RET_DK, RET_DV), F32),
            pltpu.VMEM((1, w), F32),
            pltpu.VMEM((rows + SUBLANES, w), F32),
        ],
        compiler_params=_cparams(("arbitrary", "arbitrary")),
        name="rec_mix",
    )(proj, s0.reshape(b, hp, 2 * RET_DK, RET_DV), h0.reshape(b, 1, w), c0,
      dec, qdec, kdec, cdec, ret_g, conv_w, conv_b.reshape(1, w),
      w_rg.astype(BF16), b_rg.reshape(1, w), w_ig.astype(BF16), b_ig.reshape(1, w), lam.reshape(1, w))
    mix, s_ret, h_last, c_new = outs
    return mix, s_ret.reshape(b, RET_HEADS, RET_DK, RET_DV), h_last.reshape(b, w), c_new


def _out_ffn_kernel(mix_ref, x_ref, wo_ref, g_ref, wg_ref, wu_ref, wd_ref, o_ref, act_ref, *, nsl):
    x = _load_rows(x_ref, nsl)
    y = x + jnp.dot(mix_ref[...], wo_ref[...], preferred_element_type=F32)
    h = _rmsnorm(y, g_ref[...]).astype(BF16)
    for c in range(wg_ref.shape[1] // FF_CHUNK):
        cols = slice(c * FF_CHUNK, (c + 1) * FF_CHUNK)
        gate = jnp.dot(h, wg_ref[:, cols], preferred_element_type=F32)
        up = jnp.dot(h, wu_ref[:, cols], preferred_element_type=F32)
        act_ref[:, cols] = (gate * jax.nn.sigmoid(gate) * up).astype(BF16)
    _store_rows(o_ref, y + jnp.dot(act_ref[...], wd_ref[...], preferred_element_type=F32), nsl)


def _ffn_weights(w_out, g, w_gate, w_up, w_down):
    return (w_out.astype(BF16), g.reshape(1, g.shape[0]), w_gate.astype(BF16), w_up.astype(BF16),
            w_down.astype(BF16))


def _ffn_scratch(tm, wts):
    return [pltpu.VMEM((tm, wts[2].shape[1]), BF16)]


def _ffn_specs(wts):
    return [_const_spec(w.shape) for w in wts]


def _out_ffn(mix2d, x2d, wts):
    m, d = x2d.shape
    tm = min(TOKEN_TILE, m)
    row = lambda i: (i, 0)
    return pl.pallas_call(
        functools.partial(_out_ffn_kernel, nsl=1),
        grid=(m // tm,),
        in_specs=[pl.BlockSpec((tm, d), row), pl.BlockSpec((tm, d), row)] + _ffn_specs(wts),
        out_specs=pl.BlockSpec((tm, d), row),
        out_shape=jax.ShapeDtypeStruct((m, d), F32),
        scratch_shapes=_ffn_scratch(tm, wts),
        compiler_params=_cparams(("arbitrary",)),
        name="out_ffn",
    )(mix2d, x2d, *wts)


PERM_SL = 4


def _out_ffn_perm(mix, x, wts):
    b, t, d = x.shape
    ns = t // SUPER
    xv = x.reshape(b, t // MAX_DIL, MAX_DIL * d)
    tm = PERM_SL * ATT_BLOCK
    nq = MAX_DIL // PERM_SL
    view = lambda i, s, r: (i, s, r)
    out = pl.pallas_call(
        functools.partial(_out_ffn_kernel, nsl=PERM_SL),
        grid=(b, ns, nq),
        in_specs=[pl.BlockSpec((None, tm, d), lambda i, s, r: (i, s * nq + r, 0)),
                  pl.BlockSpec((None, ATT_BLOCK, PERM_SL * d), view)] + _ffn_specs(wts),
        out_specs=pl.BlockSpec((None, ATT_BLOCK, PERM_SL * d), view),
        out_shape=jax.ShapeDtypeStruct(xv.shape, F32),
        scratch_shapes=_ffn_scratch(tm, wts),
        compiler_params=_cparams(("arbitrary",) * 3),
        name="out_ffn_perm",
    )(mix, xv, *wts)
    return out.reshape(b, t, d)


def _pair_rms(a, g2, first):
    sq = a * a
    s_a = jnp.sum(jnp.where(first, sq, 0.0), axis=-1, keepdims=True)
    s_b = jnp.sum(jnp.where(first, 0.0, sq), axis=-1, keepdims=True)
    ms = jnp.where(first, s_a, s_b) * (1.0 / ATT_DH)
    return a * lax.rsqrt(ms + EPS) * g2


def _qkv_kernel(x_ref, g_ref, w_ref, qg_ref, kg_ref, *out_refs, nsl, pair_major):
    x = _load_rows(x_ref, nsl)
    h = _rmsnorm(x, g_ref[...]).astype(BF16)
    qkv = jnp.dot(h, w_ref[...], preferred_element_type=F32)
    rows = x.shape[0]
    first = lax.broadcasted_iota(jnp.int32, (rows, LANES), 1) < ATT_DH
    npair = D_MODEL // LANES
    q = [_pair_rms(qkv[:, p * LANES:(p + 1) * LANES], qg_ref[...], first) for p in range(npair)]
    k = [_pair_rms(qkv[:, D_MODEL + p * LANES:D_MODEL + (p + 1) * LANES], kg_ref[...], first) for p in range(npair)]
    v = [qkv[:, 2 * D_MODEL + p * LANES:2 * D_MODEL + (p + 1) * LANES] for p in range(npair)]
    if pair_major:
        q_ref, k_ref, v_ref, kc_ref, vc_ref = out_refs
        for p in range(npair):
            q_ref[p] = q[p]
            k_ref[p] = k[p]
            v_ref[p] = v[p]
        _store_rows(kc_ref, jnp.concatenate(k, axis=1), nsl)
        _store_rows(vc_ref, jnp.concatenate(v, axis=1), nsl)
    else:
        q_ref, k_ref, v_ref = out_refs
        q_ref[...] = jnp.concatenate(q, axis=1)
        k_ref[...] = jnp.concatenate(k, axis=1)
        v_ref[...] = jnp.concatenate(v, axis=1)


def _head_gain(g):
    return jnp.tile(g.reshape(1, ATT_DH), (1, LANES // ATT_DH))


def _qkv_flat(x2d, g, w_bf, qg, kg):
    m, d = x2d.shape
    tm = min(TOKEN_TILE, m)
    row = lambda i: (i, 0)
    return pl.pallas_call(
        functools.partial(_qkv_kernel, nsl=1, pair_major=False),
        grid=(m // tm,),
        in_specs=[pl.BlockSpec((tm, d), row), _const_spec((1, d)), _const_spec(w_bf.shape),
                  _const_spec((1, LANES)), _const_spec((1, LANES))],
        out_specs=[pl.BlockSpec((tm, d), row)] * 3,
        out_shape=[jax.ShapeDtypeStruct((m, d), F32)] * 3,
        compiler_params=_cparams(("arbitrary",)),
        name="qkv_flat",
    )(x2d, g.reshape(1, d), w_bf, _head_gain(qg), _head_gain(kg))


def _qkv_perm(x, g, w_bf, qg, kg):
    b, t, d = x.shape
    ns = t // SUPER
    xv = x.reshape(b, t // MAX_DIL, MAX_DIL * d)
    tm = PERM_SL * ATT_BLOCK
    nq = MAX_DIL // PERM_SL
    npair = d // LANES
    pm_spec = pl.BlockSpec((None, npair, tm, LANES), lambda i, s, r: (i, 0, s * nq + r, 0))
    cache_spec = pl.BlockSpec((None, ATT_BLOCK, PERM_SL * d), lambda i, s, r: (i, s, r))
    pm_shape = jax.ShapeDtypeStruct((b, npair, t, LANES), F32)
    cache_shape = jax.ShapeDtypeStruct(xv.shape, F32)
    q, k, v, kc, vc = pl.pallas_call(
        functools.partial(_qkv_kernel, nsl=PERM_SL, pair_major=True),
        grid=(b, ns, nq),
        in_specs=[pl.BlockSpec((None, ATT_BLOCK, PERM_SL * d), lambda i, s, r: (i, s, r)),
                  _const_spec((1, d)), _const_spec(w_bf.shape), _const_spec((1, LANES)), _const_spec((1, LANES))],
        out_specs=[pm_spec, pm_spec, pm_spec, cache_spec, cache_spec],
        out_shape=[pm_shape, pm_shape, pm_shape, cache_shape, cache_shape],
        compiler_params=_cparams(("arbitrary",) * 3),
        name="qkv_perm",
    )(xv, g.reshape(1, d), w_bf, _head_gain(qg), _head_gain(kg))
    return q, k, v, kc.reshape(b, t, d)[:, t - SUPER:], vc.reshape(b, t, d)[:, t - SUPER:]


def _alibi_slopes():
    return jnp.power(2.0, -8.0 * jnp.arange(1, ATT_HEADS + 1, dtype=F32) / ATT_HEADS)


PREV_SLOTS = 32
ATT_UNROLL = 8
LOG2E = 1.4426950408889634


def _band_bias():
    i = jnp.arange(ATT_BLOCK)
    c = jnp.arange(2 * ATT_BLOCK)
    q_idx = {1: MAX_DIL * (i % 8) + i // 8,
             4: 4 * (i % 32) + i // 32,
             16: i}
    k_idx = {1: (MAX_DIL * (c % 8) + c // 16, (c // 8) % 2 == 0),
             4: (4 * (c % 32) + c // 64, (c // 32) % 2 == 0),
             16: (c % ATT_BLOCK, c < ATT_BLOCK)}
    slopes = _alibi_slopes().reshape(ATT_HEADS // 2, 2, 1, 1)
    out = []
    for (w, d) in PATTERNS:
        span = w // d
        km, is_prev = k_idx[d]
        diff = q_idx[d][:, None] - (km - jnp.where(is_prev, ATT_BLOCK, 0))[None, :]
        valid = (diff >= 0) & (diff <= span)
        bias = -slopes * (d * diff).astype(F32)[None, None]
        bias = bias * LOG2E
        variants = [jnp.where(valid, bias, NEG), jnp.where(valid & ~is_prev[None, :], bias, NEG)]
        out.append(jnp.stack([v.reshape(ATT_HEADS // 2, 2 * ATT_BLOCK, 2 * ATT_BLOCK) for v in variants]))
    return jnp.stack(out)


def _attend_pair(q, k, v, bias, first):
    qs = jnp.concatenate([jnp.where(first, q, 0.0), jnp.where(first, 0.0, q)], axis=0).astype(BF16)
    s = lax.dot_general(qs, k.astype(BF16), (((1,), (1,)), ((), ())), preferred_element_type=F32) + bias
    m = jnp.max(s, axis=-1, keepdims=True)
    e = jnp.exp2(s - m)
    den = jnp.sum(e, axis=-1, keepdims=True)
    o = jnp.dot(e.astype(BF16), v.astype(BF16), preferred_element_type=F32)
    halves = lambda a: jnp.where(first, a[:ATT_BLOCK], a[ATT_BLOCK:])
    return halves(o), halves(m), halves(den)


def _attn_prompt_kernel(q_ref, kc_ref, kp_ref, vc_ref, vp_ref, bias_ref, o_ref, kext, vext, oacc, macc, dacc):
    first_sb = (pl.program_id(1) == 0).astype(jnp.int32)
    first = lax.broadcasted_iota(jnp.int32, (ATT_BLOCK, LANES), 1) < ATT_DH
    scale = (ATT_DH ** -0.5) * LOG2E
    ps = PREV_SLOTS

    for src_p, src_c, ext in ((kp_ref, kc_ref, kext), (vp_ref, vc_ref, vext)):
        for r in range(MAX_DIL):
            ext[r, 0:ps, :] = src_p[r * ATT_BLOCK + ATT_BLOCK - ps:(r + 1) * ATT_BLOCK, :]
            ext[r, ps:ps + ATT_BLOCK, :] = src_c[r * ATT_BLOCK:(r + 1) * ATT_BLOCK, :]

    def attend_store(p, variant, q_chunks, k, v):
        n = ATT_BLOCK // len(q_chunks)
        q = jnp.concatenate([q_ref[sl, :] for sl in q_chunks], axis=0) * scale
        o, m, den = _attend_pair(q, k, v, bias_ref[p, variant], first)
        for ci, sl in enumerate(q_chunks):
            oacc[p, sl, :] = o[ci * n:(ci + 1) * n]
            macc[p, sl, :] = m[ci * n:(ci + 1) * n]
            dacc[p, sl, :] = den[ci * n:(ci + 1) * n]

    n0 = ATT_BLOCK // MAX_DIL

    def block0(bi):
        base = pl.multiple_of(bi * n0, n0)
        ksl = pl.ds(base + ps - n0, 2 * n0)
        k = jnp.concatenate([kext[r, ksl, :] for r in range(MAX_DIL)], axis=0)
        v = jnp.concatenate([vext[r, ksl, :] for r in range(MAX_DIL)], axis=0)
        attend_store(0, first_sb * jnp.asarray(bi == 0, jnp.int32),
                     [pl.ds(r * ATT_BLOCK + base, n0) for r in range(MAX_DIL)], k, v)

    d1 = PATTERNS[1][1]
    nc1 = MAX_DIL // d1
    n1 = ATT_BLOCK // nc1

    def block1(idx):
        r4, mb = idx // (ATT_BLOCK // n1), idx % (ATT_BLOCK // n1)
        base = pl.multiple_of(mb * n1, n1)
        ksl = pl.ds(base + ps - n1, 2 * n1)
        k = jnp.concatenate([kext[r4 + d1 * c, ksl, :] for c in range(nc1)], axis=0)
        v = jnp.concatenate([vext[r4 + d1 * c, ksl, :] for c in range(nc1)], axis=0)
        attend_store(1, first_sb * jnp.asarray(mb == 0, jnp.int32),
                     [pl.ds(pl.multiple_of((r4 + d1 * c) * ATT_BLOCK + base, n1), n1) for c in range(nc1)], k, v)

    def block2(r):
        sl = pl.ds(pl.multiple_of(r * ATT_BLOCK, ATT_BLOCK), ATT_BLOCK)
        k = jnp.concatenate([kp_ref[sl, :], kc_ref[sl, :]], axis=0)
        v = jnp.concatenate([vp_ref[sl, :], vc_ref[sl, :]], axis=0)
        attend_store(2, first_sb, [sl], k, v)

    for block in (block0, block1, block2):
        def body(it, carry, block=block):
            for u in range(ATT_UNROLL):
                block(it * ATT_UNROLL + u)
            return carry

        lax.fori_loop(0, MAX_DIL // ATT_UNROLL, body, 0)

    def comb(r, carry):
        sl = pl.ds(pl.multiple_of(r * ATT_BLOCK, ATT_BLOCK), ATT_BLOCK)
        m0, m1, m2 = macc[0, sl, :], macc[1, sl, :], macc[2, sl, :]
        mx = jnp.maximum(jnp.maximum(m0, m1), m2)
        w0, w1, w2 = jnp.exp2(m0 - mx), jnp.exp2(m1 - mx), jnp.exp2(m2 - mx)
        num = w0 * oacc[0, sl, :] + w1 * oacc[1, sl, :] + w2 * oacc[2, sl, :]
        den = w0 * dacc[0, sl, :] + w1 * dacc[1, sl, :] + w2 * dacc[2, sl, :]
        o_ref[sl, :] = (num / den).astype(o_ref.dtype)
        return carry

    lax.fori_loop(0, SUPER // ATT_BLOCK, comb, 0)


def _attn_prompt(q, k, v):
    b, npair, t, _ = q.shape
    ns = t // SUPER
    bias = _band_bias()
    cur = pl.BlockSpec((None, None, SUPER, LANES), lambda i, s, p: (i, p, s, 0))
    prev = pl.BlockSpec((None, None, SUPER, LANES), lambda i, s, p: (i, p, jnp.maximum(s - 1, 0), 0))
    return pl.pallas_call(
        _attn_prompt_kernel,
        grid=(b, ns, npair),
        in_specs=[cur, cur, prev, cur, prev,
                  pl.BlockSpec((len(PATTERNS), 2, None, 2 * ATT_BLOCK, 2 * ATT_BLOCK),
                               lambda i, s, p: (0, 0, p, 0, 0))],
        out_specs=pl.BlockSpec((None, SUPER, LANES), lambda i, s, p: (i, s, p)),
        out_shape=jax.ShapeDtypeStruct((b, t, npair * LANES), BF16),
        scratch_shapes=[pltpu.VMEM((MAX_DIL, PREV_SLOTS + ATT_BLOCK, LANES), F32),
                        pltpu.VMEM((MAX_DIL, PREV_SLOTS + ATT_BLOCK, LANES), F32),
                        pltpu.VMEM((len(PATTERNS), SUPER, LANES), F32),
                        pltpu.VMEM((len(PATTERNS), SUPER, LANES), F32),
                        pltpu.VMEM((len(PATTERNS), SUPER, LANES), F32)],
        compiler_params=_cparams(("arbitrary",) * 3),
        name="attn_prompt",
    )(q, k, k, v, v, bias)


QUAD = 4
NEW_PAD = 128


def _sample_bias(s_len, n_past):
    s = jnp.arange(s_len)
    col = jnp.arange(n_past + NEW_PAD)
    pos = jnp.where(col < n_past, col, n_past + (col - n_past))
    dist = (n_past + s)[:, None] - pos[None, :]
    real = (col < n_past + s_len)[None, :]
    slopes = _alibi_slopes().reshape(ATT_HEADS // QUAD, QUAD, 1, 1)
    out = []
    for (w, d) in PATTERNS:
        valid = (dist >= 0) & (dist % d == 0) & (dist <= w) & real
        bias = jnp.where(valid[None, None], -slopes * dist.astype(F32)[None, None], NEG)
        out.append(bias.reshape(ATT_HEADS // QUAD, QUAD * s_len, n_past + NEW_PAD))
    return jnp.stack(out)


def _attn_sample_kernel(q_ref, kn_ref, vn_ref, ck_ref, cv_ref, bias_ref, o_ref, *, s_len):
    n_past = ck_ref.shape[1]
    width = QUAD * ATT_DH
    rows = QUAD * s_len
    q = q_ref[...] * (ATT_DH ** -0.5)
    q4 = jnp.concatenate([q] * QUAD, axis=0)
    own = (lax.broadcasted_iota(jnp.int32, (rows, width), 0) // s_len
           == lax.broadcasted_iota(jnp.int32, (rows, width), 1) // ATT_DH)
    qm = jnp.where(own, q4, 0.0).astype(BF16)
    nt = (((1,), (1,)), ((), ()))
    kn = _pad_rows(kn_ref[...], NEW_PAD).astype(BF16)
    vn = _pad_rows(vn_ref[...], NEW_PAD).astype(BF16)
    s_c = jnp.dot(qm, ck_ref[...].astype(BF16), preferred_element_type=F32)
    s_n = lax.dot_general(qm, kn, nt, preferred_element_type=F32)
    es_c, es_n, dens, lses = [], [], [], []
    for p in range(len(PATTERNS)):
        b_c = bias_ref[p, :, 0:n_past]
        b_n = bias_ref[p, :, n_past:n_past + NEW_PAD]
        sc, sn = s_c + b_c, s_n + b_n
        m = jnp.maximum(jnp.max(sc, axis=-1, keepdims=True), jnp.max(sn, axis=-1, keepdims=True))
        ec, en = jnp.exp(sc - m), jnp.exp(sn - m)
        den = jnp.sum(ec, axis=-1, keepdims=True) + jnp.sum(en, axis=-1, keepdims=True)
        es_c.append(ec)
        es_n.append(en)
        dens.append(den)
        lses.append(m + jnp.log(den))
    mx = jnp.maximum(jnp.maximum(lses[0], lses[1]), lses[2])
    ws = [jnp.exp(l - mx) for l in lses]
    wsum = ws[0] + ws[1] + ws[2]
    coef = [w / (wsum * d) for w, d in zip(ws, dens)]
    c_c = coef[0] * es_c[0] + coef[1] * es_c[1] + coef[2] * es_c[2]
    c_n = coef[0] * es_n[0] + coef[1] * es_n[1] + coef[2] * es_n[2]
    o = (lax.dot_general(c_c.astype(BF16), cv_ref[...].astype(BF16), nt, preferred_element_type=F32)
         + jnp.dot(c_n.astype(BF16), vn, preferred_element_type=F32))
    o = jnp.where(own, o, 0.0)
    acc = o[0:s_len]
    for hh in range(1, QUAD):
        acc = acc + o[hh * s_len:(hh + 1) * s_len]
    o_ref[...] = acc.astype(o_ref.dtype)


def _attn_sample(q, kn, vn, ck, cv):
    b, s_len, d = q.shape
    n_past = ck.shape[2]
    width = QUAD * ATT_DH
    nquad = d // width
    bias = _sample_bias(s_len, n_past)
    new = pl.BlockSpec((None, s_len, width), lambda i, h: (i, 0, h))
    old = pl.BlockSpec((None, width, n_past), lambda i, h: (i, h, 0))
    return pl.pallas_call(
        functools.partial(_attn_sample_kernel, s_len=s_len),
        grid=(b, nquad),
        in_specs=[new, new, new, old, old,
                  pl.BlockSpec((len(PATTERNS), None, QUAD * s_len, n_past + NEW_PAD), lambda i, h: (0, h, 0, 0))],
        out_specs=new,
        out_shape=jax.ShapeDtypeStruct((b, s_len, d), BF16),
        compiler_params=_cparams(("arbitrary",) * 2),
        name="attn_sample",
    )(q, kn, vn, ck, cv, bias)


def kernel(x_prompt, x_sample, state_ret, state_lru, state_conv, cache_k, cache_v, norm_mix, norm_ffn, w_in_rec, w_out_rec, ret_norm_g, conv_w, conv_b, w_rgate, b_rgate, w_igate, b_igate, lru_lambda, w_in_att, w_out_att, q_norm_g, k_norm_g, w_ffn_gate, w_ffn_up, w_ffn_down):
    bp, tp, d = x_prompt.shape
    bs, ts, _ = x_sample.shape
    depth = norm_mix.shape[0]
    assert tp % SUPER == 0 and d == D_MODEL
    yp, ys = x_prompt, x_sample
    ret_p, lru_p, conv_p, kp, vp = [], [], [], [], []
    ret_s, lru_s, conv_s, ksl, vsl = [], [], [], [], []
    for l in range(depth):
        i = l // 2
        if l % 2 == 0:
            ffn = _ffn_weights(w_out_rec[i], norm_ffn[l], w_ffn_gate[l], w_ffn_up[l], w_ffn_down[l])
            w_in = w_in_rec[i].astype(BF16)
            rec_prm = (ret_norm_g[i], conv_w[i], conv_b[i], w_rgate[i], b_rgate[i], w_igate[i], b_igate[i],
                       lru_lambda[i])
            proj = _norm_proj(yp.reshape(bp * tp, d), norm_mix[l], w_in).reshape(bp, tp, REC_IN)
            mix, sr, sl, sc = _rec_mix(proj, jnp.zeros((bp, RET_HEADS, RET_DK, RET_DV), F32),
                                       jnp.zeros((bp, LRU_WIDTH), F32),
                                       jnp.zeros((bp, CONV_WIDTH - 1, LRU_WIDTH), F32), *rec_prm)
            yp = _out_ffn(mix.reshape(bp * tp, d), yp.reshape(bp * tp, d), ffn).reshape(bp, tp, d)
            ret_p.append(sr); lru_p.append(sl); conv_p.append(sc)
            proj = _norm_proj(ys.reshape(bs * ts, d), norm_mix[l], w_in).reshape(bs, ts, REC_IN)
            mix, sr, sl, sc = _rec_mix(proj, state_ret[i], state_lru[i], state_conv[i], *rec_prm)
            ys = _out_ffn(mix.reshape(bs * ts, d), ys.reshape(bs * ts, d), ffn).reshape(bs, ts, d)
            ret_s.append(sr); lru_s.append(sl); conv_s.append(sc)
        else:
            ffn = _ffn_weights(w_out_att[i], norm_ffn[l], w_ffn_gate[l], w_ffn_up[l], w_ffn_down[l])
            w_in = w_in_att[i].astype(BF16)
            q, k, v, kc, vc = _qkv_perm(yp, norm_mix[l], w_in, q_norm_g[i], k_norm_g[i])
            o = _attn_prompt(q, k, v)
            yp = _out_ffn_perm(o, yp, ffn)
            kp.append(kc.reshape(bp, SUPER, ATT_HEADS, ATT_DH)); vp.append(vc.reshape(bp, SUPER, ATT_HEADS, ATT_DH))
            qs, kn, vn = _qkv_flat(ys.reshape(bs * ts, d), norm_mix[l], w_in, q_norm_g[i], k_norm_g[i])
            n_past = cache_k.shape[2]
            to_slabs = lambda c: jnp.transpose(c, (0, 2, 3, 1)).reshape(bs, d, n_past)
            o = _attn_sample(qs.reshape(bs, ts, d), kn.reshape(bs, ts, d), vn.reshape(bs, ts, d),
                             to_slabs(cache_k[i]), to_slabs(cache_v[i]))
            ys = _out_ffn(o.reshape(bs * ts, d), ys.reshape(bs * ts, d), ffn).reshape(bs, ts, d)
            ksl.append(kn.reshape(bs, ts, ATT_HEADS, ATT_DH)); vsl.append(vn.reshape(bs, ts, ATT_HEADS, ATT_DH))
    return (yp, ys,
            jnp.stack(ret_p), jnp.stack(lru_p), jnp.stack(conv_p), jnp.stack(kp), jnp.stack(vp),
            jnp.stack(ret_s), jnp.stack(lru_s), jnp.stack(conv_s), jnp.stack(ksl), jnp.stack(vsl))
```

```python
import functools

import jax
import jax.numpy as jnp
from jax import lax
from jax.experimental import pallas as pl
from jax.experimental.pallas import tpu as pltpu

F32 = jnp.float32
BF16 = jnp.bfloat16

EPS = 1e-6
D_MODEL = 1024
RET_HEADS = 4
RET_DK = 64
RET_DV = 128
RET_CHUNK = 128
LRU_WIDTH = 512
LRU_BLOCKS = 4
LRU_BLOCK = LRU_WIDTH // LRU_BLOCKS
CONV_WIDTH = 4
LRU_C = 8.0
REC_IN = 2 * RET_HEADS * RET_DK + 2 * RET_HEADS * RET_DV + 2 * LRU_WIDTH
ATT_HEADS = 16
ATT_DH = 64
PATTERNS = ((128, 1), (512, 4), (2048, 16))
ATT_BLOCK = 128
SUPER = 2048
MAX_DIL = 16
NEG = -1e30

LANES = 128
SUBLANES = 8
NPAIR = D_MODEL // LANES
VMEM_LIMIT = 56 * 1024 * 1024

FF_CHUNK = 256
TOKEN_TILE = 512


def _cparams(sem):
    return pltpu.CompilerParams(dimension_semantics=sem, vmem_limit_bytes=VMEM_LIMIT)


def _const_spec(shape):
    nd = len(shape)
    return pl.BlockSpec(shape, lambda *_: (0,) * nd, pipeline_mode=pl.Buffered(1))


def _rmsnorm(x, g):
    return x * lax.rsqrt(jnp.mean(x * x, axis=-1, keepdims=True) + EPS) * g


def _norm_proj_kernel(x_ref, g_ref, w_ref, o_ref):
    h = _rmsnorm(x_ref[...], g_ref[...]).astype(BF16)
    o_ref[...] = jnp.dot(h, w_ref[...], preferred_element_type=F32)


def _norm_proj(x2d, g, w_bf):
    m, d = x2d.shape
    n = w_bf.shape[1]
    tm = min(TOKEN_TILE, m)
    return pl.pallas_call(
        _norm_proj_kernel,
        grid=(m // tm,),
        in_specs=[pl.BlockSpec((tm, d), lambda i: (i, 0)), _const_spec((1, d)), _const_spec((d, n))],
        out_specs=pl.BlockSpec((tm, n), lambda i: (i, 0)),
        out_shape=jax.ShapeDtypeStruct((m, n), F32),
        compiler_params=_cparams(("arbitrary",)),
        name="norm_proj",
    )(x2d, g.reshape(1, d), w_bf)


def _shift_rows(x, s, fill):
    rows = lax.broadcasted_iota(jnp.int32, x.shape, 0)
    return jnp.where(rows < s, fill, pltpu.roll(x, s, axis=0))


def _pad_rows(x, rows):
    if x.shape[0] == rows:
        return x
    return jnp.concatenate([x, jnp.zeros((rows - x.shape[0], x.shape[1]), x.dtype)], axis=0)


def _rec_mix_kernel(proj_ref, s0_ref, h0_ref, c0_ref, dec_ref, qdec_ref, kdec_ref, cdec_ref,
                    retg_ref, convw_ref, convb_ref, wrg_ref, brg_ref, wig_ref, big_ref, lam_ref,
                    mix_ref, sret_ref, hlast_ref, cnew_ref,
                    s_scr, h_scr, xe_scr, *, rows):
    c = pl.program_id(1)
    cp = RET_CHUNK
    hk = RET_HEADS * RET_DK
    v_off, g_off = 2 * hk, 2 * hk + RET_HEADS * RET_DV
    xl_off = g_off + RET_HEADS * RET_DV
    yl_off = xl_off + LRU_WIDTH

    @pl.when(c == 0)
    def _():
        s_scr[...] = s0_ref[...]
        h_scr[...] = h0_ref[...]
        xe_scr[0:SUBLANES, :] = jnp.zeros((SUBLANES, LRU_WIDTH), F32)
        xe_scr[SUBLANES - (CONV_WIDTH - 1):SUBLANES, :] = c0_ref[...]

    lane = lax.broadcasted_iota(jnp.int32, (cp, LANES), 1)
    first = lane < RET_DK
    srow_first = lax.broadcasted_iota(jnp.int32, (2 * RET_DK, RET_DV), 0) < RET_DK

    for p in range(RET_HEADS // 2):
        q2 = _pad_rows(proj_ref[:, p * LANES:(p + 1) * LANES], cp)
        k2 = _pad_rows(proj_ref[:, hk + p * LANES:hk + (p + 1) * LANES], cp) * (RET_DK ** -0.5)
        kb = k2.astype(BF16)
        kdt = jnp.transpose(k2 * kdec_ref[p]).astype(BF16)
        s2 = s_scr[p]
        s2b = s2.astype(BF16)
        new_s = []
        for hh in range(2):
            h = 2 * p + hh
            sel = first if hh == 0 else jnp.logical_not(first)
            qh = jnp.where(sel, q2, 0.0)
            v = _pad_rows(proj_ref[:, v_off + h * RET_DV:v_off + (h + 1) * RET_DV], cp)
            vb = v.astype(BF16)
            att = lax.dot_general(qh.astype(BF16), kb, (((1,), (1,)), ((), ())),
                                  preferred_element_type=F32) * dec_ref[h]
            o = (jnp.dot(att.astype(BF16), vb, preferred_element_type=F32)
                 + jnp.dot((qh * qdec_ref[p]).astype(BF16), s2b, preferred_element_type=F32))
            new_s.append(jnp.dot(kdt, vb, preferred_element_type=F32))
            o = o[:rows]
            o = o * lax.rsqrt(jnp.mean(o * o, axis=-1, keepdims=True) + EPS) * retg_ref[h:h + 1, :]
            gate = proj_ref[:, g_off + h * RET_DV:g_off + (h + 1) * RET_DV]
            mix_ref[:, h * RET_DV:(h + 1) * RET_DV] = (o * (gate * jax.nn.sigmoid(gate))).astype(mix_ref.dtype)
        s_scr[p] = s2 * cdec_ref[p] + jnp.where(srow_first, new_s[0], new_s[1])

    x = proj_ref[:, xl_off:xl_off + LRU_WIDTH]
    xe_scr[SUBLANES:SUBLANES + rows, :] = x
    xc = convb_ref[...] + xe_scr[pl.ds(SUBLANES - 3, rows), :] * convw_ref[0:1, :]
    for j in range(1, CONV_WIDTH):
        xc = xc + xe_scr[pl.ds(SUBLANES - 3 + j, rows), :] * convw_ref[j:j + 1, :]
    tail = xe_scr[rows:rows + SUBLANES, :]
    xe_scr[0:SUBLANES, :] = tail
    cnew_ref[...] = xe_scr[pl.ds(SUBLANES - (CONV_WIDTH - 1), CONV_WIDTH - 1), :]

    xcb = _pad_rows(xc, max(rows, 2 * SUBLANES)).astype(BF16)
    r_parts, i_parts = [], []
    for n in range(LRU_BLOCKS):
        xb = xcb[:, n * LRU_BLOCK:(n + 1) * LRU_BLOCK]
        r_parts.append(jnp.dot(xb, wrg_ref[n], preferred_element_type=F32)[:rows])
        i_parts.append(jnp.dot(xb, wig_ref[n], preferred_element_type=F32)[:rows])
    r = jax.nn.sigmoid(jnp.concatenate(r_parts, axis=1) + brg_ref[...])
    ig = jax.nn.sigmoid(jnp.concatenate(i_parts, axis=1) + big_ref[...])
    nl = -lam_ref[...]
    softplus = jnp.maximum(nl, 0.0) + jnp.log1p(jnp.exp(-jnp.abs(nl)))
    log_a = (-LRU_C) * r * softplus
    a = jnp.exp(log_a)
    b = jnp.sqrt(1.0 - jnp.exp(2.0 * log_a)) * (ig * xc)

    s = 1
    while s < rows:
        b = a * _shift_rows(b, s, 0.0) + b
        a = a * _shift_rows(a, s, 1.0)
        s *= 2
    hseq = b + a * h_scr[...]
    h_scr[...] = hseq[rows - 1:rows, :]
    hlast_ref[...] = hseq[rows - 1:rows, :]
    yl = proj_ref[:, yl_off:yl_off + LRU_WIDTH]
    gelu = 0.5 * yl * (1.0 + jnp.tanh(0.7978845608028654 * (yl + 0.044715 * (yl * yl * yl))))
    mix_ref[:, RET_HEADS * RET_DV:] = (hseq * gelu).astype(mix_ref.dtype)
    sret_ref[...] = s_scr[...]


def _ret_tables(t_eff):
    cp = RET_CHUNK
    log_g = jnp.log1p(-jnp.power(2.0, -5.0 - jnp.arange(RET_HEADS, dtype=F32)))
    idx = jnp.arange(cp, dtype=F32)
    live = idx < t_eff
    diff = idx[:, None] - idx[None, :]
    ok = (diff >= 0) & live[:, None] & live[None, :]
    decay = jnp.where(ok, jnp.exp(log_g[:, None, None] * jnp.maximum(diff, 0.0)), 0.0)
    q_dec = jnp.where(live, jnp.exp(log_g[:, None] * (idx + 1.0)), 0.0)
    k_dec = jnp.where(live, jnp.exp(log_g[:, None] * (t_eff - 1.0 - idx)), 0.0)
    c_dec = jnp.exp(log_g * t_eff)

    def pair_lanes(t):
        t = t.reshape(RET_HEADS // 2, 2, cp)
        return jnp.repeat(jnp.transpose(t, (0, 2, 1)), RET_DK, axis=2)

    c_rows = jnp.repeat(c_dec.reshape(RET_HEADS // 2, 2), RET_DK, axis=1)
    c_rows = jnp.broadcast_to(c_rows[:, :, None], (RET_HEADS // 2, 2 * RET_DK, RET_DV))
    return decay, pair_lanes(q_dec), pair_lanes(k_dec), c_rows


def _rec_mix(proj, s0, h0, c0, ret_g, conv_w, conv_b, w_rg, b_rg, w_ig, b_ig, lam):
    b, t, _ = proj.shape
    rows = RET_CHUNK if t % RET_CHUNK == 0 else t
    assert t % rows == 0 and rows % SUBLANES == 0 and (t == rows or rows == RET_CHUNK)
    n = t // rows
    dec, qdec, kdec, cdec = _ret_tables(rows)
    hp = RET_HEADS // 2
    w = LRU_WIDTH
    per_b3 = lambda i, c: (i, 0, 0)
    outs = pl.pallas_call(
        functools.partial(_rec_mix_kernel, rows=rows),
        grid=(b, n),
        in_specs=[
            pl.BlockSpec((None, rows, REC_IN), lambda i, c: (i, c, 0)),
            pl.BlockSpec((None, hp, 2 * RET_DK, RET_DV), lambda i, c: (i, 0, 0, 0)),
            pl.BlockSpec((None, 1, w), per_b3),
            pl.BlockSpec((None, CONV_WIDTH - 1, w), per_b3),
            _const_spec(dec.shape), _const_spec(qdec.shape), _const_spec(kdec.shape), _const_spec(cdec.shape),
            _const_spec((RET_HEADS, RET_DV)), _const_spec((CONV_WIDTH, w)), _const_spec((1, w)),
            _const_spec((LRU_BLOCKS, LRU_BLOCK, LRU_BLOCK)), _const_spec((1, w)),
            _const_spec((LRU_BLOCKS, LRU_BLOCK, LRU_BLOCK)), _const_spec((1, w)), _const_spec((1, w)),
        ],
        out_specs=[
            pl.BlockSpec((None, rows, D_MODEL), lambda i, c: (i, c, 0)),
            pl.BlockSpec((None, hp, 2 * RET_DK, RET_DV), lambda i, c: (i, 0, 0, 0)),
            pl.BlockSpec((None, 1, w), per_b3),
            pl.BlockSpec((None, CONV_WIDTH - 1, w), per_b3),
        ],
        out_shape=[
            jax.ShapeDtypeStruct((b, t, D_MODEL), BF16),
            jax.ShapeDtypeStruct((b, hp, 2 * RET_DK, RET_DV), F32),
            jax.ShapeDtypeStruct((b, 1, w), F32),
            jax.ShapeDtypeStruct((b, CONV_WIDTH - 1, w), F32),
        ],
        scratch_shapes=[
            pltpu.VMEM((hp, 2 * RET_DK, RET_DV), F32),
            pltpu.VMEM((1, w), F32),
            pltpu.VMEM((rows + SUBLANES, w), F32),
        ],
        compiler_params=_cparams(("arbitrary", "arbitrary")),
        name="rec_mix",
    )(proj, s0.reshape(b, hp, 2 * RET_DK, RET_DV), h0.reshape(b, 1, w), c0,
      dec, qdec, kdec, cdec, ret_g, conv_w, conv_b.reshape(1, w),
      w_rg.astype(BF16), b_rg.reshape(1, w), w_ig.astype(BF16), b_ig.reshape(1, w), lam.reshape(1, w))
    mix, s_ret, h_last, c_new = outs
    return mix, s_ret.reshape(b, RET_HEADS, RET_DK, RET_DV), h_last.reshape(b, w), c_new


def _out_ffn_kernel(mix_ref, x_ref, wo_ref, g_ref, wg_ref, wu_ref, wd_ref, o_ref, act_ref):
    y = x_ref[...] + jnp.dot(mix_ref[...].astype(BF16), wo_ref[...], preferred_element_type=F32)
    h = _rmsnorm(y, g_ref[...]).astype(BF16)
    for c in range(wg_ref.shape[1] // FF_CHUNK):
        cols = slice(c * FF_CHUNK, (c + 1) * FF_CHUNK)
        gate = jnp.dot(h, wg_ref[:, cols], preferred_element_type=F32)
        up = jnp.dot(h, wu_ref[:, cols], preferred_element_type=F32)
        act_ref[:, cols] = (gate * jax.nn.sigmoid(gate) * up).astype(BF16)
    o_ref[...] = y + jnp.dot(act_ref[...], wd_ref[...], preferred_element_type=F32)


def _ffn_weights(w_out, g, w_gate, w_up, w_down):
    return (w_out.astype(BF16), g.reshape(1, g.shape[0]), w_gate.astype(BF16), w_up.astype(BF16),
            w_down.astype(BF16))


def _out_ffn(mix2d, x2d, wts):
    m, d = x2d.shape
    tm = min(TOKEN_TILE, m)
    row = lambda i: (i, 0)
    return pl.pallas_call(
        _out_ffn_kernel,
        grid=(m // tm,),
        in_specs=[pl.BlockSpec((tm, d), row), pl.BlockSpec((tm, d), row)] + [_const_spec(w.shape) for w in wts],
        out_specs=pl.BlockSpec((tm, d), row),
        out_shape=jax.ShapeDtypeStruct((m, d), F32),
        scratch_shapes=[pltpu.VMEM((tm, wts[2].shape[1]), BF16)],
        compiler_params=_cparams(("arbitrary",)),
        name="out_ffn",
    )(mix2d, x2d, *wts)


def _pair_rms(a, g2, first):
    sq = a * a
    s_a = jnp.sum(jnp.where(first, sq, 0.0), axis=-1, keepdims=True)
    s_b = jnp.sum(jnp.where(first, 0.0, sq), axis=-1, keepdims=True)
    ms = jnp.where(first, s_a, s_b) * (1.0 / ATT_DH)
    return a * lax.rsqrt(ms + EPS) * g2


def _qkv_slabs(x, g_ref, w_ref, qg_ref, kg_ref):
    h = _rmsnorm(x, g_ref[...]).astype(BF16)
    qkv = jnp.dot(h, w_ref[...], preferred_element_type=F32)
    first = lax.broadcasted_iota(jnp.int32, (x.shape[0], LANES), 1) < ATT_DH
    slab = lambda base, p: qkv[:, base + p * LANES:base + (p + 1) * LANES]
    q = [_pair_rms(slab(0, p), qg_ref[...], first) for p in range(NPAIR)]
    k = [_pair_rms(slab(D_MODEL, p), kg_ref[...], first) for p in range(NPAIR)]
    v = [slab(2 * D_MODEL, p) for p in range(NPAIR)]
    return q, k, v


def _qkv_flat_kernel(x_ref, g_ref, w_ref, qg_ref, kg_ref, q_ref, k_ref, v_ref):
    q, k, v = _qkv_slabs(x_ref[...], g_ref, w_ref, qg_ref, kg_ref)
    q_ref[...] = jnp.concatenate(q, axis=1)
    k_ref[...] = jnp.concatenate(k, axis=1)
    v_ref[...] = jnp.concatenate(v, axis=1)


def _qkv_perm_kernel(x_ref, g_ref, w_ref, qg_ref, kg_ref, q_ref, k_ref, v_ref, xs_ref):
    n = x_ref.shape[0] // MAX_DIL
    for p in range(NPAIR):
        xs_ref[p] = x_ref[:, p * LANES:(p + 1) * LANES]
    x = jnp.concatenate(
        [jnp.concatenate([xs_ref[p, pl.ds(r, n, stride=MAX_DIL), :] for p in range(NPAIR)], axis=1)
         for r in range(MAX_DIL)], axis=0)
    q, k, v = _qkv_slabs(x, g_ref, w_ref, qg_ref, kg_ref)
    for ref, slabs in ((q_ref, q), (k_ref, k), (v_ref, v)):
        for p in range(NPAIR):
            for r in range(MAX_DIL):
                ref[p, r] = slabs[p][r * n:(r + 1) * n]


def _head_gain(g):
    return jnp.tile(g.reshape(1, ATT_DH), (1, LANES // ATT_DH))


def _qkv_flat(x2d, g, w_bf, qg, kg):
    m, d = x2d.shape
    tm = min(TOKEN_TILE, m)
    row = lambda i: (i, 0)
    return pl.pallas_call(
        _qkv_flat_kernel,
        grid=(m // tm,),
        in_specs=[pl.BlockSpec((tm, d), row), _const_spec((1, d)), _const_spec(w_bf.shape),
                  _const_spec((1, LANES)), _const_spec((1, LANES))],
        out_specs=[pl.BlockSpec((tm, d), row)] * 3,
        out_shape=[jax.ShapeDtypeStruct((m, d), F32)] * 3,
        compiler_params=_cparams(("arbitrary",)),
        name="qkv_flat",
    )(x2d, g.reshape(1, d), w_bf, _head_gain(qg), _head_gain(kg))


def _qkv_perm(x, g, w_bf, qg, kg):
    b, t, d = x.shape
    ns = t // SUPER
    tm = TOKEN_TILE
    nt = SUPER // tm
    n = tm // MAX_DIL
    out_spec = pl.BlockSpec((None, NPAIR, None, MAX_DIL, None, n, LANES), lambda i, s, j: (i, 0, s, 0, j, 0, 0))
    out_shape = jax.ShapeDtypeStruct((b, NPAIR, ns, MAX_DIL, nt, n, LANES), F32)
    q, k, v = pl.pallas_call(
        _qkv_perm_kernel,
        grid=(b, ns, nt),
        in_specs=[pl.BlockSpec((None, tm, d), lambda i, s, j: (i, s * nt + j, 0)),
                  _const_spec((1, d)), _const_spec(w_bf.shape), _const_spec((1, LANES)), _const_spec((1, LANES))],
        out_specs=[out_spec] * 3,
        out_shape=[out_shape] * 3,
        scratch_shapes=[pltpu.VMEM((NPAIR, tm, LANES), F32)],
        compiler_params=_cparams(("arbitrary",) * 3),
        name="qkv_perm",
    )(x, g.reshape(1, d), w_bf, _head_gain(qg), _head_gain(kg))
    return tuple(a.reshape(b, NPAIR, t, LANES) for a in (q, k, v))


def _window_cache_kernel(k_ref, v_ref, ko_ref, vo_ref, nat_ref):
    for src, dst in ((k_ref, ko_ref), (v_ref, vo_ref)):
        for r in range(MAX_DIL):
            nat_ref[pl.ds(r, ATT_BLOCK, stride=MAX_DIL), :] = src[r * ATT_BLOCK:(r + 1) * ATT_BLOCK, :]
        dst[...] = jnp.transpose(nat_ref[...])


def _window_cache(k, v):
    b, npair, t, _ = k.shape
    last = t // SUPER - 1
    src = pl.BlockSpec((None, None, SUPER, LANES), lambda i, p: (i, p, last, 0))
    dst = pl.BlockSpec((None, LANES, SUPER), lambda i, p: (i, p, 0))
    shape = jax.ShapeDtypeStruct((b, npair * LANES, SUPER), F32)
    return pl.pallas_call(
        _window_cache_kernel,
        grid=(b, npair),
        in_specs=[src, src],
        out_specs=[dst, dst],
        out_shape=[shape, shape],
        scratch_shapes=[pltpu.VMEM((SUPER, LANES), F32)],
        compiler_params=_cparams(("arbitrary",) * 2),
        name="window_cache",
    )(k, v)


def _alibi_slopes():
    return jnp.power(2.0, -8.0 * jnp.arange(1, ATT_HEADS + 1, dtype=F32) / ATT_HEADS)


PREV_SLOTS = 32
ATT_UNROLL = 8
LOG2E = 1.4426950408889634


def _band_bias():
    i = jnp.arange(ATT_BLOCK)
    c = jnp.arange(2 * ATT_BLOCK)
    q_idx = {1: MAX_DIL * (i % 8) + i // 8,
             4: 4 * (i % 32) + i // 32,
             16: i}
    k_idx = {1: (MAX_DIL * (c % 8) + c // 16, (c // 8) % 2 == 0),
             4: (4 * (c % 32) + c // 64, (c // 32) % 2 == 0),
             16: (c % ATT_BLOCK, c < ATT_BLOCK)}
    slopes = _alibi_slopes().reshape(ATT_HEADS // 2, 2, 1, 1)
    out = []
    for (w, d) in PATTERNS:
        span = w // d
        km, is_prev = k_idx[d]
        diff = q_idx[d][:, None] - (km - jnp.where(is_prev, ATT_BLOCK, 0))[None, :]
        valid = (diff >= 0) & (diff <= span)
        bias = -slopes * (d * diff).astype(F32)[None, None]
        bias = bias * LOG2E
        variants = [jnp.where(valid, bias, NEG), jnp.where(valid & ~is_prev[None, :], bias, NEG)]
        out.append(jnp.stack([v.reshape(ATT_HEADS // 2, 2 * ATT_BLOCK, 2 * ATT_BLOCK) for v in variants]))
    return jnp.stack(out)


def _attend_pair(q, k, v, bias, first):
    qs = jnp.concatenate([jnp.where(first, q, 0.0), jnp.where(first, 0.0, q)], axis=0).astype(BF16)
    s = lax.dot_general(qs, k.astype(BF16), (((1,), (1,)), ((), ())), preferred_element_type=F32) + bias
    m = jnp.max(s, axis=-1, keepdims=True)
    e = jnp.exp2(s - m)
    den = jnp.sum(e, axis=-1, keepdims=True)
    o = jnp.dot(e.astype(BF16), v.astype(BF16), preferred_element_type=F32)
    halves = lambda a: jnp.where(first, a[:ATT_BLOCK], a[ATT_BLOCK:])
    return halves(o), halves(m), halves(den)


def _attn_prompt_kernel(q_ref, kc_ref, kp_ref, vc_ref, vp_ref, bias_ref, o_ref, kext, vext, oacc, macc, dacc):
    first_sb = (pl.program_id(1) == 0).astype(jnp.int32)
    first = lax.broadcasted_iota(jnp.int32, (ATT_BLOCK, LANES), 1) < ATT_DH
    scale = (ATT_DH ** -0.5) * LOG2E
    ps = PREV_SLOTS

    for src_p, src_c, ext in ((kp_ref, kc_ref, kext), (vp_ref, vc_ref, vext)):
        for r in range(MAX_DIL):
            ext[r, 0:ps, :] = src_p[r * ATT_BLOCK + ATT_BLOCK - ps:(r + 1) * ATT_BLOCK, :]
            ext[r, ps:ps + ATT_BLOCK, :] = src_c[r * ATT_BLOCK:(r + 1) * ATT_BLOCK, :]

    def attend_store(p, variant, q_chunks, k, v):
        n = ATT_BLOCK // len(q_chunks)
        q = jnp.concatenate([q_ref[sl, :] for sl in q_chunks], axis=0) * scale
        o, m, den = _attend_pair(q, k, v, bias_ref[p, variant], first)
        for ci, sl in enumerate(q_chunks):
            oacc[p, sl, :] = o[ci * n:(ci + 1) * n]
            macc[p, sl, :] = m[ci * n:(ci + 1) * n]
            dacc[p, sl, :] = den[ci * n:(ci + 1) * n]

    n0 = ATT_BLOCK // MAX_DIL

    def block0(bi):
        base = pl.multiple_of(bi * n0, n0)
        ksl = pl.ds(base + ps - n0, 2 * n0)
        k = jnp.concatenate([kext[r, ksl, :] for r in range(MAX_DIL)], axis=0)
        v = jnp.concatenate([vext[r, ksl, :] for r in range(MAX_DIL)], axis=0)
        attend_store(0, first_sb * jnp.asarray(bi == 0, jnp.int32),
                     [pl.ds(r * ATT_BLOCK + base, n0) for r in range(MAX_DIL)], k, v)

    d1 = PATTERNS[1][1]
    nc1 = MAX_DIL // d1
    n1 = ATT_BLOCK // nc1

    def block1(idx):
        r4, mb = idx // (ATT_BLOCK // n1), idx % (ATT_BLOCK // n1)
        base = pl.multiple_of(mb * n1, n1)
        ksl = pl.ds(base + ps - n1, 2 * n1)
        k = jnp.concatenate([kext[r4 + d1 * c, ksl, :] for c in range(nc1)], axis=0)
        v = jnp.concatenate([vext[r4 + d1 * c, ksl, :] for c in range(nc1)], axis=0)
        attend_store(1, first_sb * jnp.asarray(mb == 0, jnp.int32),
                     [pl.ds(pl.multiple_of((r4 + d1 * c) * ATT_BLOCK + base, n1), n1) for c in range(nc1)], k, v)

    def block2(r):
        sl = pl.ds(pl.multiple_of(r * ATT_BLOCK, ATT_BLOCK), ATT_BLOCK)
        k = jnp.concatenate([kp_ref[sl, :], kc_ref[sl, :]], axis=0)
        v = jnp.concatenate([vp_ref[sl, :], vc_ref[sl, :]], axis=0)
        attend_store(2, first_sb, [sl], k, v)

    for block in (block0, block1, block2):
        def body(it, carry, block=block):
            for u in range(ATT_UNROLL):
                block(it * ATT_UNROLL + u)
            return carry

        lax.fori_loop(0, MAX_DIL // ATT_UNROLL, body, 0)

    def comb(r, carry):
        sl = pl.ds(pl.multiple_of(r * ATT_BLOCK, ATT_BLOCK), ATT_BLOCK)
        m0, m1, m2 = macc[0, sl, :], macc[1, sl, :], macc[2, sl, :]
        mx = jnp.maximum(jnp.maximum(m0, m1), m2)
        w0, w1, w2 = jnp.exp2(m0 - mx), jnp.exp2(m1 - mx), jnp.exp2(m2 - mx)
        num = w0 * oacc[0, sl, :] + w1 * oacc[1, sl, :] + w2 * oacc[2, sl, :]
        den = w0 * dacc[0, sl, :] + w1 * dacc[1, sl, :] + w2 * dacc[2, sl, :]
        o_ref[pl.ds(r, ATT_BLOCK, stride=MAX_DIL), :] = num / den
        return carry

    lax.fori_loop(0, SUPER // ATT_BLOCK, comb, 0)


def _attn_prompt(q, k, v):
    b, npair, t, _ = q.shape
    ns = t // SUPER
    bias = _band_bias()
    cur = pl.BlockSpec((None, None, SUPER, LANES), lambda i, s, p: (i, p, s, 0))
    prev = pl.BlockSpec((None, None, SUPER, LANES), lambda i, s, p: (i, p, jnp.maximum(s - 1, 0), 0))
    return pl.pallas_call(
        _attn_prompt_kernel,
        grid=(b, ns, npair),
        in_specs=[cur, cur, prev, cur, prev,
                  pl.BlockSpec((len(PATTERNS), 2, None, 2 * ATT_BLOCK, 2 * ATT_BLOCK),
                               lambda i, s, p: (0, 0, p, 0, 0))],
        out_specs=pl.BlockSpec((None, SUPER, LANES), lambda i, s, p: (i, s, p)),
        out_shape=jax.ShapeDtypeStruct((b, t, npair * LANES), F32),
        scratch_shapes=[pltpu.VMEM((MAX_DIL, PREV_SLOTS + ATT_BLOCK, LANES), F32),
                        pltpu.VMEM((MAX_DIL, PREV_SLOTS + ATT_BLOCK, LANES), F32),
                        pltpu.VMEM((len(PATTERNS), SUPER, LANES), F32),
                        pltpu.VMEM((len(PATTERNS), SUPER, LANES), F32),
                        pltpu.VMEM((len(PATTERNS), SUPER, LANES), F32)],
        compiler_params=_cparams(("arbitrary",) * 3),
        name="attn_prompt",
    )(q, k, k, v, v, bias)


QUAD = 4
NEW_PAD = 128


def _sample_bias(s_len, n_past):
    s = jnp.arange(s_len)
    col = jnp.arange(n_past + NEW_PAD)
    dist = (n_past + s)[:, None] - col[None, :]
    real = (col < n_past + s_len)[None, :]
    slopes = _alibi_slopes().reshape(ATT_HEADS // QUAD, QUAD, 1, 1)
    out = []
    for (w, d) in PATTERNS:
        valid = (dist >= 0) & (dist % d == 0) & (dist <= w) & real
        bias = jnp.where(valid[None, None], -slopes * dist.astype(F32)[None, None], NEG)
        out.append(bias.reshape(ATT_HEADS // QUAD, QUAD * s_len, n_past + NEW_PAD))
    return jnp.stack(out)


def _attn_sample_kernel(q_ref, kn_ref, vn_ref, ck_ref, cv_ref, bias_ref, o_ref, *, s_len):
    n_past = ck_ref.shape[1]
    width = QUAD * ATT_DH
    rows = QUAD * s_len
    q = q_ref[...] * (ATT_DH ** -0.5)
    q4 = jnp.concatenate([q] * QUAD, axis=0)
    own = (lax.broadcasted_iota(jnp.int32, (rows, width), 0) // s_len
           == lax.broadcasted_iota(jnp.int32, (rows, width), 1) // ATT_DH)
    qm = jnp.where(own, q4, 0.0).astype(BF16)
    nt = (((1,), (1,)), ((), ()))
    kn = _pad_rows(kn_ref[...], NEW_PAD).astype(BF16)
    vn = _pad_rows(vn_ref[...], NEW_PAD).astype(BF16)
    s_c = jnp.dot(qm, ck_ref[...].astype(BF16), preferred_element_type=F32)
    s_n = lax.dot_general(qm, kn, nt, preferred_element_type=F32)
    es_c, es_n, dens, lses = [], [], [], []
    for p in range(len(PATTERNS)):
        b_c = bias_ref[p, :, 0:n_past]
        b_n = bias_ref[p, :, n_past:n_past + NEW_PAD]
        sc, sn = s_c + b_c, s_n + b_n
        m = jnp.maximum(jnp.max(sc, axis=-1, keepdims=True), jnp.max(sn, axis=-1, keepdims=True))
        ec, en = jnp.exp(sc - m), jnp.exp(sn - m)
        den = jnp.sum(ec, axis=-1, keepdims=True) + jnp.sum(en, axis=-1, keepdims=True)
        es_c.append(ec)
        es_n.append(en)
        dens.append(den)
        lses.append(m + jnp.log(den))
    mx = jnp.maximum(jnp.maximum(lses[0], lses[1]), lses[2])
    ws = [jnp.exp(l - mx) for l in lses]
    wsum = ws[0] + ws[1] + ws[2]
    coef = [w / (wsum * d) for w, d in zip(ws, dens)]
    c_c = coef[0] * es_c[0] + coef[1] * es_c[1] + coef[2] * es_c[2]
    c_n = coef[0] * es_n[0] + coef[1] * es_n[1] + coef[2] * es_n[2]
    o = (lax.dot_general(c_c.astype(BF16), cv_ref[...].astype(BF16), nt, preferred_element_type=F32)
         + jnp.dot(c_n.astype(BF16), vn, preferred_element_type=F32))
    o = jnp.where(own, o, 0.0)
    acc = o[0:s_len]
    for hh in range(1, QUAD):
        acc = acc + o[hh * s_len:(hh + 1) * s_len]
    o_ref[...] = acc.astype(o_ref.dtype)


def _attn_sample(q, kn, vn, ck, cv):
    b, s_len, d = q.shape
    n_past = ck.shape[2]
    width = QUAD * ATT_DH
    nquad = d // width
    bias = _sample_bias(s_len, n_past)
    new = pl.BlockSpec((None, s_len, width), lambda i, h: (i, 0, h))
    old = pl.BlockSpec((None, width, n_past), lambda i, h: (i, h, 0))
    return pl.pallas_call(
        functools.partial(_attn_sample_kernel, s_len=s_len),
        grid=(b, nquad),
        in_specs=[new, new, new, old, old,
                  pl.BlockSpec((len(PATTERNS), None, QUAD * s_len, n_past + NEW_PAD), lambda i, h: (0, h, 0, 0))],
        out_specs=new,
        out_shape=jax.ShapeDtypeStruct((b, s_len, d), BF16),
        compiler_params=_cparams(("arbitrary",) * 2),
        name="attn_sample",
    )(q, kn, vn, ck, cv, bias)


def kernel(x_prompt, x_sample, state_ret, state_lru, state_conv, cache_k, cache_v, norm_mix, norm_ffn, w_in_rec, w_out_rec, ret_norm_g, conv_w, conv_b, w_rgate, b_rgate, w_igate, b_igate, lru_lambda, w_in_att, w_out_att, q_norm_g, k_norm_g, w_ffn_gate, w_ffn_up, w_ffn_down):
    bp, tp, d = x_prompt.shape
    bs, ts, _ = x_sample.shape
    depth = norm_mix.shape[0]
    assert tp % SUPER == 0 and d == D_MODEL
    yp, ys = x_prompt, x_sample
    ret_p, lru_p, conv_p, kp, vp = [], [], [], [], []
    ret_s, lru_s, conv_s, ksl, vsl = [], [], [], [], []
    for l in range(depth):
        i = l // 2
        if l % 2 == 0:
            ffn = _ffn_weights(w_out_rec[i], norm_ffn[l], w_ffn_gate[l], w_ffn_up[l], w_ffn_down[l])
            w_in = w_in_rec[i].astype(BF16)
            rec_prm = (ret_norm_g[i], conv_w[i], conv_b[i], w_rgate[i], b_rgate[i], w_igate[i], b_igate[i],
                       lru_lambda[i])
            proj = _norm_proj(yp.reshape(bp * tp, d), norm_mix[l], w_in).reshape(bp, tp, REC_IN)
            mix, sr, sl, sc = _rec_mix(proj, jnp.zeros((bp, RET_HEADS, RET_DK, RET_DV), F32),
                                       jnp.zeros((bp, LRU_WIDTH), F32),
                                       jnp.zeros((bp, CONV_WIDTH - 1, LRU_WIDTH), F32), *rec_prm)
            yp = _out_ffn(mix.reshape(bp * tp, d), yp.reshape(bp * tp, d), ffn).reshape(bp, tp, d)
            ret_p.append(sr); lru_p.append(sl); conv_p.append(sc)
            proj = _norm_proj(ys.reshape(bs * ts, d), norm_mix[l], w_in).reshape(bs, ts, REC_IN)
            mix, sr, sl, sc = _rec_mix(proj, state_ret[i], state_lru[i], state_conv[i], *rec_prm)
            ys = _out_ffn(mix.reshape(bs * ts, d), ys.reshape(bs * ts, d), ffn).reshape(bs, ts, d)
            ret_s.append(sr); lru_s.append(sl); conv_s.append(sc)
        else:
            ffn = _ffn_weights(w_out_att[i], norm_ffn[l], w_ffn_gate[l], w_ffn_up[l], w_ffn_down[l])
            w_in = w_in_att[i].astype(BF16)
            q, k, v = _qkv_perm(yp, norm_mix[l], w_in, q_norm_g[i], k_norm_g[i])
            o = _attn_prompt(q, k, v)
            yp = _out_ffn(o.reshape(bp * tp, d), yp.reshape(bp * tp, d), ffn).reshape(bp, tp, d)
            from_slabs = lambda c: jnp.transpose(c.reshape(bp, ATT_HEADS, ATT_DH, SUPER), (0, 3, 1, 2))
            kc, vc = _window_cache(k, v)
            kp.append(from_slabs(kc)); vp.append(from_slabs(vc))
            qs, kn, vn = _qkv_flat(ys.reshape(bs * ts, d), norm_mix[l], w_in, q_norm_g[i], k_norm_g[i])
            n_past = cache_k.shape[2]
            to_slabs = lambda c: jnp.transpose(c, (0, 2, 3, 1)).reshape(bs, d, n_past)
            o = _attn_sample(qs.reshape(bs, ts, d), kn.reshape(bs, ts, d), vn.reshape(bs, ts, d),
                             to_slabs(cache_k[i]), to_slabs(cache_v[i]))
            ys = _out_ffn(o.reshape(bs * ts, d), ys.reshape(bs * ts, d), ffn).reshape(bs, ts, d)
            ksl.append(kn.reshape(bs, ts, ATT_HEADS, ATT_DH)); vsl.append(vn.reshape(bs, ts, ATT_HEADS, ATT_DH))
    return (yp, ys,
            jnp.stack(ret_p), jnp.stack(lru_p), jnp.stack(conv_p), jnp.stack(kp), jnp.stack(vp),
            jnp.stack(ret_s), jnp.stack(lru_s), jnp.stack(conv_s), jnp.stack(ksl), jnp.stack(vsl))
```

```python
import functools

import jax
import jax.numpy as jnp
from jax import lax
from jax.experimental import pallas as pl
from jax.experimental.pallas import tpu as pltpu

F32 = jnp.float32
BF16 = jnp.bfloat16

EPS = 1e-6
D_MODEL = 1024
RET_HEADS = 4
RET_DK = 64
RET_DV = 128
RET_CHUNK = 128
LRU_WIDTH = 512
LRU_BLOCKS = 4
LRU_BLOCK = LRU_WIDTH // LRU_BLOCKS
CONV_WIDTH = 4
LRU_C = 8.0
REC_IN = 2 * RET_HEADS * RET_DK + 2 * RET_HEADS * RET_DV + 2 * LRU_WIDTH
ATT_HEADS = 16
ATT_DH = 64
PATTERNS = ((128, 1), (512, 4), (2048, 16))
ATT_BLOCK = 128
SUPER = 2048
MAX_DIL = 16
NEG = -1e30

LANES = 128
SUBLANES = 8
NPAIR = D_MODEL // LANES
VMEM_LIMIT = 56 * 1024 * 1024

FF_CHUNK = 256
TOKEN_TILE = 512
REC_SEQS = 4


def _cparams(sem):
    return pltpu.CompilerParams(dimension_semantics=sem, vmem_limit_bytes=VMEM_LIMIT)


def _const_spec(shape):
    nd = len(shape)
    return pl.BlockSpec(shape, lambda *_: (0,) * nd, pipeline_mode=pl.Buffered(1))


def _rmsnorm(x, g):
    return x * lax.rsqrt(jnp.mean(x * x, axis=-1, keepdims=True) + EPS) * g


def _norm_proj_kernel(x_ref, g_ref, w_ref, o_ref):
    h = _rmsnorm(x_ref[...], g_ref[...]).astype(BF16)
    o_ref[...] = jnp.dot(h, w_ref[...], preferred_element_type=F32)


def _norm_proj(x2d, g, w_bf):
    m, d = x2d.shape
    n = w_bf.shape[1]
    tm = min(TOKEN_TILE, m)
    return pl.pallas_call(
        _norm_proj_kernel,
        grid=(m // tm,),
        in_specs=[pl.BlockSpec((tm, d), lambda i: (i, 0)), _const_spec((1, d)), _const_spec((d, n))],
        out_specs=pl.BlockSpec((tm, n), lambda i: (i, 0)),
        out_shape=jax.ShapeDtypeStruct((m, n), F32),
        compiler_params=_cparams(("arbitrary",)),
        name="norm_proj",
    )(x2d, g.reshape(1, d), w_bf)


def _linear_scan(a, b, h0):
    rows, w = a.shape
    a = a.reshape(rows // SUBLANES, SUBLANES, w)
    b = b.reshape(rows // SUBLANES, SUBLANES, w)
    sub = lax.broadcasted_iota(jnp.int32, a.shape, 1)
    s = 1
    while s < SUBLANES:
        keep = sub >= s
        b = jnp.where(keep, a * pltpu.roll(b, s, axis=1), 0.0) + b
        a = jnp.where(keep, a * pltpu.roll(a, s, axis=1), a)
        s *= 2
    groups = []
    carry = h0
    for i in range(rows // SUBLANES):
        groups.append(b[i] + a[i] * carry)
        carry = groups[-1][SUBLANES - 1:SUBLANES]
    return jnp.concatenate(groups, axis=0)


def _pad_rows(x, rows):
    if x.shape[0] == rows:
        return x
    return jnp.concatenate([x, jnp.zeros((rows - x.shape[0], x.shape[1]), x.dtype)], axis=0)


def _rec_mix_kernel(proj_ref, s0_ref, h0_ref, c0_ref, dec_ref, qdec_ref, kdec_ref, cdec_ref,
                    retg_ref, convw_ref, convb_ref, wrg_ref, brg_ref, wig_ref, big_ref, lam_ref,
                    mix_ref, sret_ref, hlast_ref, cnew_ref,
                    s_scr, h_scr, xe_scr, *, rows):
    cp = RET_CHUNK
    hk = RET_HEADS * RET_DK
    v_off, g_off = 2 * hk, 2 * hk + RET_HEADS * RET_DV
    xl_off = g_off + RET_HEADS * RET_DV
    yl_off = xl_off + LRU_WIDTH

    @pl.when(pl.program_id(1) == 0)
    def _():
        s_scr[...] = s0_ref[...]
        h_scr[...] = h0_ref[...]
        xe_scr[...] = jnp.zeros(xe_scr.shape, F32)
        xe_scr[:, SUBLANES - (CONV_WIDTH - 1):SUBLANES, :] = c0_ref[...]

    lane = lax.broadcasted_iota(jnp.int32, (cp, LANES), 1)
    first = lane < RET_DK
    srow_first = lax.broadcasted_iota(jnp.int32, (2 * RET_DK, RET_DV), 0) < RET_DK
    nl = -lam_ref[...]
    softplus = jnp.maximum(nl, 0.0) + jnp.log1p(jnp.exp(-jnp.abs(nl)))

    for bi in range(proj_ref.shape[0]):
        for p in range(RET_HEADS // 2):
            q2 = _pad_rows(proj_ref[bi, :, p * LANES:(p + 1) * LANES], cp)
            k2 = _pad_rows(proj_ref[bi, :, hk + p * LANES:hk + (p + 1) * LANES], cp) * (RET_DK ** -0.5)
            kb = k2.astype(BF16)
            kdt = jnp.transpose(k2 * kdec_ref[p]).astype(BF16)
            s2 = s_scr[bi, p]
            s2b = s2.astype(BF16)
            new_s = []
            for hh in range(2):
                h = 2 * p + hh
                sel = first if hh == 0 else jnp.logical_not(first)
                qh = jnp.where(sel, q2, 0.0)
                v = _pad_rows(proj_ref[bi, :, v_off + h * RET_DV:v_off + (h + 1) * RET_DV], cp)
                vb = v.astype(BF16)
                att = lax.dot_general(qh.astype(BF16), kb, (((1,), (1,)), ((), ())),
                                      preferred_element_type=F32) * dec_ref[h]
                o = (jnp.dot(att.astype(BF16), vb, preferred_element_type=F32)
                     + jnp.dot((qh * qdec_ref[p]).astype(BF16), s2b, preferred_element_type=F32))
                new_s.append(jnp.dot(kdt, vb, preferred_element_type=F32))
                o = o[:rows]
                o = o * lax.rsqrt(jnp.mean(o * o, axis=-1, keepdims=True) + EPS) * retg_ref[h:h + 1, :]
                gate = proj_ref[bi, :, g_off + h * RET_DV:g_off + (h + 1) * RET_DV]
                mix_ref[bi, :, h * RET_DV:(h + 1) * RET_DV] = (
                    o * (gate * jax.nn.sigmoid(gate))).astype(mix_ref.dtype)
            s_scr[bi, p] = s2 * cdec_ref[p] + jnp.where(srow_first, new_s[0], new_s[1])

        x = proj_ref[bi, :, xl_off:xl_off + LRU_WIDTH]
        xe = jnp.concatenate([xe_scr[bi], x], axis=0)
        xc = convb_ref[...]
        for j in range(CONV_WIDTH):
            back = CONV_WIDTH - 1 - j
            xj = x if back == 0 else pltpu.roll(xe, back, axis=0)[SUBLANES:]
            xc = xc + xj * convw_ref[j:j + 1, :]
        xe_scr[bi] = xe[rows:]
        cnew_ref[bi] = xe_scr[bi, pl.ds(SUBLANES - (CONV_WIDTH - 1), CONV_WIDTH - 1), :]

        xcb = _pad_rows(xc, max(rows, 2 * SUBLANES)).astype(BF16)
        r_parts, i_parts = [], []
        for n in range(LRU_BLOCKS):
            xb = xcb[:, n * LRU_BLOCK:(n + 1) * LRU_BLOCK]
            r_parts.append(jnp.dot(xb, wrg_ref[n], preferred_element_type=F32)[:rows])
            i_parts.append(jnp.dot(xb, wig_ref[n], preferred_element_type=F32)[:rows])
        r = jax.nn.sigmoid(jnp.concatenate(r_parts, axis=1) + brg_ref[...])
        ig = jax.nn.sigmoid(jnp.concatenate(i_parts, axis=1) + big_ref[...])
        a = jnp.exp((-LRU_C) * r * softplus)
        one_m = 1.0 - a * a
        root = jnp.where(one_m > 0.0, one_m * lax.rsqrt(one_m), 0.0)
        hseq = _linear_scan(a, root * (ig * xc), h_scr[bi])
        h_scr[bi] = hseq[rows - 1:rows, :]
        hlast_ref[bi] = hseq[rows - 1:rows, :]
        yl = proj_ref[bi, :, yl_off:yl_off + LRU_WIDTH]
        gelu = 0.5 * yl * (1.0 + jnp.tanh(0.7978845608028654 * (yl + 0.044715 * (yl * yl * yl))))
        mix_ref[bi, :, RET_HEADS * RET_DV:] = (hseq * gelu).astype(mix_ref.dtype)
    sret_ref[...] = s_scr[...]


def _ret_tables(t_eff):
    cp = RET_CHUNK
    log_g = jnp.log1p(-jnp.power(2.0, -5.0 - jnp.arange(RET_HEADS, dtype=F32)))
    idx = jnp.arange(cp, dtype=F32)
    live = idx < t_eff
    diff = idx[:, None] - idx[None, :]
    ok = (diff >= 0) & live[:, None] & live[None, :]
    decay = jnp.where(ok, jnp.exp(log_g[:, None, None] * jnp.maximum(diff, 0.0)), 0.0)
    q_dec = jnp.where(live, jnp.exp(log_g[:, None] * (idx + 1.0)), 0.0)
    k_dec = jnp.where(live, jnp.exp(log_g[:, None] * (t_eff - 1.0 - idx)), 0.0)
    c_dec = jnp.exp(log_g * t_eff)

    def pair_lanes(t):
        t = t.reshape(RET_HEADS // 2, 2, cp)
        return jnp.repeat(jnp.transpose(t, (0, 2, 1)), RET_DK, axis=2)

    c_rows = jnp.repeat(c_dec.reshape(RET_HEADS // 2, 2), RET_DK, axis=1)
    c_rows = jnp.broadcast_to(c_rows[:, :, None], (RET_HEADS // 2, 2 * RET_DK, RET_DV))
    return decay, pair_lanes(q_dec), pair_lanes(k_dec), c_rows


def _rec_mix(proj, s0, h0, c0, ret_g, conv_w, conv_b, w_rg, b_rg, w_ig, b_ig, lam):
    b, t, _ = proj.shape
    rows = RET_CHUNK if t % RET_CHUNK == 0 else t
    assert t % rows == 0 and rows % SUBLANES == 0 and (t == rows or rows == RET_CHUNK)
    n = t // rows
    nb = REC_SEQS if b % REC_SEQS == 0 else 1
    dec, qdec, kdec, cdec = _ret_tables(rows)
    hp = RET_HEADS // 2
    w = LRU_WIDTH
    per_b3 = lambda i, c: (i, 0, 0)
    outs = pl.pallas_call(
        functools.partial(_rec_mix_kernel, rows=rows),
        grid=(b // nb, n),
        in_specs=[
            pl.BlockSpec((nb, rows, REC_IN), lambda i, c: (i, c, 0)),
            pl.BlockSpec((nb, hp, 2 * RET_DK, RET_DV), lambda i, c: (i, 0, 0, 0)),
            pl.BlockSpec((nb, 1, w), per_b3),
            pl.BlockSpec((nb, CONV_WIDTH - 1, w), per_b3),
            _const_spec(dec.shape), _const_spec(qdec.shape), _const_spec(kdec.shape), _const_spec(cdec.shape),
            _const_spec((RET_HEADS, RET_DV)), _const_spec((CONV_WIDTH, w)), _const_spec((1, w)),
            _const_spec((LRU_BLOCKS, LRU_BLOCK, LRU_BLOCK)), _const_spec((1, w)),
            _const_spec((LRU_BLOCKS, LRU_BLOCK, LRU_BLOCK)), _const_spec((1, w)), _const_spec((1, w)),
        ],
        out_specs=[
            pl.BlockSpec((nb, rows, D_MODEL), lambda i, c: (i, c, 0)),
            pl.BlockSpec((nb, hp, 2 * RET_DK, RET_DV), lambda i, c: (i, 0, 0, 0)),
            pl.BlockSpec((nb, 1, w), per_b3),
            pl.BlockSpec((nb, CONV_WIDTH - 1, w), per_b3),
        ],
        out_shape=[
            jax.ShapeDtypeStruct((b, t, D_MODEL), BF16),
            jax.ShapeDtypeStruct((b, hp, 2 * RET_DK, RET_DV), F32),
            jax.ShapeDtypeStruct((b, 1, w), F32),
            jax.ShapeDtypeStruct((b, CONV_WIDTH - 1, w), F32),
        ],
        scratch_shapes=[
            pltpu.VMEM((nb, hp, 2 * RET_DK, RET_DV), F32),
            pltpu.VMEM((nb, 1, w), F32),
            pltpu.VMEM((nb, SUBLANES, w), F32),
        ],
        compiler_params=_cparams(("arbitrary", "arbitrary")),
        name="rec_mix",
    )(proj, s0.reshape(b, hp, 2 * RET_DK, RET_DV), h0.reshape(b, 1, w), c0,
      dec, qdec, kdec, cdec, ret_g, conv_w, conv_b.reshape(1, w),
      w_rg.astype(BF16), b_rg.reshape(1, w), w_ig.astype(BF16), b_ig.reshape(1, w), lam.reshape(1, w))
    mix, s_ret, h_last, c_new = outs
    return mix, s_ret.reshape(b, RET_HEADS, RET_DK, RET_DV), h_last.reshape(b, w), c_new


def _out_ffn_kernel(mix_ref, x_ref, wo_ref, g_ref, wg_ref, wu_ref, wd_ref, o_ref, act_ref):
    y = x_ref[...] + jnp.dot(mix_ref[...].astype(BF16), wo_ref[...], preferred_element_type=F32)
    h = _rmsnorm(y, g_ref[...]).astype(BF16)
    for c in range(wg_ref.shape[1] // FF_CHUNK):
        cols = slice(c * FF_CHUNK, (c + 1) * FF_CHUNK)
        gate = jnp.dot(h, wg_ref[:, cols], preferred_element_type=F32)
        up = jnp.dot(h, wu_ref[:, cols], preferred_element_type=F32)
        act_ref[:, cols] = (gate * jax.nn.sigmoid(gate) * up).astype(BF16)
    o_ref[...] = y + jnp.dot(act_ref[...], wd_ref[...], preferred_element_type=F32)


def _ffn_weights(w_out, g, w_gate, w_up, w_down):
    return (w_out.astype(BF16), g.reshape(1, g.shape[0]), w_gate.astype(BF16), w_up.astype(BF16),
            w_down.astype(BF16))


def _out_ffn(mix2d, x2d, wts):
    m, d = x2d.shape
    tm = min(TOKEN_TILE, m)
    row = lambda i: (i, 0)
    return pl.pallas_call(
        _out_ffn_kernel,
        grid=(m // tm,),
        in_specs=[pl.BlockSpec((tm, d), row), pl.BlockSpec((tm, d), row)] + [_const_spec(w.shape) for w in wts],
        out_specs=pl.BlockSpec((tm, d), row),
        out_shape=jax.ShapeDtypeStruct((m, d), F32),
        scratch_shapes=[pltpu.VMEM((tm, wts[2].shape[1]), BF16)],
        compiler_params=_cparams(("arbitrary",)),
        name="out_ffn",
    )(mix2d, x2d, *wts)


def _pair_rms(a, g2, first):
    sq = a * a
    s_a = jnp.sum(jnp.where(first, sq, 0.0), axis=-1, keepdims=True)
    s_b = jnp.sum(jnp.where(first, 0.0, sq), axis=-1, keepdims=True)
    ms = jnp.where(first, s_a, s_b) * (1.0 / ATT_DH)
    return a * lax.rsqrt(ms + EPS) * g2


def _qkv_slabs(x, g_ref, w_ref, qg_ref, kg_ref):
    h = _rmsnorm(x, g_ref[...]).astype(BF16)
    qkv = jnp.dot(h, w_ref[...], preferred_element_type=F32)
    first = lax.broadcasted_iota(jnp.int32, (x.shape[0], LANES), 1) < ATT_DH
    slab = lambda base, p: qkv[:, base + p * LANES:base + (p + 1) * LANES]
    q = [_pair_rms(slab(0, p), qg_ref[...], first) for p in range(NPAIR)]
    k = [_pair_rms(slab(D_MODEL, p), kg_ref[...], first) for p in range(NPAIR)]
    v = [slab(2 * D_MODEL, p) for p in range(NPAIR)]
    return q, k, v


def _qkv_flat_kernel(x_ref, g_ref, w_ref, qg_ref, kg_ref, q_ref, k_ref, v_ref):
    q, k, v = _qkv_slabs(x_ref[...], g_ref, w_ref, qg_ref, kg_ref)
    q_ref[...] = jnp.concatenate(q, axis=1)
    k_ref[...] = jnp.concatenate(k, axis=1)
    v_ref[...] = jnp.concatenate(v, axis=1)


def _qkv_perm_kernel(x_ref, g_ref, w_ref, qg_ref, kg_ref, q_ref, k_ref, v_ref, xs_ref):
    n = x_ref.shape[0] // MAX_DIL
    for p in range(NPAIR):
        xs_ref[p] = x_ref[:, p * LANES:(p + 1) * LANES]
    x = jnp.concatenate(
        [jnp.concatenate([xs_ref[p, pl.ds(r, n, stride=MAX_DIL), :] for p in range(NPAIR)], axis=1)
         for r in range(MAX_DIL)], axis=0)
    q, k, v = _qkv_slabs(x, g_ref, w_ref, qg_ref, kg_ref)
    for ref, slabs in ((q_ref, q), (k_ref, k), (v_ref, v)):
        for p in range(NPAIR):
            for r in range(MAX_DIL):
                ref[p, r] = slabs[p][r * n:(r + 1) * n]


def _head_gain(g):
    return jnp.tile(g.reshape(1, ATT_DH), (1, LANES // ATT_DH))


def _qkv_flat(x2d, g, w_bf, qg, kg):
    m, d = x2d.shape
    tm = min(TOKEN_TILE, m)
    row = lambda i: (i, 0)
    return pl.pallas_call(
        _qkv_flat_kernel,
        grid=(m // tm,),
        in_specs=[pl.BlockSpec((tm, d), row), _const_spec((1, d)), _const_spec(w_bf.shape),
                  _const_spec((1, LANES)), _const_spec((1, LANES))],
        out_specs=[pl.BlockSpec((tm, d), row)] * 3,
        out_shape=[jax.ShapeDtypeStruct((m, d), F32)] * 3,
        compiler_params=_cparams(("arbitrary",)),
        name="qkv_flat",
    )(x2d, g.reshape(1, d), w_bf, _head_gain(qg), _head_gain(kg))


def _qkv_perm(x, g, w_bf, qg, kg):
    b, t, d = x.shape
    ns = t // SUPER
    tm = TOKEN_TILE
    nt = SUPER // tm
    n = tm // MAX_DIL
    out_spec = pl.BlockSpec((None, NPAIR, None, MAX_DIL, None, n, LANES), lambda i, s, j: (i, 0, s, 0, j, 0, 0))
    out_shape = jax.ShapeDtypeStruct((b, NPAIR, ns, MAX_DIL, nt, n, LANES), F32)
    q, k, v = pl.pallas_call(
        _qkv_perm_kernel,
        grid=(b, ns, nt),
        in_specs=[pl.BlockSpec((None, tm, d), lambda i, s, j: (i, s * nt + j, 0)),
                  _const_spec((1, d)), _const_spec(w_bf.shape), _const_spec((1, LANES)), _const_spec((1, LANES))],
        out_specs=[out_spec] * 3,
        out_shape=[out_shape] * 3,
        scratch_shapes=[pltpu.VMEM((NPAIR, tm, LANES), F32)],
        compiler_params=_cparams(("arbitrary",) * 3),
        name="qkv_perm",
    )(x, g.reshape(1, d), w_bf, _head_gain(qg), _head_gain(kg))
    return tuple(a.reshape(b, NPAIR, t, LANES) for a in (q, k, v))


def _window_cache_kernel(k_ref, v_ref, ko_ref, vo_ref, nat_ref):
    for src, dst in ((k_ref, ko_ref), (v_ref, vo_ref)):
        for r in range(MAX_DIL):
            nat_ref[pl.ds(r, ATT_BLOCK, stride=MAX_DIL), :] = src[r * ATT_BLOCK:(r + 1) * ATT_BLOCK, :]
        dst[...] = jnp.transpose(nat_ref[...])


def _window_cache(k, v):
    b, npair, t, _ = k.shape
    last = t // SUPER - 1
    src = pl.BlockSpec((None, None, SUPER, LANES), lambda i, p: (i, p, last, 0))
    dst = pl.BlockSpec((None, LANES, SUPER), lambda i, p: (i, p, 0))
    shape = jax.ShapeDtypeStruct((b, npair * LANES, SUPER), F32)
    return pl.pallas_call(
        _window_cache_kernel,
        grid=(b, npair),
        in_specs=[src, src],
        out_specs=[dst, dst],
        out_shape=[shape, shape],
        scratch_shapes=[pltpu.VMEM((SUPER, LANES), F32)],
        compiler_params=_cparams(("arbitrary",) * 2),
        name="window_cache",
    )(k, v)


def _alibi_slopes():
    return jnp.power(2.0, -8.0 * jnp.arange(1, ATT_HEADS + 1, dtype=F32) / ATT_HEADS)


PREV_SLOTS = 32
ATT_UNROLL = 16
LOG2E = 1.4426950408889634


def _band_bias():
    i = jnp.arange(ATT_BLOCK)
    c = jnp.arange(2 * ATT_BLOCK)
    q_idx = {1: MAX_DIL * (i % 8) + i // 8,
             4: 4 * (i % 32) + i // 32,
             16: i}
    k_idx = {1: (MAX_DIL * (c % 8) + c // 16, (c // 8) % 2 == 0),
             4: (4 * (c % 32) + c // 64, (c // 32) % 2 == 0),
             16: (c % ATT_BLOCK, c < ATT_BLOCK)}
    slopes = _alibi_slopes().reshape(ATT_HEADS // 2, 2, 1, 1)
    out = []
    for (w, d) in PATTERNS:
        span = w // d
        km, is_prev = k_idx[d]
        diff = q_idx[d][:, None] - (km - jnp.where(is_prev, ATT_BLOCK, 0))[None, :]
        valid = (diff >= 0) & (diff <= span)
        bias = -slopes * (d * diff).astype(F32)[None, None]
        bias = bias * LOG2E
        variants = [jnp.where(valid, bias, NEG), jnp.where(valid & ~is_prev[None, :], bias, NEG)]
        out.append(jnp.stack([v.reshape(ATT_HEADS // 2, 2 * ATT_BLOCK, 2 * ATT_BLOCK) for v in variants]))
    return jnp.stack(out)


def _attend_pair(q, k, v, bias, first):
    qs = jnp.concatenate([jnp.where(first, q, 0.0), jnp.where(first, 0.0, q)], axis=0).astype(BF16)
    s = lax.dot_general(qs, k.astype(BF16), (((1,), (1,)), ((), ())), preferred_element_type=F32) + bias
    m = jnp.max(s, axis=-1, keepdims=True)
    e = jnp.exp2(s - m)
    den = jnp.sum(e, axis=-1, keepdims=True)
    o = jnp.dot(e.astype(BF16), v.astype(BF16), preferred_element_type=F32)
    halves = lambda a: jnp.where(first, a[:ATT_BLOCK], a[ATT_BLOCK:])
    return halves(o), halves(m), halves(den)


def _attn_prompt_kernel(q_ref, kc_ref, kp_ref, vc_ref, vp_ref, bias_ref, o_ref, kext, vext, oacc, macc, dacc):
    first_sb = (pl.program_id(1) == 0).astype(jnp.int32)
    first = lax.broadcasted_iota(jnp.int32, (ATT_BLOCK, LANES), 1) < ATT_DH
    scale = (ATT_DH ** -0.5) * LOG2E
    ps = PREV_SLOTS

    for src_p, src_c, ext in ((kp_ref, kc_ref, kext), (vp_ref, vc_ref, vext)):
        for r in range(MAX_DIL):
            ext[r, 0:ps, :] = src_p[r * ATT_BLOCK + ATT_BLOCK - ps:(r + 1) * ATT_BLOCK, :]
            ext[r, ps:ps + ATT_BLOCK, :] = src_c[r * ATT_BLOCK:(r + 1) * ATT_BLOCK, :]

    def attend_store(p, variant, q_chunks, k, v):
        n = ATT_BLOCK // len(q_chunks)
        q = jnp.concatenate([q_ref[sl, :] for sl in q_chunks], axis=0) * scale
        o, m, den = _attend_pair(q, k, v, bias_ref[p, variant], first)
        for ci, sl in enumerate(q_chunks):
            oacc[p, sl, :] = o[ci * n:(ci + 1) * n]
            macc[p, sl, :] = m[ci * n:(ci + 1) * n]
            dacc[p, sl, :] = den[ci * n:(ci + 1) * n]

    n0 = ATT_BLOCK // MAX_DIL

    def block0(bi):
        base = pl.multiple_of(bi * n0, n0)
        ksl = pl.ds(base + ps - n0, 2 * n0)
        k = jnp.concatenate([kext[r, ksl, :] for r in range(MAX_DIL)], axis=0)
        v = jnp.concatenate([vext[r, ksl, :] for r in range(MAX_DIL)], axis=0)
        attend_store(0, first_sb * jnp.asarray(bi == 0, jnp.int32),
                     [pl.ds(r * ATT_BLOCK + base, n0) for r in range(MAX_DIL)], k, v)

    d1 = PATTERNS[1][1]
    nc1 = MAX_DIL // d1
    n1 = ATT_BLOCK // nc1

    def block1(idx):
        r4, mb = idx // (ATT_BLOCK // n1), idx % (ATT_BLOCK // n1)
        base = pl.multiple_of(mb * n1, n1)
        ksl = pl.ds(base + ps - n1, 2 * n1)
        k = jnp.concatenate([kext[r4 + d1 * c, ksl, :] for c in range(nc1)], axis=0)
        v = jnp.concatenate([vext[r4 + d1 * c, ksl, :] for c in range(nc1)], axis=0)
        attend_store(1, first_sb * jnp.asarray(mb == 0, jnp.int32),
                     [pl.ds(pl.multiple_of((r4 + d1 * c) * ATT_BLOCK + base, n1), n1) for c in range(nc1)], k, v)

    def block2(r):
        sl = pl.ds(pl.multiple_of(r * ATT_BLOCK, ATT_BLOCK), ATT_BLOCK)
        k = jnp.concatenate([kp_ref[sl, :], kc_ref[sl, :]], axis=0)
        v = jnp.concatenate([vp_ref[sl, :], vc_ref[sl, :]], axis=0)
        attend_store(2, first_sb, [sl], k, v)

    for block in (block0, block1, block2):
        def body(it, carry, block=block):
            for u in range(ATT_UNROLL):
                block(it * ATT_UNROLL + u)
            return carry

        lax.fori_loop(0, MAX_DIL // ATT_UNROLL, body, 0)

    def comb(r, carry):
        sl = pl.ds(pl.multiple_of(r * ATT_BLOCK, ATT_BLOCK), ATT_BLOCK)
        m0, m1, m2 = macc[0, sl, :], macc[1, sl, :], macc[2, sl, :]
        mx = jnp.maximum(jnp.maximum(m0, m1), m2)
        w0, w1, w2 = jnp.exp2(m0 - mx), jnp.exp2(m1 - mx), jnp.exp2(m2 - mx)
        num = w0 * oacc[0, sl, :] + w1 * oacc[1, sl, :] + w2 * oacc[2, sl, :]
        den = w0 * dacc[0, sl, :] + w1 * dacc[1, sl, :] + w2 * dacc[2, sl, :]
        o_ref[pl.ds(r, ATT_BLOCK, stride=MAX_DIL), :] = num / den
        return carry

    lax.fori_loop(0, SUPER // ATT_BLOCK, comb, 0)


def _attn_prompt(q, k, v):
    b, npair, t, _ = q.shape
    ns = t // SUPER
    bias = _band_bias()
    cur = pl.BlockSpec((None, None, SUPER, LANES), lambda i, s, p: (i, p, s, 0))
    prev = pl.BlockSpec((None, None, SUPER, LANES), lambda i, s, p: (i, p, jnp.maximum(s - 1, 0), 0))
    return pl.pallas_call(
        _attn_prompt_kernel,
        grid=(b, ns, npair),
        in_specs=[cur, cur, prev, cur, prev,
                  pl.BlockSpec((len(PATTERNS), 2, None, 2 * ATT_BLOCK, 2 * ATT_BLOCK),
                               lambda i, s, p: (0, 0, p, 0, 0))],
        out_specs=pl.BlockSpec((None, SUPER, LANES), lambda i, s, p: (i, s, p)),
        out_shape=jax.ShapeDtypeStruct((b, t, npair * LANES), F32),
        scratch_shapes=[pltpu.VMEM((MAX_DIL, PREV_SLOTS + ATT_BLOCK, LANES), F32),
                        pltpu.VMEM((MAX_DIL, PREV_SLOTS + ATT_BLOCK, LANES), F32),
                        pltpu.VMEM((len(PATTERNS), SUPER, LANES), F32),
                        pltpu.VMEM((len(PATTERNS), SUPER, LANES), F32),
                        pltpu.VMEM((len(PATTERNS), SUPER, LANES), F32)],
        compiler_params=_cparams(("arbitrary",) * 3),
        name="attn_prompt",
    )(q, k, k, v, v, bias)


QUAD = 4
NEW_PAD = 128


def _sample_bias(s_len, n_past):
    s = jnp.arange(s_len)
    col = jnp.arange(n_past + NEW_PAD)
    dist = (n_past + s)[:, None] - col[None, :]
    real = (col < n_past + s_len)[None, :]
    slopes = _alibi_slopes().reshape(ATT_HEADS // QUAD, QUAD, 1, 1)
    out = []
    for (w, d) in PATTERNS:
        valid = (dist >= 0) & (dist % d == 0) & (dist <= w) & real
        bias = jnp.where(valid[None, None], -slopes * dist.astype(F32)[None, None], NEG)
        out.append(bias.reshape(ATT_HEADS // QUAD, QUAD * s_len, n_past + NEW_PAD))
    return jnp.stack(out)


def _attn_sample_kernel(q_ref, kn_ref, vn_ref, ck_ref, cv_ref, bias_ref, o_ref, *, s_len):
    n_past = ck_ref.shape[1]
    quad = pl.program_id(1)
    width = QUAD * ATT_DH
    rows = QUAD * s_len
    q = q_ref[...] * (ATT_DH ** -0.5)
    q4 = jnp.concatenate([q] * QUAD, axis=0)
    own = (lax.broadcasted_iota(jnp.int32, (rows, width), 0) // s_len
           == lax.broadcasted_iota(jnp.int32, (rows, width), 1) // ATT_DH)
    qm = jnp.where(own, q4, 0.0).astype(BF16)
    nt = (((1,), (1,)), ((), ()))
    kn = _pad_rows(kn_ref[...], NEW_PAD).astype(BF16)
    vn = _pad_rows(vn_ref[...], NEW_PAD).astype(BF16)
    s_c = jnp.dot(qm, ck_ref[...].astype(BF16), preferred_element_type=F32)
    s_n = lax.dot_general(qm, kn, nt, preferred_element_type=F32)
    es_c, es_n, dens, lses = [], [], [], []
    for p in range(len(PATTERNS)):
        b_c = bias_ref[p, quad, :, 0:n_past]
        b_n = bias_ref[p, quad, :, n_past:n_past + NEW_PAD]
        sc, sn = s_c + b_c, s_n + b_n
        m = jnp.maximum(jnp.max(sc, axis=-1, keepdims=True), jnp.max(sn, axis=-1, keepdims=True))
        ec, en = jnp.exp(sc - m), jnp.exp(sn - m)
        den = jnp.sum(ec, axis=-1, keepdims=True) + jnp.sum(en, axis=-1, keepdims=True)
        es_c.append(ec)
        es_n.append(en)
        dens.append(den)
        lses.append(m + jnp.log(den))
    mx = jnp.maximum(jnp.maximum(lses[0], lses[1]), lses[2])
    ws = [jnp.exp(l - mx) for l in lses]
    wsum = ws[0] + ws[1] + ws[2]
    coef = [w / (wsum * d) for w, d in zip(ws, dens)]
    c_c = coef[0] * es_c[0] + coef[1] * es_c[1] + coef[2] * es_c[2]
    c_n = coef[0] * es_n[0] + coef[1] * es_n[1] + coef[2] * es_n[2]
    o = (lax.dot_general(c_c.astype(BF16), cv_ref[...].astype(BF16), nt, preferred_element_type=F32)
         + jnp.dot(c_n.astype(BF16), vn, preferred_element_type=F32))
    o = jnp.where(own, o, 0.0)
    acc = o[0:s_len]
    for hh in range(1, QUAD):
        acc = acc + o[hh * s_len:(hh + 1) * s_len]
    o_ref[...] = acc.astype(o_ref.dtype)


def _attn_sample(q, kn, vn, ck, cv):
    b, s_len, d = q.shape
    n_past = ck.shape[2]
    width = QUAD * ATT_DH
    nquad = d // width
    bias = _sample_bias(s_len, n_past)
    new = pl.BlockSpec((None, s_len, width), lambda i, h: (i, 0, h))
    old = pl.BlockSpec((None, width, n_past), lambda i, h: (i, h, 0))
    return pl.pallas_call(
        functools.partial(_attn_sample_kernel, s_len=s_len),
        grid=(b, nquad),
        in_specs=[new, new, new, old, old, _const_spec(bias.shape)],
        out_specs=new,
        out_shape=jax.ShapeDtypeStruct((b, s_len, d), BF16),
        compiler_params=_cparams(("arbitrary",) * 2),
        name="attn_sample",
    )(q, kn, vn, ck, cv, bias)


def kernel(x_prompt, x_sample, state_ret, state_lru, state_conv, cache_k, cache_v, norm_mix, norm_ffn, w_in_rec, w_out_rec, ret_norm_g, conv_w, conv_b, w_rgate, b_rgate, w_igate, b_igate, lru_lambda, w_in_att, w_out_att, q_norm_g, k_norm_g, w_ffn_gate, w_ffn_up, w_ffn_down):
    bp, tp, d = x_prompt.shape
    bs, ts, _ = x_sample.shape
    depth = norm_mix.shape[0]
    assert tp % SUPER == 0 and d == D_MODEL
    yp, ys = x_prompt, x_sample
    ret_p, lru_p, conv_p, kp, vp = [], [], [], [], []
    ret_s, lru_s, conv_s, ksl, vsl = [], [], [], [], []
    for l in range(depth):
        i = l // 2
        if l % 2 == 0:
            ffn = _ffn_weights(w_out_rec[i], norm_ffn[l], w_ffn_gate[l], w_ffn_up[l], w_ffn_down[l])
            w_in = w_in_rec[i].astype(BF16)
            rec_prm = (ret_norm_g[i], conv_w[i], conv_b[i], w_rgate[i], b_rgate[i], w_igate[i], b_igate[i],
                       lru_lambda[i])
            proj = _norm_proj(yp.reshape(bp * tp, d), norm_mix[l], w_in).reshape(bp, tp, REC_IN)
            mix, sr, sl, sc = _rec_mix(proj, jnp.zeros((bp, RET_HEADS, RET_DK, RET_DV), F32),
                                       jnp.zeros((bp, LRU_WIDTH), F32),
                                       jnp.zeros((bp, CONV_WIDTH - 1, LRU_WIDTH), F32), *rec_prm)
            yp = _out_ffn(mix.reshape(bp * tp, d), yp.reshape(bp * tp, d), ffn).reshape(bp, tp, d)
            ret_p.append(sr); lru_p.append(sl); conv_p.append(sc)
            proj = _norm_proj(ys.reshape(bs * ts, d), norm_mix[l], w_in).reshape(bs, ts, REC_IN)
            mix, sr, sl, sc = _rec_mix(proj, state_ret[i], state_lru[i], state_conv[i], *rec_prm)
            ys = _out_ffn(mix.reshape(bs * ts, d), ys.reshape(bs * ts, d), ffn).reshape(bs, ts, d)
            ret_s.append(sr); lru_s.append(sl); conv_s.append(sc)
        else:
            ffn = _ffn_weights(w_out_att[i], norm_ffn[l], w_ffn_gate[l], w_ffn_up[l], w_ffn_down[l])
            w_in = w_in_att[i].astype(BF16)
            q, k, v = _qkv_perm(yp, norm_mix[l], w_in, q_norm_g[i], k_norm_g[i])
            o = _attn_prompt(q, k, v)
            yp = _out_ffn(o.reshape(bp * tp, d), yp.reshape(bp * tp, d), ffn).reshape(bp, tp, d)
            from_slabs = lambda c: jnp.transpose(c.reshape(bp, ATT_HEADS, ATT_DH, SUPER), (0, 3, 1, 2))
            kc, vc = _window_cache(k, v)
            kp.append(from_slabs(kc)); vp.append(from_slabs(vc))
            qs, kn, vn = _qkv_flat(ys.reshape(bs * ts, d), norm_mix[l], w_in, q_norm_g[i], k_norm_g[i])
            n_past = cache_k.shape[2]
            to_slabs = lambda c: jnp.transpose(c, (0, 2, 3, 1)).reshape(bs, d, n_past)
            o = _attn_sample(qs.reshape(bs, ts, d), kn.reshape(bs, ts, d), vn.reshape(bs, ts, d),
                             to_slabs(cache_k[i]), to_slabs(cache_v[i]))
            ys = _out_ffn(o.reshape(bs * ts, d), ys.reshape(bs * ts, d), ffn).reshape(bs, ts, d)
            ksl.append(kn.reshape(bs, ts, ATT_HEADS, ATT_DH)); vsl.append(vn.reshape(bs, ts, ATT_HEADS, ATT_DH))
    return (yp, ys,
            jnp.stack(ret_p), jnp.stack(lru_p), jnp.stack(conv_p), jnp.stack(kp), jnp.stack(vp),
            jnp.stack(ret_s), jnp.stack(lru_s), jnp.stack(conv_s), jnp.stack(ksl), jnp.stack(vsl))
```

```python
import functools

import jax
import numpy as np
import jax.numpy as jnp
from jax import lax
from jax.experimental import pallas as pl
from jax.experimental.pallas import tpu as pltpu

F32 = jnp.float32
BF16 = jnp.bfloat16

EPS = 1e-6
D_MODEL = 1024
RET_HEADS = 4
RET_DK = 64
RET_DV = 128
RET_CHUNK = 128
LRU_WIDTH = 512
LRU_BLOCKS = 4
LRU_BLOCK = LRU_WIDTH // LRU_BLOCKS
CONV_WIDTH = 4
LRU_C = 8.0
REC_IN = 2 * RET_HEADS * RET_DK + 2 * RET_HEADS * RET_DV + 2 * LRU_WIDTH
ATT_HEADS = 16
ATT_DH = 64
PATTERNS = ((128, 1), (512, 4), (2048, 16))
ATT_BLOCK = 128
SUPER = 2048
MAX_DIL = 16
NEG = -1e30

LANES = 128
SUBLANES = 8
MXU_COLS = 256
NPAIR = D_MODEL // LANES
VMEM_LIMIT = 56 * 1024 * 1024

FF_CHUNK = 256
TOKEN_TILE = 512
REC_SEQS = 4


def _cparams(sem):
    return pltpu.CompilerParams(dimension_semantics=sem, vmem_limit_bytes=VMEM_LIMIT)


def _const_spec(shape):
    nd = len(shape)
    return pl.BlockSpec(shape, lambda *_: (0,) * nd, pipeline_mode=pl.Buffered(1))


def _rmsnorm(x, g):
    return x * lax.rsqrt(jnp.mean(x * x, axis=-1, keepdims=True) + EPS) * g


def _norm_proj_kernel(x_ref, g_ref, w_ref, o_ref):
    h = _rmsnorm(x_ref[...], g_ref[...]).astype(BF16)
    o_ref[...] = jnp.dot(h, w_ref[...], preferred_element_type=F32)


def _norm_proj(x2d, g, w_bf):
    m, d = x2d.shape
    n = w_bf.shape[1]
    tm = min(TOKEN_TILE, m)
    return pl.pallas_call(
        _norm_proj_kernel,
        grid=(m // tm,),
        in_specs=[pl.BlockSpec((tm, d), lambda i: (i, 0)), _const_spec((1, d)), _const_spec((d, n))],
        out_specs=pl.BlockSpec((tm, n), lambda i: (i, 0)),
        out_shape=jax.ShapeDtypeStruct((m, n), F32),
        compiler_params=_cparams(("arbitrary",)),
        name="norm_proj",
    )(x2d, g.reshape(1, d), w_bf)


def _linear_scan(a, b, h0):
    rows, w = a.shape
    a = a.reshape(rows // SUBLANES, SUBLANES, w)
    b = b.reshape(rows // SUBLANES, SUBLANES, w)
    sub = lax.broadcasted_iota(jnp.int32, a.shape, 1)
    s = 1
    while s < SUBLANES:
        keep = sub >= s
        b = jnp.where(keep, a * pltpu.roll(b, s, axis=1), 0.0) + b
        a = jnp.where(keep, a * pltpu.roll(a, s, axis=1), a)
        s *= 2
    groups = []
    carry = h0
    for i in range(rows // SUBLANES):
        groups.append(b[i] + a[i] * carry)
        carry = groups[-1][SUBLANES - 1:SUBLANES]
    return jnp.concatenate(groups, axis=0)


def _pad_rows(x, rows):
    if x.shape[0] == rows:
        return x
    return jnp.concatenate([x, jnp.zeros((rows - x.shape[0], x.shape[1]), x.dtype)], axis=0)


def _rec_mix_kernel(proj_ref, s0_ref, h0_ref, c0_ref, dec_ref, qdec_ref, kdec_ref, cdec_ref,
                    retg_ref, convw_ref, convb_ref, wrg_ref, brg_ref, wig_ref, big_ref, lam_ref,
                    mix_ref, sret_ref, hlast_ref, cnew_ref,
                    s_scr, h_scr, xe_scr, *, rows):
    cp = RET_CHUNK
    hk = RET_HEADS * RET_DK
    v_off, g_off = 2 * hk, 2 * hk + RET_HEADS * RET_DV
    xl_off = g_off + RET_HEADS * RET_DV
    yl_off = xl_off + LRU_WIDTH

    @pl.when(pl.program_id(1) == 0)
    def _():
        s_scr[...] = s0_ref[...]
        h_scr[...] = h0_ref[...]
        xe_scr[...] = jnp.zeros(xe_scr.shape, F32)
        xe_scr[:, SUBLANES - (CONV_WIDTH - 1):SUBLANES, :] = c0_ref[...]

    lane = lax.broadcasted_iota(jnp.int32, (cp, LANES), 1)
    first = lane < RET_DK
    srow_first = lax.broadcasted_iota(jnp.int32, (2 * RET_DK, RET_DV), 0) < RET_DK
    nl = -lam_ref[...]
    softplus = jnp.maximum(nl, 0.0) + jnp.log1p(jnp.exp(-jnp.abs(nl)))

    for bi in range(proj_ref.shape[0]):
        for p in range(RET_HEADS // 2):
            q2 = _pad_rows(proj_ref[bi, :, p * LANES:(p + 1) * LANES], cp)
            k2 = _pad_rows(proj_ref[bi, :, hk + p * LANES:hk + (p + 1) * LANES], cp) * (RET_DK ** -0.5)
            kb = k2.astype(BF16)
            kdt = jnp.transpose(k2 * kdec_ref[p]).astype(BF16)
            s2 = s_scr[bi, p]
            s2b = s2.astype(BF16)
            new_s = []
            for hh in range(2):
                h = 2 * p + hh
                sel = first if hh == 0 else jnp.logical_not(first)
                qh = jnp.where(sel, q2, 0.0)
                v = _pad_rows(proj_ref[bi, :, v_off + h * RET_DV:v_off + (h + 1) * RET_DV], cp)
                vb = v.astype(BF16)
                att = lax.dot_general(qh.astype(BF16), kb, (((1,), (1,)), ((), ())),
                                      preferred_element_type=F32) * dec_ref[h]
                o = (jnp.dot(att.astype(BF16), vb, preferred_element_type=F32)
                     + jnp.dot((qh * qdec_ref[p]).astype(BF16), s2b, preferred_element_type=F32))
                new_s.append(jnp.dot(kdt, vb, preferred_element_type=F32))
                o = o[:rows]
                o = o * lax.rsqrt(jnp.mean(o * o, axis=-1, keepdims=True) + EPS) * retg_ref[h:h + 1, :]
                gate = proj_ref[bi, :, g_off + h * RET_DV:g_off + (h + 1) * RET_DV]
                mix_ref[bi, :, h * RET_DV:(h + 1) * RET_DV] = (
                    o * (gate * jax.nn.sigmoid(gate))).astype(mix_ref.dtype)
            s_scr[bi, p] = s2 * cdec_ref[p] + jnp.where(srow_first, new_s[0], new_s[1])

        x = proj_ref[bi, :, xl_off:xl_off + LRU_WIDTH]
        xe = jnp.concatenate([xe_scr[bi], x], axis=0)
        xc = convb_ref[...]
        for j in range(CONV_WIDTH):
            back = CONV_WIDTH - 1 - j
            xj = x if back == 0 else pltpu.roll(xe, back, axis=0)[SUBLANES:]
            xc = xc + xj * convw_ref[j:j + 1, :]
        xe_scr[bi] = xe[rows:]
        cnew_ref[bi] = xe_scr[bi, pl.ds(SUBLANES - (CONV_WIDTH - 1), CONV_WIDTH - 1), :]

        xcb = _pad_rows(xc, max(rows, 2 * SUBLANES)).astype(BF16)
        r_parts, i_parts = [], []
        for n in range(LRU_BLOCKS):
            xb = xcb[:, n * LRU_BLOCK:(n + 1) * LRU_BLOCK]
            r_parts.append(jnp.dot(xb, wrg_ref[n], preferred_element_type=F32)[:rows])
            i_parts.append(jnp.dot(xb, wig_ref[n], preferred_element_type=F32)[:rows])
        r = jax.nn.sigmoid(jnp.concatenate(r_parts, axis=1) + brg_ref[...])
        ig = jax.nn.sigmoid(jnp.concatenate(i_parts, axis=1) + big_ref[...])
        a = jnp.exp((-LRU_C) * r * softplus)
        one_m = 1.0 - a * a
        root = jnp.where(one_m > 0.0, one_m * lax.rsqrt(one_m), 0.0)
        hseq = _linear_scan(a, root * (ig * xc), h_scr[bi])
        h_scr[bi] = hseq[rows - 1:rows, :]
        hlast_ref[bi] = hseq[rows - 1:rows, :]
        yl = proj_ref[bi, :, yl_off:yl_off + LRU_WIDTH]
        gelu = 0.5 * yl * (1.0 + jnp.tanh(0.7978845608028654 * (yl + 0.044715 * (yl * yl * yl))))
        mix_ref[bi, :, RET_HEADS * RET_DV:] = (hseq * gelu).astype(mix_ref.dtype)
    sret_ref[...] = s_scr[...]


def _ret_tables(t_eff):
    cp = RET_CHUNK
    f32 = np.float32
    log_g = np.log1p(-np.power(f32(2.0), f32(-5.0) - np.arange(RET_HEADS, dtype=f32)))
    idx = np.arange(cp, dtype=f32)
    live = idx < t_eff
    diff = idx[:, None] - idx[None, :]
    ok = (diff >= 0) & live[:, None] & live[None, :]
    decay = np.where(ok, np.exp(log_g[:, None, None] * np.maximum(diff, f32(0.0))), f32(0.0))
    q_dec = np.where(live, np.exp(log_g[:, None] * (idx + f32(1.0))), f32(0.0))
    k_dec = np.where(live, np.exp(log_g[:, None] * (f32(t_eff - 1.0) - idx)), f32(0.0))
    c_dec = np.exp(log_g * f32(t_eff))

    def pair_lanes(t):
        t = t.reshape(RET_HEADS // 2, 2, cp)
        return np.repeat(np.transpose(t, (0, 2, 1)), RET_DK, axis=2)

    c_rows = np.repeat(c_dec.reshape(RET_HEADS // 2, 2), RET_DK, axis=1)
    c_rows = np.broadcast_to(c_rows[:, :, None], (RET_HEADS // 2, 2 * RET_DK, RET_DV))
    as32 = lambda t: np.ascontiguousarray(t, dtype=f32)
    return as32(decay), as32(pair_lanes(q_dec)), as32(pair_lanes(k_dec)), as32(c_rows)


def _rec_mix(proj, s0, h0, c0, ret_g, conv_w, conv_b, w_rg, b_rg, w_ig, b_ig, lam):
    b, t, _ = proj.shape
    rows = RET_CHUNK if t % RET_CHUNK == 0 else t
    assert t % rows == 0 and rows % SUBLANES == 0 and (t == rows or rows == RET_CHUNK)
    n = t // rows
    nb = REC_SEQS if b % REC_SEQS == 0 else 1
    dec, qdec, kdec, cdec = _ret_tables(rows)
    hp = RET_HEADS // 2
    w = LRU_WIDTH
    per_b3 = lambda i, c: (i, 0, 0)
    outs = pl.pallas_call(
        functools.partial(_rec_mix_kernel, rows=rows),
        grid=(b // nb, n),
        in_specs=[
            pl.BlockSpec((nb, rows, REC_IN), lambda i, c: (i, c, 0)),
            pl.BlockSpec((nb, hp, 2 * RET_DK, RET_DV), lambda i, c: (i, 0, 0, 0)),
            pl.BlockSpec((nb, 1, w), per_b3),
            pl.BlockSpec((nb, CONV_WIDTH - 1, w), per_b3),
            _const_spec(dec.shape), _const_spec(qdec.shape), _const_spec(kdec.shape), _const_spec(cdec.shape),
            _const_spec((RET_HEADS, RET_DV)), _const_spec((CONV_WIDTH, w)), _const_spec((1, w)),
            _const_spec((LRU_BLOCKS, LRU_BLOCK, LRU_BLOCK)), _const_spec((1, w)),
            _const_spec((LRU_BLOCKS, LRU_BLOCK, LRU_BLOCK)), _const_spec((1, w)), _const_spec((1, w)),
        ],
        out_specs=[
            pl.BlockSpec((nb, rows, D_MODEL), lambda i, c: (i, c, 0)),
            pl.BlockSpec((nb, hp, 2 * RET_DK, RET_DV), lambda i, c: (i, 0, 0, 0)),
            pl.BlockSpec((nb, 1, w), per_b3),
            pl.BlockSpec((nb, CONV_WIDTH - 1, w), per_b3),
        ],
        out_shape=[
            jax.ShapeDtypeStruct((b, t, D_MODEL), BF16),
            jax.ShapeDtypeStruct((b, hp, 2 * RET_DK, RET_DV), F32),
            jax.ShapeDtypeStruct((b, 1, w), F32),
            jax.ShapeDtypeStruct((b, CONV_WIDTH - 1, w), F32),
        ],
        scratch_shapes=[
            pltpu.VMEM((nb, hp, 2 * RET_DK, RET_DV), F32),
            pltpu.VMEM((nb, 1, w), F32),
            pltpu.VMEM((nb, SUBLANES, w), F32),
        ],
        compiler_params=_cparams(("arbitrary", "arbitrary")),
        name="rec_mix",
    )(proj, s0.reshape(b, hp, 2 * RET_DK, RET_DV), h0.reshape(b, 1, w), c0,
      dec, qdec, kdec, cdec, ret_g, conv_w, conv_b.reshape(1, w),
      w_rg.astype(BF16), b_rg.reshape(1, w), w_ig.astype(BF16), b_ig.reshape(1, w), lam.reshape(1, w))
    mix, s_ret, h_last, c_new = outs
    return mix, s_ret.reshape(b, RET_HEADS, RET_DK, RET_DV), h_last.reshape(b, w), c_new


def _out_ffn_kernel(mix_ref, x_ref, wo_ref, g_ref, wg_ref, wu_ref, wd_ref, o_ref, act_ref):
    y = x_ref[...] + jnp.dot(mix_ref[...].astype(BF16), wo_ref[...], preferred_element_type=F32)
    h = _rmsnorm(y, g_ref[...]).astype(BF16)
    for c in range(wg_ref.shape[1] // FF_CHUNK):
        cols = slice(c * FF_CHUNK, (c + 1) * FF_CHUNK)
        gate = jnp.dot(h, wg_ref[:, cols], preferred_element_type=F32)
        up = jnp.dot(h, wu_ref[:, cols], preferred_element_type=F32)
        act_ref[:, cols] = (gate * jax.nn.sigmoid(gate) * up).astype(BF16)
    o_ref[...] = y + jnp.dot(act_ref[...], wd_ref[...], preferred_element_type=F32)


def _ffn_weights(w_out, g, w_gate, w_up, w_down):
    return (w_out.astype(BF16), g.reshape(1, g.shape[0]), w_gate.astype(BF16), w_up.astype(BF16),
            w_down.astype(BF16))


def _out_ffn(mix2d, x2d, wts):
    m, d = x2d.shape
    tm = min(TOKEN_TILE, m)
    row = lambda i: (i, 0)
    return pl.pallas_call(
        _out_ffn_kernel,
        grid=(m // tm,),
        in_specs=[pl.BlockSpec((tm, d), row), pl.BlockSpec((tm, d), row)] + [_const_spec(w.shape) for w in wts],
        out_specs=pl.BlockSpec((tm, d), row),
        out_shape=jax.ShapeDtypeStruct((m, d), F32),
        scratch_shapes=[pltpu.VMEM((tm, wts[2].shape[1]), BF16)],
        compiler_params=_cparams(("arbitrary",)),
        name="out_ffn",
    )(mix2d, x2d, *wts)


def _pair_rms(a, g2, first):
    sq = a * a
    s_a = jnp.sum(jnp.where(first, sq, 0.0), axis=-1, keepdims=True)
    s_b = jnp.sum(jnp.where(first, 0.0, sq), axis=-1, keepdims=True)
    ms = jnp.where(first, s_a, s_b) * (1.0 / ATT_DH)
    return a * lax.rsqrt(ms + EPS) * g2


def _qkv_slabs(x, g_ref, w_ref, qg_ref, kg_ref):
    h = _rmsnorm(x, g_ref[...]).astype(BF16)
    first = lax.broadcasted_iota(jnp.int32, (x.shape[0], LANES), 1) < ATT_DH
    slabs = []
    for c in range(w_ref.shape[1] // MXU_COLS):
        part = jnp.dot(h, w_ref[:, c * MXU_COLS:(c + 1) * MXU_COLS], preferred_element_type=F32)
        slabs += [part[:, i * LANES:(i + 1) * LANES] for i in range(MXU_COLS // LANES)]
    q = [_pair_rms(slabs[p], qg_ref[...], first) for p in range(NPAIR)]
    k = [_pair_rms(slabs[NPAIR + p], kg_ref[...], first) for p in range(NPAIR)]
    v = slabs[2 * NPAIR:]
    return q, k, v


def _qkv_flat_kernel(x_ref, g_ref, w_ref, qg_ref, kg_ref, q_ref, k_ref, v_ref):
    q, k, v = _qkv_slabs(x_ref[...], g_ref, w_ref, qg_ref, kg_ref)
    q_ref[...] = jnp.concatenate(q, axis=1)
    k_ref[...] = jnp.concatenate(k, axis=1)
    v_ref[...] = jnp.concatenate(v, axis=1)


def _qkv_perm_kernel(x_ref, g_ref, w_ref, qg_ref, kg_ref, q_ref, k_ref, v_ref, xs_ref):
    n = x_ref.shape[0] // MAX_DIL
    for p in range(NPAIR):
        xs_ref[p] = x_ref[:, p * LANES:(p + 1) * LANES]
    x = jnp.concatenate(
        [jnp.concatenate([xs_ref[p, pl.ds(r, n, stride=MAX_DIL), :] for p in range(NPAIR)], axis=1)
         for r in range(MAX_DIL)], axis=0)
    q, k, v = _qkv_slabs(x, g_ref, w_ref, qg_ref, kg_ref)
    for ref, slabs in ((q_ref, q), (k_ref, k), (v_ref, v)):
        for p in range(NPAIR):
            for r in range(MAX_DIL):
                ref[p, r] = slabs[p][r * n:(r + 1) * n]


def _head_gain(g):
    return jnp.tile(g.reshape(1, ATT_DH), (1, LANES // ATT_DH))


def _qkv_flat(x2d, g, w_bf, qg, kg):
    m, d = x2d.shape
    tm = min(TOKEN_TILE, m)
    row = lambda i: (i, 0)
    return pl.pallas_call(
        _qkv_flat_kernel,
        grid=(m // tm,),
        in_specs=[pl.BlockSpec((tm, d), row), _const_spec((1, d)), _const_spec(w_bf.shape),
                  _const_spec((1, LANES)), _const_spec((1, LANES))],
        out_specs=[pl.BlockSpec((tm, d), row)] * 3,
        out_shape=[jax.ShapeDtypeStruct((m, d), F32)] * 3,
        compiler_params=_cparams(("arbitrary",)),
        name="qkv_flat",
    )(x2d, g.reshape(1, d), w_bf, _head_gain(qg), _head_gain(kg))


def _qkv_perm(x, g, w_bf, qg, kg):
    b, t, d = x.shape
    ns = t // SUPER
    tm = TOKEN_TILE
    nt = SUPER // tm
    n = tm // MAX_DIL
    out_spec = pl.BlockSpec((None, NPAIR, None, MAX_DIL, None, n, LANES), lambda i, s, j: (i, 0, s, 0, j, 0, 0))
    out_shape = jax.ShapeDtypeStruct((b, NPAIR, ns, MAX_DIL, nt, n, LANES), F32)
    q, k, v = pl.pallas_call(
        _qkv_perm_kernel,
        grid=(b, ns, nt),
        in_specs=[pl.BlockSpec((None, tm, d), lambda i, s, j: (i, s * nt + j, 0)),
                  _const_spec((1, d)), _const_spec(w_bf.shape), _const_spec((1, LANES)), _const_spec((1, LANES))],
        out_specs=[out_spec] * 3,
        out_shape=[out_shape] * 3,
        scratch_shapes=[pltpu.VMEM((NPAIR, tm, LANES), F32)],
        compiler_params=_cparams(("arbitrary",) * 3),
        name="qkv_perm",
    )(x, g.reshape(1, d), w_bf, _head_gain(qg), _head_gain(kg))
    return tuple(a.reshape(b, NPAIR, t, LANES) for a in (q, k, v))


def _alibi_slopes():
    return np.power(np.float32(2.0), np.float32(-8.0) * np.arange(1, ATT_HEADS + 1, dtype=np.float32) / ATT_HEADS)


PREV_SLOTS = 32
ATT_UNROLL = 16
LOG2E = 1.4426950408889634


def _band_bias():
    i = np.arange(ATT_BLOCK)
    c = np.arange(2 * ATT_BLOCK)
    q_idx = {1: MAX_DIL * (i % 8) + i // 8,
             4: 4 * (i % 32) + i // 32,
             16: i}
    k_idx = {1: (MAX_DIL * (c % 8) + c // 16, (c // 8) % 2 == 0),
             4: (4 * (c % 32) + c // 64, (c // 32) % 2 == 0),
             16: (c % ATT_BLOCK, c < ATT_BLOCK)}
    slopes = _alibi_slopes().reshape(ATT_HEADS // 2, 2, 1, 1)
    neg = np.float32(NEG)
    out = []
    for (w, d) in PATTERNS:
        span = w // d
        km, is_prev = k_idx[d]
        diff = q_idx[d][:, None] - (km - np.where(is_prev, ATT_BLOCK, 0))[None, :]
        valid = (diff >= 0) & (diff <= span)
        bias = -slopes * (d * diff).astype(np.float32)[None, None]
        bias = bias * np.float32(LOG2E)
        variants = [np.where(valid, bias, neg), np.where(valid & ~is_prev[None, :], bias, neg)]
        out.append(np.stack([v.reshape(ATT_HEADS // 2, 2 * ATT_BLOCK, 2 * ATT_BLOCK) for v in variants]))
    return np.stack(out).astype(np.float32)


def _attend_pair(q, k, v, bias, first):
    qs = jnp.concatenate([jnp.where(first, q, 0.0), jnp.where(first, 0.0, q)], axis=0).astype(BF16)
    s = lax.dot_general(qs, k.astype(BF16), (((1,), (1,)), ((), ())), preferred_element_type=F32) + bias
    m = jnp.max(s, axis=-1, keepdims=True)
    e = jnp.exp2(s - m)
    den = jnp.sum(e, axis=-1, keepdims=True)
    o = jnp.dot(e.astype(BF16), v.astype(BF16), preferred_element_type=F32)
    halves = lambda a: jnp.where(first, a[:ATT_BLOCK], a[ATT_BLOCK:])
    return halves(o), halves(m), halves(den)


def _attn_prompt_kernel(q_ref, kc_ref, kp_ref, vc_ref, vp_ref, bias_ref, o_ref, ko_ref, vo_ref,
                        kext, vext, oacc, macc, dacc, nat_ref):
    sb = pl.program_id(2)
    first_sb = (sb == 0).astype(jnp.int32)
    first = lax.broadcasted_iota(jnp.int32, (ATT_BLOCK, LANES), 1) < ATT_DH
    scale = (ATT_DH ** -0.5) * LOG2E
    ps = PREV_SLOTS

    for src_p, src_c, ext in ((kp_ref, kc_ref, kext), (vp_ref, vc_ref, vext)):
        for r in range(MAX_DIL):
            ext[r, 0:ps, :] = src_p[r * ATT_BLOCK + ATT_BLOCK - ps:(r + 1) * ATT_BLOCK, :]
            ext[r, ps:ps + ATT_BLOCK, :] = src_c[r * ATT_BLOCK:(r + 1) * ATT_BLOCK, :]

    def attend_store(p, variant, q_chunks, k, v):
        n = ATT_BLOCK // len(q_chunks)
        q = jnp.concatenate([q_ref[sl, :] for sl in q_chunks], axis=0) * scale
        o, m, den = _attend_pair(q, k, v, bias_ref[p, variant], first)
        for ci, sl in enumerate(q_chunks):
            oacc[p, sl, :] = o[ci * n:(ci + 1) * n]
            macc[p, sl, :] = m[ci * n:(ci + 1) * n]
            dacc[p, sl, :] = den[ci * n:(ci + 1) * n]

    n0 = ATT_BLOCK // MAX_DIL

    def block0(bi):
        base = pl.multiple_of(bi * n0, n0)
        ksl = pl.ds(base + ps - n0, 2 * n0)
        k = jnp.concatenate([kext[r, ksl, :] for r in range(MAX_DIL)], axis=0)
        v = jnp.concatenate([vext[r, ksl, :] for r in range(MAX_DIL)], axis=0)
        attend_store(0, first_sb * jnp.asarray(bi == 0, jnp.int32),
                     [pl.ds(r * ATT_BLOCK + base, n0) for r in range(MAX_DIL)], k, v)

    d1 = PATTERNS[1][1]
    nc1 = MAX_DIL // d1
    n1 = ATT_BLOCK // nc1

    def block1(idx):
        r4, mb = idx // (ATT_BLOCK // n1), idx % (ATT_BLOCK // n1)
        base = pl.multiple_of(mb * n1, n1)
        ksl = pl.ds(base + ps - n1, 2 * n1)
        k = jnp.concatenate([kext[r4 + d1 * c, ksl, :] for c in range(nc1)], axis=0)
        v = jnp.concatenate([vext[r4 + d1 * c, ksl, :] for c in range(nc1)], axis=0)
        attend_store(1, first_sb * jnp.asarray(mb == 0, jnp.int32),
                     [pl.ds(pl.multiple_of((r4 + d1 * c) * ATT_BLOCK + base, n1), n1) for c in range(nc1)], k, v)

    def block2(r):
        sl = pl.ds(pl.multiple_of(r * ATT_BLOCK, ATT_BLOCK), ATT_BLOCK)
        k = jnp.concatenate([kp_ref[sl, :], kc_ref[sl, :]], axis=0)
        v = jnp.concatenate([vp_ref[sl, :], vc_ref[sl, :]], axis=0)
        attend_store(2, first_sb, [sl], k, v)

    for block in (block0, block1, block2):
        def body(it, carry, block=block):
            for u in range(ATT_UNROLL):
                block(it * ATT_UNROLL + u)
            return carry

        lax.fori_loop(0, MAX_DIL // ATT_UNROLL, body, 0)

    def comb(r, carry):
        sl = pl.ds(pl.multiple_of(r * ATT_BLOCK, ATT_BLOCK), ATT_BLOCK)
        m0, m1, m2 = macc[0, sl, :], macc[1, sl, :], macc[2, sl, :]
        mx = jnp.maximum(jnp.maximum(m0, m1), m2)
        w0, w1, w2 = jnp.exp2(m0 - mx), jnp.exp2(m1 - mx), jnp.exp2(m2 - mx)
        num = w0 * oacc[0, sl, :] + w1 * oacc[1, sl, :] + w2 * oacc[2, sl, :]
        den = w0 * dacc[0, sl, :] + w1 * dacc[1, sl, :] + w2 * dacc[2, sl, :]
        o_ref[pl.ds(r, ATT_BLOCK, stride=MAX_DIL), :] = num / den
        return carry

    lax.fori_loop(0, SUPER // ATT_BLOCK, comb, 0)

    @pl.when(sb == pl.num_programs(2) - 1)
    def _():
        for src, dst in ((kc_ref, ko_ref), (vc_ref, vo_ref)):
            for r in range(MAX_DIL):
                nat_ref[pl.ds(r, ATT_BLOCK, stride=MAX_DIL), :] = src[r * ATT_BLOCK:(r + 1) * ATT_BLOCK, :]
            dst[...] = jnp.transpose(nat_ref[...])


def _attn_prompt(q, k, v):
    b, npair, t, _ = q.shape
    ns = t // SUPER
    bias = _band_bias()
    cur = pl.BlockSpec((None, None, SUPER, LANES), lambda i, p, s: (i, p, s, 0))
    prev = pl.BlockSpec((None, None, SUPER, LANES), lambda i, p, s: (i, p, jnp.maximum(s - 1, 0), 0))
    cache = pl.BlockSpec((None, LANES, SUPER), lambda i, p, s: (i, p, 0))
    cache_shape = jax.ShapeDtypeStruct((b, npair * LANES, SUPER), F32)
    return pl.pallas_call(
        _attn_prompt_kernel,
        grid=(b, npair, ns),
        in_specs=[cur, cur, prev, cur, prev,
                  pl.BlockSpec((len(PATTERNS), 2, None, 2 * ATT_BLOCK, 2 * ATT_BLOCK),
                               lambda i, p, s: (0, 0, p, 0, 0))],
        out_specs=[pl.BlockSpec((None, SUPER, LANES), lambda i, p, s: (i, s, p)), cache, cache],
        out_shape=[jax.ShapeDtypeStruct((b, t, npair * LANES), F32), cache_shape, cache_shape],
        scratch_shapes=[pltpu.VMEM((MAX_DIL, PREV_SLOTS + ATT_BLOCK, LANES), F32),
                        pltpu.VMEM((MAX_DIL, PREV_SLOTS + ATT_BLOCK, LANES), F32),
                        pltpu.VMEM((len(PATTERNS), SUPER, LANES), F32),
                        pltpu.VMEM((len(PATTERNS), SUPER, LANES), F32),
                        pltpu.VMEM((len(PATTERNS), SUPER, LANES), F32),
                        pltpu.VMEM((SUPER, LANES), F32)],
        compiler_params=_cparams(("arbitrary",) * 3),
        name="attn_prompt",
    )(q, k, k, v, v, bias)


QUAD = 4
NEW_PAD = 128


def _sample_bias(s_len, n_past):
    s = np.arange(s_len)
    col = np.arange(n_past + NEW_PAD)
    dist = (n_past + s)[:, None] - col[None, :]
    real = (col < n_past + s_len)[None, :]
    slopes = _alibi_slopes().reshape(ATT_HEADS // QUAD, QUAD, 1, 1)
    out = []
    for (w, d) in PATTERNS:
        valid = (dist >= 0) & (dist % d == 0) & (dist <= w) & real
        bias = np.where(valid[None, None], -slopes * dist.astype(np.float32)[None, None], np.float32(NEG))
        out.append(bias.reshape(ATT_HEADS // QUAD, QUAD * s_len, n_past + NEW_PAD))
    return np.stack(out).astype(np.float32)


def _attn_sample_kernel(q_ref, kn_ref, vn_ref, ck_ref, cv_ref, bias_ref, o_ref, *, s_len):
    n_past = ck_ref.shape[1]
    width = QUAD * ATT_DH
    rows = QUAD * s_len
    own = (lax.broadcasted_iota(jnp.int32, (rows, width), 0) // s_len
           == lax.broadcasted_iota(jnp.int32, (rows, width), 1) // ATT_DH)
    nt = (((1,), (1,)), ((), ()))
    for quad in range(q_ref.shape[1] // width):
        cols = slice(quad * width, (quad + 1) * width)
        q = q_ref[:, cols] * (ATT_DH ** -0.5)
        qm = jnp.where(own, jnp.concatenate([q] * QUAD, axis=0), 0.0).astype(BF16)
        kn = _pad_rows(kn_ref[:, cols], NEW_PAD).astype(BF16)
        vn = _pad_rows(vn_ref[:, cols], NEW_PAD).astype(BF16)
        s_c = jnp.dot(qm, ck_ref[cols, :].astype(BF16), preferred_element_type=F32)
        s_n = lax.dot_general(qm, kn, nt, preferred_element_type=F32)
        es_c, es_n, dens, lses = [], [], [], []
        for p in range(len(PATTERNS)):
            sc = s_c + bias_ref[p, quad, :, 0:n_past]
            sn = s_n + bias_ref[p, quad, :, n_past:n_past + NEW_PAD]
            m = jnp.maximum(jnp.max(sc, axis=-1, keepdims=True), jnp.max(sn, axis=-1, keepdims=True))
            ec, en = jnp.exp(sc - m), jnp.exp(sn - m)
            den = jnp.sum(ec, axis=-1, keepdims=True) + jnp.sum(en, axis=-1, keepdims=True)
            es_c.append(ec)
            es_n.append(en)
            dens.append(den)
            lses.append(m + jnp.log(den))
        mx = jnp.maximum(jnp.maximum(lses[0], lses[1]), lses[2])
        ws = [jnp.exp(l - mx) for l in lses]
        wsum = ws[0] + ws[1] + ws[2]
        coef = [w / (wsum * d) for w, d in zip(ws, dens)]
        c_c = coef[0] * es_c[0] + coef[1] * es_c[1] + coef[2] * es_c[2]
        c_n = coef[0] * es_n[0] + coef[1] * es_n[1] + coef[2] * es_n[2]
        o = (lax.dot_general(c_c.astype(BF16), cv_ref[cols, :].astype(BF16), nt, preferred_element_type=F32)
             + jnp.dot(c_n.astype(BF16), vn, preferred_element_type=F32))
        o = jnp.where(own, o, 0.0)
        acc = o[0:s_len]
        for hh in range(1, QUAD):
            acc = acc + o[hh * s_len:(hh + 1) * s_len]
        o_ref[:, cols] = acc.astype(o_ref.dtype)


def _attn_sample(q, kn, vn, ck, cv):
    b, s_len, d = q.shape
    n_past = ck.shape[2]
    bias = _sample_bias(s_len, n_past)
    new = pl.BlockSpec((None, s_len, d), lambda i: (i, 0, 0))
    old = pl.BlockSpec((None, d, n_past), lambda i: (i, 0, 0))
    return pl.pallas_call(
        functools.partial(_attn_sample_kernel, s_len=s_len),
        grid=(b,),
        in_specs=[new, new, new, old, old, _const_spec(bias.shape)],
        out_specs=new,
        out_shape=jax.ShapeDtypeStruct((b, s_len, d), BF16),
        compiler_params=_cparams(("arbitrary",)),
        name="attn_sample",
    )(q, kn, vn, ck, cv, bias)


def kernel(x_prompt, x_sample, state_ret, state_lru, state_conv, cache_k, cache_v, norm_mix, norm_ffn, w_in_rec, w_out_rec, ret_norm_g, conv_w, conv_b, w_rgate, b_rgate, w_igate, b_igate, lru_lambda, w_in_att, w_out_att, q_norm_g, k_norm_g, w_ffn_gate, w_ffn_up, w_ffn_down):
    bp, tp, d = x_prompt.shape
    bs, ts, _ = x_sample.shape
    depth = norm_mix.shape[0]
    assert tp % SUPER == 0 and d == D_MODEL
    yp, ys = x_prompt, x_sample
    ret_p, lru_p, conv_p, kp, vp = [], [], [], [], []
    ret_s, lru_s, conv_s, ksl, vsl = [], [], [], [], []
    for l in range(depth):
        i = l // 2
        if l % 2 == 0:
            ffn = _ffn_weights(w_out_rec[i], norm_ffn[l], w_ffn_gate[l], w_ffn_up[l], w_ffn_down[l])
            w_in = w_in_rec[i].astype(BF16)
            rec_prm = (ret_norm_g[i], conv_w[i], conv_b[i], w_rgate[i], b_rgate[i], w_igate[i], b_igate[i],
                       lru_lambda[i])
            proj = _norm_proj(yp.reshape(bp * tp, d), norm_mix[l], w_in).reshape(bp, tp, REC_IN)
            mix, sr, sl, sc = _rec_mix(proj, jnp.zeros((bp, RET_HEADS, RET_DK, RET_DV), F32),
                                       jnp.zeros((bp, LRU_WIDTH), F32),
                                       jnp.zeros((bp, CONV_WIDTH - 1, LRU_WIDTH), F32), *rec_prm)
            yp = _out_ffn(mix.reshape(bp * tp, d), yp.reshape(bp * tp, d), ffn).reshape(bp, tp, d)
            ret_p.append(sr); lru_p.append(sl); conv_p.append(sc)
            proj = _norm_proj(ys.reshape(bs * ts, d), norm_mix[l], w_in).reshape(bs, ts, REC_IN)
            mix, sr, sl, sc = _rec_mix(proj, state_ret[i], state_lru[i], state_conv[i], *rec_prm)
            ys = _out_ffn(mix.reshape(bs * ts, d), ys.reshape(bs * ts, d), ffn).reshape(bs, ts, d)
            ret_s.append(sr); lru_s.append(sl); conv_s.append(sc)
        else:
            ffn = _ffn_weights(w_out_att[i], norm_ffn[l], w_ffn_gate[l], w_ffn_up[l], w_ffn_down[l])
            w_in = w_in_att[i].astype(BF16)
            q, k, v = _qkv_perm(yp, norm_mix[l], w_in, q_norm_g[i], k_norm_g[i])
            o, kc, vc = _attn_prompt(q, k, v)
            yp = _out_ffn(o.reshape(bp * tp, d), yp.reshape(bp * tp, d), ffn).reshape(bp, tp, d)
            from_slabs = lambda c: jnp.transpose(c.reshape(bp, ATT_HEADS, ATT_DH, SUPER), (0, 3, 1, 2))
            kp.append(from_slabs(kc)); vp.append(from_slabs(vc))
            qs, kn, vn = _qkv_flat(ys.reshape(bs * ts, d), norm_mix[l], w_in, q_norm_g[i], k_norm_g[i])
            n_past = cache_k.shape[2]
            to_slabs = lambda c: jnp.transpose(c, (0, 2, 3, 1)).reshape(bs, d, n_past)
            o = _attn_sample(qs.reshape(bs, ts, d), kn.reshape(bs, ts, d), vn.reshape(bs, ts, d),
                             to_slabs(cache_k[i]), to_slabs(cache_v[i]))
            ys = _out_ffn(o.reshape(bs * ts, d), ys.reshape(bs * ts, d), ffn).reshape(bs, ts, d)
            ksl.append(kn.reshape(bs, ts, ATT_HEADS, ATT_DH)); vsl.append(vn.reshape(bs, ts, ATT_HEADS, ATT_DH))
    return (yp, ys,
            jnp.stack(ret_p), jnp.stack(lru_p), jnp.stack(conv_p), jnp.stack(kp), jnp.stack(vp),
            jnp.stack(ret_s), jnp.stack(lru_s), jnp.stack(conv_s), jnp.stack(ksl), jnp.stack(vsl))
```

```python
import functools

import jax
import numpy as np
import jax.numpy as jnp
from jax import lax
from jax.experimental import pallas as pl
from jax.experimental.pallas import tpu as pltpu

F32 = jnp.float32
BF16 = jnp.bfloat16

EPS = 1e-6
D_MODEL = 1024
RET_HEADS = 4
RET_DK = 64
RET_DV = 128
RET_CHUNK = 128
LRU_WIDTH = 512
LRU_BLOCKS = 4
LRU_BLOCK = LRU_WIDTH // LRU_BLOCKS
CONV_WIDTH = 4
LRU_C = 8.0
REC_IN = 2 * RET_HEADS * RET_DK + 2 * RET_HEADS * RET_DV + 2 * LRU_WIDTH
ATT_HEADS = 16
ATT_DH = 64
PATTERNS = ((128, 1), (512, 4), (2048, 16))
ATT_BLOCK = 128
SUPER = 2048
MAX_DIL = 16
NEG = -1e30

LANES = 128
SUBLANES = 8
MXU_COLS = 256
NPAIR = D_MODEL // LANES
VMEM_LIMIT = 56 * 1024 * 1024

FF_CHUNK = 256
TOKEN_TILE = 512
WIDE_TILE = 1024
STREAM_ROWS = 512
REC_SEQS = 4


def _cparams(sem):
    return pltpu.CompilerParams(dimension_semantics=sem, vmem_limit_bytes=VMEM_LIMIT)


def _const_spec(shape):
    nd = len(shape)
    return pl.BlockSpec(shape, lambda *_: (0,) * nd, pipeline_mode=pl.Buffered(1))


def _rmsnorm(x, g):
    return x * lax.rsqrt(jnp.mean(x * x, axis=-1, keepdims=True) + EPS) * g


def _row_parts(rows):
    n = max(1, rows // STREAM_ROWS)
    return [slice(i * (rows // n), (i + 1) * (rows // n)) for i in range(n)]


def _norm_proj_kernel(x_ref, g_ref, w_ref, o_ref):
    for rs in _row_parts(x_ref.shape[0]):
        h = _rmsnorm(x_ref[rs, :], g_ref[...]).astype(BF16)
        o_ref[rs, :] = jnp.dot(h, w_ref[...], preferred_element_type=F32)


def _norm_proj(x2d, g, w_bf):
    m, d = x2d.shape
    n = w_bf.shape[1]
    tm = min(WIDE_TILE, m)
    return pl.pallas_call(
        _norm_proj_kernel,
        grid=(m // tm,),
        in_specs=[pl.BlockSpec((tm, d), lambda i: (i, 0)), _const_spec((1, d)), _const_spec((d, n))],
        out_specs=pl.BlockSpec((tm, n), lambda i: (i, 0)),
        out_shape=jax.ShapeDtypeStruct((m, n), F32),
        compiler_params=_cparams(("arbitrary",)),
        name="norm_proj",
    )(x2d, g.reshape(1, d), w_bf)


def _linear_scan(a, b, h0):
    rows, w = a.shape
    a = a.reshape(rows // SUBLANES, SUBLANES, w)
    b = b.reshape(rows // SUBLANES, SUBLANES, w)
    sub = lax.broadcasted_iota(jnp.int32, a.shape, 1)
    s = 1
    while s < SUBLANES:
        keep = sub >= s
        b = jnp.where(keep, a * pltpu.roll(b, s, axis=1), 0.0) + b
        a = jnp.where(keep, a * pltpu.roll(a, s, axis=1), a)
        s *= 2
    groups = []
    carry = h0
    for i in range(rows // SUBLANES):
        groups.append(b[i] + a[i] * carry)
        carry = groups[-1][SUBLANES - 1:SUBLANES]
    return jnp.concatenate(groups, axis=0)


def _pad_rows(x, rows):
    if x.shape[0] == rows:
        return x
    return jnp.concatenate([x, jnp.zeros((rows - x.shape[0], x.shape[1]), x.dtype)], axis=0)


def _rec_mix_kernel(proj_ref, s0_ref, h0_ref, c0_ref, dec_ref, qdec_ref, kdec_ref, cdec_ref,
                    retg_ref, convw_ref, convb_ref, wrg_ref, brg_ref, wig_ref, big_ref, lam_ref,
                    mix_ref, sret_ref, hlast_ref, cnew_ref,
                    s_scr, h_scr, xe_scr, *, rows):
    cp = RET_CHUNK
    hk = RET_HEADS * RET_DK
    v_off, g_off = 2 * hk, 2 * hk + RET_HEADS * RET_DV
    xl_off = g_off + RET_HEADS * RET_DV
    yl_off = xl_off + LRU_WIDTH

    @pl.when(pl.program_id(1) == 0)
    def _():
        s_scr[...] = s0_ref[...]
        h_scr[...] = h0_ref[...]
        xe_scr[...] = jnp.zeros(xe_scr.shape, F32)
        xe_scr[:, SUBLANES - (CONV_WIDTH - 1):SUBLANES, :] = c0_ref[...]

    lane = lax.broadcasted_iota(jnp.int32, (cp, LANES), 1)
    first = lane < RET_DK
    srow_first = lax.broadcasted_iota(jnp.int32, (2 * RET_DK, RET_DV), 0) < RET_DK
    nl = -lam_ref[...]
    softplus = jnp.maximum(nl, 0.0) + jnp.log1p(jnp.exp(-jnp.abs(nl)))

    for bi in range(proj_ref.shape[0]):
        for p in range(RET_HEADS // 2):
            q2 = _pad_rows(proj_ref[bi, :, p * LANES:(p + 1) * LANES], cp)
            k2 = _pad_rows(proj_ref[bi, :, hk + p * LANES:hk + (p + 1) * LANES], cp) * (RET_DK ** -0.5)
            kb = k2.astype(BF16)
            kdt = jnp.transpose(k2 * kdec_ref[p]).astype(BF16)
            s2 = s_scr[bi, p]
            s2b = s2.astype(BF16)
            new_s = []
            for hh in range(2):
                h = 2 * p + hh
                sel = first if hh == 0 else jnp.logical_not(first)
                qh = jnp.where(sel, q2, 0.0)
                v = _pad_rows(proj_ref[bi, :, v_off + h * RET_DV:v_off + (h + 1) * RET_DV], cp)
                vb = v.astype(BF16)
                att = lax.dot_general(qh.astype(BF16), kb, (((1,), (1,)), ((), ())),
                                      preferred_element_type=F32) * dec_ref[h]
                o = (jnp.dot(att.astype(BF16), vb, preferred_element_type=F32)
                     + jnp.dot((qh * qdec_ref[p]).astype(BF16), s2b, preferred_element_type=F32))
                new_s.append(jnp.dot(kdt, vb, preferred_element_type=F32))
                o = o[:rows]
                o = o * lax.rsqrt(jnp.mean(o * o, axis=-1, keepdims=True) + EPS) * retg_ref[h:h + 1, :]
                gate = proj_ref[bi, :, g_off + h * RET_DV:g_off + (h + 1) * RET_DV]
                mix_ref[bi, :, h * RET_DV:(h + 1) * RET_DV] = (
                    o * (gate * jax.nn.sigmoid(gate))).astype(mix_ref.dtype)
            s_scr[bi, p] = s2 * cdec_ref[p] + jnp.where(srow_first, new_s[0], new_s[1])

        x = proj_ref[bi, :, xl_off:xl_off + LRU_WIDTH]
        xe = jnp.concatenate([xe_scr[bi], x], axis=0)
        xc = convb_ref[...]
        for j in range(CONV_WIDTH):
            back = CONV_WIDTH - 1 - j
            xj = x if back == 0 else pltpu.roll(xe, back, axis=0)[SUBLANES:]
            xc = xc + xj * convw_ref[j:j + 1, :]
        xe_scr[bi] = xe[rows:]
        cnew_ref[bi] = xe_scr[bi, pl.ds(SUBLANES - (CONV_WIDTH - 1), CONV_WIDTH - 1), :]

        xcb = _pad_rows(xc, max(rows, 2 * SUBLANES)).astype(BF16)
        r_parts, i_parts = [], []
        for n in range(LRU_BLOCKS):
            xb = xcb[:, n * LRU_BLOCK:(n + 1) * LRU_BLOCK]
            r_parts.append(jnp.dot(xb, wrg_ref[n], preferred_element_type=F32)[:rows])
            i_parts.append(jnp.dot(xb, wig_ref[n], preferred_element_type=F32)[:rows])
        r = jax.nn.sigmoid(jnp.concatenate(r_parts, axis=1) + brg_ref[...])
        ig = jax.nn.sigmoid(jnp.concatenate(i_parts, axis=1) + big_ref[...])
        a = jnp.exp((-LRU_C) * r * softplus)
        one_m = 1.0 - a * a
        root = jnp.where(one_m > 0.0, one_m * lax.rsqrt(one_m), 0.0)
        hseq = _linear_scan(a, root * (ig * xc), h_scr[bi])
        h_scr[bi] = hseq[rows - 1:rows, :]
        hlast_ref[bi] = hseq[rows - 1:rows, :]
        yl = proj_ref[bi, :, yl_off:yl_off + LRU_WIDTH]
        gelu = 0.5 * yl * (1.0 + jnp.tanh(0.7978845608028654 * (yl + 0.044715 * (yl * yl * yl))))
        mix_ref[bi, :, RET_HEADS * RET_DV:] = (hseq * gelu).astype(mix_ref.dtype)
    sret_ref[...] = s_scr[...]


def _ret_tables(t_eff):
    cp = RET_CHUNK
    f32 = np.float32
    log_g = np.log1p(-np.power(f32(2.0), f32(-5.0) - np.arange(RET_HEADS, dtype=f32)))
    idx = np.arange(cp, dtype=f32)
    live = idx < t_eff
    diff = idx[:, None] - idx[None, :]
    ok = (diff >= 0) & live[:, None] & live[None, :]
    decay = np.where(ok, np.exp(log_g[:, None, None] * np.maximum(diff, f32(0.0))), f32(0.0))
    q_dec = np.where(live, np.exp(log_g[:, None] * (idx + f32(1.0))), f32(0.0))
    k_dec = np.where(live, np.exp(log_g[:, None] * (f32(t_eff - 1.0) - idx)), f32(0.0))
    c_dec = np.exp(log_g * f32(t_eff))

    def pair_lanes(t):
        t = t.reshape(RET_HEADS // 2, 2, cp)
        return np.repeat(np.transpose(t, (0, 2, 1)), RET_DK, axis=2)

    c_rows = np.repeat(c_dec.reshape(RET_HEADS // 2, 2), RET_DK, axis=1)
    c_rows = np.broadcast_to(c_rows[:, :, None], (RET_HEADS // 2, 2 * RET_DK, RET_DV))
    as32 = lambda t: np.ascontiguousarray(t, dtype=f32)
    return as32(decay), as32(pair_lanes(q_dec)), as32(pair_lanes(k_dec)), as32(c_rows)


def _rec_mix(proj, s0, h0, c0, ret_g, conv_w, conv_b, w_rg, b_rg, w_ig, b_ig, lam):
    b, t, _ = proj.shape
    rows = RET_CHUNK if t % RET_CHUNK == 0 else t
    assert t % rows == 0 and rows % SUBLANES == 0 and (t == rows or rows == RET_CHUNK)
    n = t // rows
    nb = REC_SEQS if b % REC_SEQS == 0 else 1
    dec, qdec, kdec, cdec = _ret_tables(rows)
    hp = RET_HEADS // 2
    w = LRU_WIDTH
    per_b3 = lambda i, c: (i, 0, 0)
    outs = pl.pallas_call(
        functools.partial(_rec_mix_kernel, rows=rows),
        grid=(b // nb, n),
        in_specs=[
            pl.BlockSpec((nb, rows, REC_IN), lambda i, c: (i, c, 0)),
            pl.BlockSpec((nb, hp, 2 * RET_DK, RET_DV), lambda i, c: (i, 0, 0, 0)),
            pl.BlockSpec((nb, 1, w), per_b3),
            pl.BlockSpec((nb, CONV_WIDTH - 1, w), per_b3),
            _const_spec(dec.shape), _const_spec(qdec.shape), _const_spec(kdec.shape), _const_spec(cdec.shape),
            _const_spec((RET_HEADS, RET_DV)), _const_spec((CONV_WIDTH, w)), _const_spec((1, w)),
            _const_spec((LRU_BLOCKS, LRU_BLOCK, LRU_BLOCK)), _const_spec((1, w)),
            _const_spec((LRU_BLOCKS, LRU_BLOCK, LRU_BLOCK)), _const_spec((1, w)), _const_spec((1, w)),
        ],
        out_specs=[
            pl.BlockSpec((nb, rows, D_MODEL), lambda i, c: (i, c, 0)),
            pl.BlockSpec((nb, hp, 2 * RET_DK, RET_DV), lambda i, c: (i, 0, 0, 0)),
            pl.BlockSpec((nb, 1, w), per_b3),
            pl.BlockSpec((nb, CONV_WIDTH - 1, w), per_b3),
        ],
        out_shape=[
            jax.ShapeDtypeStruct((b, t, D_MODEL), BF16),
            jax.ShapeDtypeStruct((b, hp, 2 * RET_DK, RET_DV), F32),
            jax.ShapeDtypeStruct((b, 1, w), F32),
            jax.ShapeDtypeStruct((b, CONV_WIDTH - 1, w), F32),
        ],
        scratch_shapes=[
            pltpu.VMEM((nb, hp, 2 * RET_DK, RET_DV), F32),
            pltpu.VMEM((nb, 1, w), F32),
            pltpu.VMEM((nb, SUBLANES, w), F32),
        ],
        compiler_params=_cparams(("arbitrary", "arbitrary")),
        name="rec_mix",
    )(proj, s0.reshape(b, hp, 2 * RET_DK, RET_DV), h0.reshape(b, 1, w), c0,
      dec, qdec, kdec, cdec, ret_g, conv_w, conv_b.reshape(1, w),
      w_rg.astype(BF16), b_rg.reshape(1, w), w_ig.astype(BF16), b_ig.reshape(1, w), lam.reshape(1, w))
    mix, s_ret, h_last, c_new = outs
    return mix, s_ret.reshape(b, RET_HEADS, RET_DK, RET_DV), h_last.reshape(b, w), c_new


def _out_ffn_kernel(mix_ref, x_ref, wo_ref, g_ref, wg_ref, wu_ref, wd_ref, o_ref, act_ref):
    y = x_ref[...] + jnp.dot(mix_ref[...].astype(BF16), wo_ref[...], preferred_element_type=F32)
    h = _rmsnorm(y, g_ref[...]).astype(BF16)
    for c in range(wg_ref.shape[1] // FF_CHUNK):
        cols = slice(c * FF_CHUNK, (c + 1) * FF_CHUNK)
        gate = jnp.dot(h, wg_ref[:, cols], preferred_element_type=F32)
        up = jnp.dot(h, wu_ref[:, cols], preferred_element_type=F32)
        act_ref[:, cols] = (gate * jax.nn.sigmoid(gate) * up).astype(BF16)
    o_ref[...] = y + jnp.dot(act_ref[...], wd_ref[...], preferred_element_type=F32)


def _layer_spec(stacked, layer):
    shape = stacked.shape[1:]
    return pl.BlockSpec((None,) + shape, lambda *_: (layer,) + (0,) * len(shape), pipeline_mode=pl.Buffered(1))


def _out_ffn(mix2d, x2d, w_out, g, ffn_stacked, layer):
    m, d = x2d.shape
    tm = min(TOKEN_TILE, m)
    row = lambda i: (i, 0)
    return pl.pallas_call(
        _out_ffn_kernel,
        grid=(m // tm,),
        in_specs=[pl.BlockSpec((tm, d), row), pl.BlockSpec((tm, d), row), _const_spec(w_out.shape),
                  _const_spec((1, d))] + [_layer_spec(w, layer) for w in ffn_stacked],
        out_specs=pl.BlockSpec((tm, d), row),
        out_shape=jax.ShapeDtypeStruct((m, d), F32),
        scratch_shapes=[pltpu.VMEM((tm, ffn_stacked[0].shape[2]), BF16)],
        compiler_params=_cparams(("arbitrary",)),
        name="out_ffn",
    )(mix2d, x2d, w_out, g.reshape(1, d), *ffn_stacked)


def _pair_rms(a, g2, first):
    sq = a * a
    s_a = jnp.sum(jnp.where(first, sq, 0.0), axis=-1, keepdims=True)
    s_b = jnp.sum(jnp.where(first, 0.0, sq), axis=-1, keepdims=True)
    ms = jnp.where(first, s_a, s_b) * (1.0 / ATT_DH)
    return a * lax.rsqrt(ms + EPS) * g2


def _qkv_slabs(x, g_ref, w_ref, qg_ref, kg_ref):
    h = _rmsnorm(x, g_ref[...]).astype(BF16)
    first = lax.broadcasted_iota(jnp.int32, (x.shape[0], LANES), 1) < ATT_DH
    slabs = []
    for c in range(w_ref.shape[1] // MXU_COLS):
        part = jnp.dot(h, w_ref[:, c * MXU_COLS:(c + 1) * MXU_COLS], preferred_element_type=F32)
        slabs += [part[:, i * LANES:(i + 1) * LANES] for i in range(MXU_COLS // LANES)]
    q = [_pair_rms(slabs[p], qg_ref[...], first) for p in range(NPAIR)]
    k = [_pair_rms(slabs[NPAIR + p], kg_ref[...], first) for p in range(NPAIR)]
    v = slabs[2 * NPAIR:]
    return q, k, v


def _qkv_flat_kernel(x_ref, g_ref, w_ref, qg_ref, kg_ref, q_ref, k_ref, v_ref):
    q, k, v = _qkv_slabs(x_ref[...], g_ref, w_ref, qg_ref, kg_ref)
    q_ref[...] = jnp.concatenate(q, axis=1)
    k_ref[...] = jnp.concatenate(k, axis=1)
    v_ref[...] = jnp.concatenate(v, axis=1)


def _qkv_perm_kernel(x_ref, g_ref, w_ref, qg_ref, kg_ref, q_ref, k_ref, v_ref, xs_ref):
    for p in range(NPAIR):
        xs_ref[p] = x_ref[:, p * LANES:(p + 1) * LANES]
    for rs in _row_parts(x_ref.shape[0]):
        n = (rs.stop - rs.start) // MAX_DIL
        x = jnp.concatenate(
            [jnp.concatenate([xs_ref[p, pl.ds(rs.start + r, n, stride=MAX_DIL), :] for p in range(NPAIR)], axis=1)
             for r in range(MAX_DIL)], axis=0)
        q, k, v = _qkv_slabs(x, g_ref, w_ref, qg_ref, kg_ref)
        dst = slice(rs.start // MAX_DIL, rs.stop // MAX_DIL)
        for ref, slabs in ((q_ref, q), (k_ref, k), (v_ref, v)):
            for p in range(NPAIR):
                for r in range(MAX_DIL):
                    ref[p, r, dst] = slabs[p][r * n:(r + 1) * n]


def _head_gain(g):
    return jnp.tile(g.reshape(1, ATT_DH), (1, LANES // ATT_DH))


def _qkv_flat(x2d, g, w_bf, qg, kg):
    m, d = x2d.shape
    tm = min(TOKEN_TILE, m)
    row = lambda i: (i, 0)
    return pl.pallas_call(
        _qkv_flat_kernel,
        grid=(m // tm,),
        in_specs=[pl.BlockSpec((tm, d), row), _const_spec((1, d)), _const_spec(w_bf.shape),
                  _const_spec((1, LANES)), _const_spec((1, LANES))],
        out_specs=[pl.BlockSpec((tm, d), row)] * 3,
        out_shape=[jax.ShapeDtypeStruct((m, d), F32)] * 3,
        compiler_params=_cparams(("arbitrary",)),
        name="qkv_flat",
    )(x2d, g.reshape(1, d), w_bf, _head_gain(qg), _head_gain(kg))


def _qkv_perm(x, g, w_bf, qg, kg):
    b, t, d = x.shape
    ns = t // SUPER
    tm = WIDE_TILE
    nt = SUPER // tm
    n = tm // MAX_DIL
    out_spec = pl.BlockSpec((None, NPAIR, None, MAX_DIL, None, n, LANES), lambda i, s, j: (i, 0, s, 0, j, 0, 0))
    out_shape = jax.ShapeDtypeStruct((b, NPAIR, ns, MAX_DIL, nt, n, LANES), F32)
    q, k, v = pl.pallas_call(
        _qkv_perm_kernel,
        grid=(b, ns, nt),
        in_specs=[pl.BlockSpec((None, tm, d), lambda i, s, j: (i, s * nt + j, 0)),
                  _const_spec((1, d)), _const_spec(w_bf.shape), _const_spec((1, LANES)), _const_spec((1, LANES))],
        out_specs=[out_spec] * 3,
        out_shape=[out_shape] * 3,
        scratch_shapes=[pltpu.VMEM((NPAIR, tm, LANES), F32)],
        compiler_params=_cparams(("arbitrary",) * 3),
        name="qkv_perm",
    )(x, g.reshape(1, d), w_bf, _head_gain(qg), _head_gain(kg))
    return tuple(a.reshape(b, NPAIR, t, LANES) for a in (q, k, v))


def _alibi_slopes():
    return np.power(np.float32(2.0), np.float32(-8.0) * np.arange(1, ATT_HEADS + 1, dtype=np.float32) / ATT_HEADS)


PREV_SLOTS = 32
ATT_UNROLL = 16
LOG2E = 1.4426950408889634


def _band_bias():
    i = np.arange(ATT_BLOCK)
    c = np.arange(2 * ATT_BLOCK)
    q_idx = {1: MAX_DIL * (i % 8) + i // 8,
             4: 4 * (i % 32) + i // 32,
             16: i}
    k_idx = {1: (MAX_DIL * (c % 8) + c // 16, (c // 8) % 2 == 0),
             4: (4 * (c % 32) + c // 64, (c // 32) % 2 == 0),
             16: (c % ATT_BLOCK, c < ATT_BLOCK)}
    slopes = _alibi_slopes().reshape(ATT_HEADS // 2, 2, 1, 1)
    neg = np.float32(NEG)
    out = []
    for (w, d) in PATTERNS:
        span = w // d
        km, is_prev = k_idx[d]
        diff = q_idx[d][:, None] - (km - np.where(is_prev, ATT_BLOCK, 0))[None, :]
        valid = (diff >= 0) & (diff <= span)
        bias = -slopes * (d * diff).astype(np.float32)[None, None]
        bias = bias * np.float32(LOG2E)
        variants = [np.where(valid, bias, neg), np.where(valid & ~is_prev[None, :], bias, neg)]
        out.append(np.stack([v.reshape(ATT_HEADS // 2, 2 * ATT_BLOCK, 2 * ATT_BLOCK) for v in variants]))
    return np.stack(out).astype(np.float32)


def _attend_pair(q, k, v, bias, first):
    qs = jnp.concatenate([jnp.where(first, q, 0.0), jnp.where(first, 0.0, q)], axis=0).astype(BF16)
    s = lax.dot_general(qs, k.astype(BF16), (((1,), (1,)), ((), ())), preferred_element_type=F32) + bias
    m = jnp.max(s, axis=-1, keepdims=True)
    e = jnp.exp2(s - m)
    den = jnp.sum(e, axis=-1, keepdims=True)
    o = jnp.dot(e.astype(BF16), v.astype(BF16), preferred_element_type=F32)
    halves = lambda a: jnp.where(first, a[:ATT_BLOCK], a[ATT_BLOCK:])
    return halves(o), halves(m), halves(den)


def _attn_prompt_kernel(q_ref, kc_ref, kp_ref, vc_ref, vp_ref, bias_ref, o_ref, ko_ref, vo_ref,
                        kext, vext, oacc, macc, dacc, nat_ref):
    sb = pl.program_id(2)
    first_sb = (sb == 0).astype(jnp.int32)
    first = lax.broadcasted_iota(jnp.int32, (ATT_BLOCK, LANES), 1) < ATT_DH
    scale = (ATT_DH ** -0.5) * LOG2E
    ps = PREV_SLOTS

    for src_p, src_c, ext in ((kp_ref, kc_ref, kext), (vp_ref, vc_ref, vext)):
        for r in range(MAX_DIL):
            ext[r, 0:ps, :] = src_p[r * ATT_BLOCK + ATT_BLOCK - ps:(r + 1) * ATT_BLOCK, :]
            ext[r, ps:ps + ATT_BLOCK, :] = src_c[r * ATT_BLOCK:(r + 1) * ATT_BLOCK, :]

    def attend_store(p, variant, q_chunks, k, v):
        n = ATT_BLOCK // len(q_chunks)
        q = jnp.concatenate([q_ref[sl, :] for sl in q_chunks], axis=0) * scale
        o, m, den = _attend_pair(q, k, v, bias_ref[p, variant], first)
        for ci, sl in enumerate(q_chunks):
            oacc[p, sl, :] = o[ci * n:(ci + 1) * n]
            macc[p, sl, :] = m[ci * n:(ci + 1) * n]
            dacc[p, sl, :] = den[ci * n:(ci + 1) * n]

    n0 = ATT_BLOCK // MAX_DIL

    def block0(bi):
        base = pl.multiple_of(bi * n0, n0)
        ksl = pl.ds(base + ps - n0, 2 * n0)
        k = jnp.concatenate([kext[r, ksl, :] for r in range(MAX_DIL)], axis=0)
        v = jnp.concatenate([vext[r, ksl, :] for r in range(MAX_DIL)], axis=0)
        attend_store(0, first_sb * jnp.asarray(bi == 0, jnp.int32),
                     [pl.ds(r * ATT_BLOCK + base, n0) for r in range(MAX_DIL)], k, v)

    d1 = PATTERNS[1][1]
    nc1 = MAX_DIL // d1
    n1 = ATT_BLOCK // nc1

    def block1(idx):
        r4, mb = idx // (ATT_BLOCK // n1), idx % (ATT_BLOCK // n1)
        base = pl.multiple_of(mb * n1, n1)
        ksl = pl.ds(base + ps - n1, 2 * n1)
        k = jnp.concatenate([kext[r4 + d1 * c, ksl, :] for c in range(nc1)], axis=0)
        v = jnp.concatenate([vext[r4 + d1 * c, ksl, :] for c in range(nc1)], axis=0)
        attend_store(1, first_sb * jnp.asarray(mb == 0, jnp.int32),
                     [pl.ds(pl.multiple_of((r4 + d1 * c) * ATT_BLOCK + base, n1), n1) for c in range(nc1)], k, v)

    def block2(r):
        sl = pl.ds(pl.multiple_of(r * ATT_BLOCK, ATT_BLOCK), ATT_BLOCK)
        k = jnp.concatenate([kp_ref[sl, :], kc_ref[sl, :]], axis=0)
        v = jnp.concatenate([vp_ref[sl, :], vc_ref[sl, :]], axis=0)
        attend_store(2, first_sb, [sl], k, v)

    for block in (block0, block1, block2):
        def body(it, carry, block=block):
            for u in range(ATT_UNROLL):
                block(it * ATT_UNROLL + u)
            return carry

        lax.fori_loop(0, MAX_DIL // ATT_UNROLL, body, 0)

    def comb(r, carry):
        sl = pl.ds(pl.multiple_of(r * ATT_BLOCK, ATT_BLOCK), ATT_BLOCK)
        m0, m1, m2 = macc[0, sl, :], macc[1, sl, :], macc[2, sl, :]
        mx = jnp.maximum(jnp.maximum(m0, m1), m2)
        w0, w1, w2 = jnp.exp2(m0 - mx), jnp.exp2(m1 - mx), jnp.exp2(m2 - mx)
        num = w0 * oacc[0, sl, :] + w1 * oacc[1, sl, :] + w2 * oacc[2, sl, :]
        den = w0 * dacc[0, sl, :] + w1 * dacc[1, sl, :] + w2 * dacc[2, sl, :]
        o_ref[pl.ds(r, ATT_BLOCK, stride=MAX_DIL), :] = num / den
        return carry

    lax.fori_loop(0, SUPER // ATT_BLOCK, comb, 0)

    @pl.when(sb == pl.num_programs(2) - 1)
    def _():
        for src, dst in ((kc_ref, ko_ref), (vc_ref, vo_ref)):
            for r in range(MAX_DIL):
                nat_ref[pl.ds(r, ATT_BLOCK, stride=MAX_DIL), :] = src[r * ATT_BLOCK:(r + 1) * ATT_BLOCK, :]
            dst[...] = jnp.transpose(nat_ref[...])


def _attn_prompt(q, k, v):
    b, npair, t, _ = q.shape
    ns = t // SUPER
    bias = _band_bias()
    cur = pl.BlockSpec((None, None, SUPER, LANES), lambda i, p, s: (i, p, s, 0))
    prev = pl.BlockSpec((None, None, SUPER, LANES), lambda i, p, s: (i, p, jnp.maximum(s - 1, 0), 0))
    cache = pl.BlockSpec((None, LANES, SUPER), lambda i, p, s: (i, p, 0))
    cache_shape = jax.ShapeDtypeStruct((b, npair * LANES, SUPER), F32)
    return pl.pallas_call(
        _attn_prompt_kernel,
        grid=(b, npair, ns),
        in_specs=[cur, cur, prev, cur, prev,
                  pl.BlockSpec((len(PATTERNS), 2, None, 2 * ATT_BLOCK, 2 * ATT_BLOCK),
                               lambda i, p, s: (0, 0, p, 0, 0))],
        out_specs=[pl.BlockSpec((None, SUPER, LANES), lambda i, p, s: (i, s, p)), cache, cache],
        out_shape=[jax.ShapeDtypeStruct((b, t, npair * LANES), F32), cache_shape, cache_shape],
        scratch_shapes=[pltpu.VMEM((MAX_DIL, PREV_SLOTS + ATT_BLOCK, LANES), F32),
                        pltpu.VMEM((MAX_DIL, PREV_SLOTS + ATT_BLOCK, LANES), F32),
                        pltpu.VMEM((len(PATTERNS), SUPER, LANES), F32),
                        pltpu.VMEM((len(PATTERNS), SUPER, LANES), F32),
                        pltpu.VMEM((len(PATTERNS), SUPER, LANES), F32),
                        pltpu.VMEM((SUPER, LANES), F32)],
        compiler_params=_cparams(("arbitrary",) * 3),
        name="attn_prompt",
    )(q, k, k, v, v, bias)


QUAD = 4
NEW_PAD = 128


def _sample_bias(s_len, n_past):
    s = np.arange(s_len)
    col = np.arange(n_past + NEW_PAD)
    dist = (n_past + s)[:, None] - col[None, :]
    real = (col < n_past + s_len)[None, :]
    slopes = _alibi_slopes().reshape(ATT_HEADS // QUAD, QUAD, 1, 1)
    out = []
    for (w, d) in PATTERNS:
        valid = (dist >= 0) & (dist % d == 0) & (dist <= w) & real
        bias = np.where(valid[None, None], -slopes * dist.astype(np.float32)[None, None], np.float32(NEG))
        out.append(bias.reshape(ATT_HEADS // QUAD, QUAD * s_len, n_past + NEW_PAD))
    return np.stack(out).astype(np.float32)


def _attn_sample_kernel(q_ref, kn_ref, vn_ref, ck_ref, cv_ref, bias_ref, o_ref, *, s_len):
    n_past = ck_ref.shape[1]
    width = QUAD * ATT_DH
    rows = QUAD * s_len
    own = (lax.broadcasted_iota(jnp.int32, (rows, width), 0) // s_len
           == lax.broadcasted_iota(jnp.int32, (rows, width), 1) // ATT_DH)
    nt = (((1,), (1,)), ((), ()))
    for quad in range(q_ref.shape[1] // width):
        cols = slice(quad * width, (quad + 1) * width)
        q = q_ref[:, cols] * (ATT_DH ** -0.5)
        qm = jnp.where(own, jnp.concatenate([q] * QUAD, axis=0), 0.0).astype(BF16)
        kn = _pad_rows(kn_ref[:, cols], NEW_PAD).astype(BF16)
        vn = _pad_rows(vn_ref[:, cols], NEW_PAD).astype(BF16)
        s_c = jnp.dot(qm, ck_ref[cols, :].astype(BF16), preferred_element_type=F32)
        s_n = lax.dot_general(qm, kn, nt, preferred_element_type=F32)
        es_c, es_n, dens, lses = [], [], [], []
        for p in range(len(PATTERNS)):
            sc = s_c + bias_ref[p, quad, :, 0:n_past]
            sn = s_n + bias_ref[p, quad, :, n_past:n_past + NEW_PAD]
            m = jnp.maximum(jnp.max(sc, axis=-1, keepdims=True), jnp.max(sn, axis=-1, keepdims=True))
            ec, en = jnp.exp(sc - m), jnp.exp(sn - m)
            den = jnp.sum(ec, axis=-1, keepdims=True) + jnp.sum(en, axis=-1, keepdims=True)
            es_c.append(ec)
            es_n.append(en)
            dens.append(den)
            lses.append(m + jnp.log(den))
        mx = jnp.maximum(jnp.maximum(lses[0], lses[1]), lses[2])
        ws = [jnp.exp(l - mx) for l in lses]
        wsum = ws[0] + ws[1] + ws[2]
        coef = [w / (wsum * d) for w, d in zip(ws, dens)]
        c_c = coef[0] * es_c[0] + coef[1] * es_c[1] + coef[2] * es_c[2]
        c_n = coef[0] * es_n[0] + coef[1] * es_n[1] + coef[2] * es_n[2]
        o = (lax.dot_general(c_c.astype(BF16), cv_ref[cols, :].astype(BF16), nt, preferred_element_type=F32)
             + jnp.dot(c_n.astype(BF16), vn, preferred_element_type=F32))
        o = jnp.where(own, o, 0.0)
        acc = o[0:s_len]
        for hh in range(1, QUAD):
            acc = acc + o[hh * s_len:(hh + 1) * s_len]
        o_ref[:, cols] = acc.astype(o_ref.dtype)


def _attn_sample(q, kn, vn, ck, cv):
    b, s_len, d = q.shape
    n_past = ck.shape[2]
    bias = _sample_bias(s_len, n_past)
    new = pl.BlockSpec((None, s_len, d), lambda i: (i, 0, 0))
    old = pl.BlockSpec((None, d, n_past), lambda i: (i, 0, 0))
    return pl.pallas_call(
        functools.partial(_attn_sample_kernel, s_len=s_len),
        grid=(b,),
        in_specs=[new, new, new, old, old, _const_spec(bias.shape)],
        out_specs=new,
        out_shape=jax.ShapeDtypeStruct((b, s_len, d), BF16),
        compiler_params=_cparams(("arbitrary",)),
        name="attn_sample",
    )(q, kn, vn, ck, cv, bias)


def kernel(x_prompt, x_sample, state_ret, state_lru, state_conv, cache_k, cache_v, norm_mix, norm_ffn, w_in_rec, w_out_rec, ret_norm_g, conv_w, conv_b, w_rgate, b_rgate, w_igate, b_igate, lru_lambda, w_in_att, w_out_att, q_norm_g, k_norm_g, w_ffn_gate, w_ffn_up, w_ffn_down):
    bp, tp, d = x_prompt.shape
    bs, ts, _ = x_sample.shape
    depth = norm_mix.shape[0]
    assert tp % SUPER == 0 and d == D_MODEL
    yp, ys = x_prompt, x_sample
    ffn_stacked = (w_ffn_gate.astype(BF16), w_ffn_up.astype(BF16), w_ffn_down.astype(BF16))
    ret_p, lru_p, conv_p, kp, vp = [], [], [], [], []
    ret_s, lru_s, conv_s, ksl, vsl = [], [], [], [], []
    for l in range(depth):
        i = l // 2
        if l % 2 == 0:
            ffn = (w_out_rec[i].astype(BF16), norm_ffn[l], ffn_stacked, l)
            w_in = w_in_rec[i].astype(BF16)
            rec_prm = (ret_norm_g[i], conv_w[i], conv_b[i], w_rgate[i], b_rgate[i], w_igate[i], b_igate[i],
                       lru_lambda[i])
            proj = _norm_proj(yp.reshape(bp * tp, d), norm_mix[l], w_in).reshape(bp, tp, REC_IN)
            mix, sr, sl, sc = _rec_mix(proj, jnp.zeros((bp, RET_HEADS, RET_DK, RET_DV), F32),
                                       jnp.zeros((bp, LRU_WIDTH), F32),
                                       jnp.zeros((bp, CONV_WIDTH - 1, LRU_WIDTH), F32), *rec_prm)
            yp = _out_ffn(mix.reshape(bp * tp, d), yp.reshape(bp * tp, d), *ffn).reshape(bp, tp, d)
            ret_p.append(sr); lru_p.append(sl); conv_p.append(sc)
            proj = _norm_proj(ys.reshape(bs * ts, d), norm_mix[l], w_in).reshape(bs, ts, REC_IN)
            mix, sr, sl, sc = _rec_mix(proj, state_ret[i], state_lru[i], state_conv[i], *rec_prm)
            ys = _out_ffn(mix.reshape(bs * ts, d), ys.reshape(bs * ts, d), *ffn).reshape(bs, ts, d)
            ret_s.append(sr); lru_s.append(sl); conv_s.append(sc)
        else:
            ffn = (w_out_att[i].astype(BF16), norm_ffn[l], ffn_stacked, l)
            w_in = w_in_att[i].astype(BF16)
            q, k, v = _qkv_perm(yp, norm_mix[l], w_in, q_norm_g[i], k_norm_g[i])
            o, kc, vc = _attn_prompt(q, k, v)
            yp = _out_ffn(o.reshape(bp * tp, d), yp.reshape(bp * tp, d), *ffn).reshape(bp, tp, d)
            from_slabs = lambda c: jnp.transpose(c.reshape(bp, ATT_HEADS, ATT_DH, SUPER), (0, 3, 1, 2))
            kp.append(from_slabs(kc)); vp.append(from_slabs(vc))
            qs, kn, vn = _qkv_flat(ys.reshape(bs * ts, d), norm_mix[l], w_in, q_norm_g[i], k_norm_g[i])
            n_past = cache_k.shape[2]
            to_slabs = lambda c: jnp.transpose(c, (0, 2, 3, 1)).reshape(bs, d, n_past)
            o = _attn_sample(qs.reshape(bs, ts, d), kn.reshape(bs, ts, d), vn.reshape(bs, ts, d),
                             to_slabs(cache_k[i]), to_slabs(cache_v[i]))
            ys = _out_ffn(o.reshape(bs * ts, d), ys.reshape(bs * ts, d), *ffn).reshape(bs, ts, d)
            ksl.append(kn.reshape(bs, ts, ATT_HEADS, ATT_DH)); vsl.append(vn.reshape(bs, ts, ATT_HEADS, ATT_DH))
    return (yp, ys,
            jnp.stack(ret_p), jnp.stack(lru_p), jnp.stack(conv_p), jnp.stack(kp), jnp.stack(vp),
            jnp.stack(ret_s), jnp.stack(lru_s), jnp.stack(conv_s), jnp.stack(ksl), jnp.stack(vsl))
```

```python
import functools

import jax
import numpy as np
import jax.numpy as jnp
from jax import lax
from jax.experimental import pallas as pl
from jax.experimental.pallas import tpu as pltpu

F32 = jnp.float32
BF16 = jnp.bfloat16

EPS = 1e-6
D_MODEL = 1024
RET_HEADS = 4
RET_DK = 64
RET_DV = 128
RET_CHUNK = 128
LRU_WIDTH = 512
LRU_BLOCKS = 4
LRU_BLOCK = LRU_WIDTH // LRU_BLOCKS
CONV_WIDTH = 4
LRU_C = 8.0
GELU_C = 0.7978845608028654
REC_IN = 2 * RET_HEADS * RET_DK + 2 * RET_HEADS * RET_DV + 2 * LRU_WIDTH
ATT_HEADS = 16
ATT_DH = 64
PATTERNS = ((128, 1), (512, 4), (2048, 16))
ATT_BLOCK = 128
SUPER = 2048
MAX_DIL = 16
NEG = -1e30

LANES = 128
SUBLANES = 8
MXU_COLS = 256
NPAIR = D_MODEL // LANES
VMEM_LIMIT = 56 * 1024 * 1024

FF_CHUNK = 256
TOKEN_TILE = 512
WIDE_TILE = 1024
STREAM_ROWS = 512
REC_SEQS = 4


def _cparams(sem):
    return pltpu.CompilerParams(dimension_semantics=sem, vmem_limit_bytes=VMEM_LIMIT)


def _const_spec(shape):
    nd = len(shape)
    return pl.BlockSpec(shape, lambda *_: (0,) * nd, pipeline_mode=pl.Buffered(1))


def _rmsnorm(x, g):
    return x * lax.rsqrt(jnp.mean(x * x, axis=-1, keepdims=True) + EPS) * g


def _row_parts(rows):
    n = max(1, rows // STREAM_ROWS)
    return [slice(i * (rows // n), (i + 1) * (rows // n)) for i in range(n)]


def _norm_proj_kernel(x_ref, g_ref, w_ref, o_ref):
    for rs in _row_parts(x_ref.shape[0]):
        h = _rmsnorm(x_ref[rs, :], g_ref[...]).astype(BF16)
        o_ref[rs, :] = jnp.dot(h, w_ref[...], preferred_element_type=F32)


def _norm_proj(x2d, g, w_bf):
    m, d = x2d.shape
    n = w_bf.shape[1]
    tm = min(WIDE_TILE, m)
    return pl.pallas_call(
        _norm_proj_kernel,
        grid=(m // tm,),
        in_specs=[pl.BlockSpec((tm, d), lambda i: (i, 0)), _const_spec((1, d)), _const_spec((d, n))],
        out_specs=pl.BlockSpec((tm, n), lambda i: (i, 0)),
        out_shape=jax.ShapeDtypeStruct((m, n), F32),
        compiler_params=_cparams(("arbitrary",)),
        name="norm_proj",
    )(x2d, g.reshape(1, d), w_bf)


def _linear_scan(a, b, h0):
    rows, w = a.shape
    a = a.reshape(rows // SUBLANES, SUBLANES, w)
    b = b.reshape(rows // SUBLANES, SUBLANES, w)
    sub = lax.broadcasted_iota(jnp.int32, a.shape, 1)
    s = 1
    while s < SUBLANES:
        keep = sub >= s
        b = jnp.where(keep, a * pltpu.roll(b, s, axis=1), 0.0) + b
        a = jnp.where(keep, a * pltpu.roll(a, s, axis=1), a)
        s *= 2
    groups = []
    carry = h0
    for i in range(rows // SUBLANES):
        groups.append(b[i] + a[i] * carry)
        carry = groups[-1][SUBLANES - 1:SUBLANES]
    return jnp.concatenate(groups, axis=0)


def _pad_rows(x, rows):
    if x.shape[0] == rows:
        return x
    return jnp.concatenate([x, jnp.zeros((rows - x.shape[0], x.shape[1]), x.dtype)], axis=0)


def _rec_mix_kernel(proj_ref, s0_ref, h0_ref, c0_ref, dec_ref, qdec_ref, kdec_ref, cdec_ref,
                    retg_ref, convw_ref, convb_ref, wrg_ref, brg_ref, wig_ref, big_ref, lam_ref,
                    mix_ref, sret_ref, hlast_ref, cnew_ref,
                    s_scr, h_scr, xe_scr, *, rows):
    cp = RET_CHUNK
    hk = RET_HEADS * RET_DK
    v_off, g_off = 2 * hk, 2 * hk + RET_HEADS * RET_DV
    xl_off = g_off + RET_HEADS * RET_DV
    yl_off = xl_off + LRU_WIDTH

    @pl.when(pl.program_id(1) == 0)
    def _():
        s_scr[...] = s0_ref[...]
        h_scr[...] = h0_ref[...]
        xe_scr[...] = jnp.zeros(xe_scr.shape, F32)
        xe_scr[:, SUBLANES - (CONV_WIDTH - 1):SUBLANES, :] = c0_ref[...]

    lane = lax.broadcasted_iota(jnp.int32, (cp, LANES), 1)
    first = lane < RET_DK
    srow_first = lax.broadcasted_iota(jnp.int32, (2 * RET_DK, RET_DV), 0) < RET_DK
    nl = -lam_ref[...]
    neg_c_softplus = (-LRU_C) * (jnp.maximum(nl, 0.0) + jnp.log1p(jnp.exp(-jnp.abs(nl))))

    for bi in range(proj_ref.shape[0]):
        for p in range(RET_HEADS // 2):
            q2 = _pad_rows(proj_ref[bi, :, p * LANES:(p + 1) * LANES], cp)
            k2 = _pad_rows(proj_ref[bi, :, hk + p * LANES:hk + (p + 1) * LANES], cp) * (RET_DK ** -0.5)
            kb = k2.astype(BF16)
            kdt = jnp.transpose(k2 * kdec_ref[p]).astype(BF16)
            s2 = s_scr[bi, p]
            s2b = s2.astype(BF16)
            new_s = []
            for hh in range(2):
                h = 2 * p + hh
                sel = first if hh == 0 else jnp.logical_not(first)
                qh = jnp.where(sel, q2, 0.0)
                v = _pad_rows(proj_ref[bi, :, v_off + h * RET_DV:v_off + (h + 1) * RET_DV], cp)
                vb = v.astype(BF16)
                att = lax.dot_general(qh.astype(BF16), kb, (((1,), (1,)), ((), ())),
                                      preferred_element_type=F32) * dec_ref[h]
                o = (jnp.dot(att.astype(BF16), vb, preferred_element_type=F32)
                     + jnp.dot((qh * qdec_ref[p]).astype(BF16), s2b, preferred_element_type=F32))
                new_s.append(jnp.dot(kdt, vb, preferred_element_type=F32))
                o = o[:rows]
                o = o * lax.rsqrt(jnp.mean(o * o, axis=-1, keepdims=True) + EPS) * retg_ref[h:h + 1, :]
                gate = proj_ref[bi, :, g_off + h * RET_DV:g_off + (h + 1) * RET_DV]
                mix_ref[bi, :, h * RET_DV:(h + 1) * RET_DV] = (
                    o * (gate * jax.nn.sigmoid(gate))).astype(mix_ref.dtype)
            s_scr[bi, p] = s2 * cdec_ref[p] + jnp.where(srow_first, new_s[0], new_s[1])

        x = proj_ref[bi, :, xl_off:xl_off + LRU_WIDTH]
        xe = jnp.concatenate([xe_scr[bi], x], axis=0)
        xc = convb_ref[...]
        for j in range(CONV_WIDTH):
            back = CONV_WIDTH - 1 - j
            xj = x if back == 0 else pltpu.roll(xe, back, axis=0)[SUBLANES:]
            xc = xc + xj * convw_ref[j:j + 1, :]
        xe_scr[bi] = xe[rows:]
        cnew_ref[bi] = xe_scr[bi, pl.ds(SUBLANES - (CONV_WIDTH - 1), CONV_WIDTH - 1), :]

        xcb = _pad_rows(xc, max(rows, 2 * SUBLANES)).astype(BF16)
        r_parts, i_parts = [], []
        for n in range(LRU_BLOCKS):
            xb = xcb[:, n * LRU_BLOCK:(n + 1) * LRU_BLOCK]
            r_parts.append(jnp.dot(xb, wrg_ref[n], preferred_element_type=F32)[:rows])
            i_parts.append(jnp.dot(xb, wig_ref[n], preferred_element_type=F32)[:rows])
        r = jax.nn.sigmoid(jnp.concatenate(r_parts, axis=1) + brg_ref[...])
        ig = jax.nn.sigmoid(jnp.concatenate(i_parts, axis=1) + big_ref[...])
        a = jnp.exp(r * neg_c_softplus)
        one_m = 1.0 - a * a
        root = jnp.where(one_m > 0.0, one_m * lax.rsqrt(one_m), 0.0)
        hseq = _linear_scan(a, root * (ig * xc), h_scr[bi])
        h_scr[bi] = hseq[rows - 1:rows, :]
        hlast_ref[bi] = hseq[rows - 1:rows, :]
        yl = proj_ref[bi, :, yl_off:yl_off + LRU_WIDTH]
        half = 0.5 * yl
        gelu = half + half * jnp.tanh(yl * (GELU_C + (0.044715 * GELU_C) * (yl * yl)))
        mix_ref[bi, :, RET_HEADS * RET_DV:] = (hseq * gelu).astype(mix_ref.dtype)
    sret_ref[...] = s_scr[...]


def _ret_tables(t_eff):
    cp = RET_CHUNK
    f32 = np.float32
    log_g = np.log1p(-np.power(f32(2.0), f32(-5.0) - np.arange(RET_HEADS, dtype=f32)))
    idx = np.arange(cp, dtype=f32)
    live = idx < t_eff
    diff = idx[:, None] - idx[None, :]
    ok = (diff >= 0) & live[:, None] & live[None, :]
    decay = np.where(ok, np.exp(log_g[:, None, None] * np.maximum(diff, f32(0.0))), f32(0.0))
    q_dec = np.where(live, np.exp(log_g[:, None] * (idx + f32(1.0))), f32(0.0))
    k_dec = np.where(live, np.exp(log_g[:, None] * (f32(t_eff - 1.0) - idx)), f32(0.0))
    c_dec = np.exp(log_g * f32(t_eff))

    def pair_lanes(t):
        t = t.reshape(RET_HEADS // 2, 2, cp)
        return np.repeat(np.transpose(t, (0, 2, 1)), RET_DK, axis=2)

    c_rows = np.repeat(c_dec.reshape(RET_HEADS // 2, 2), RET_DK, axis=1)
    c_rows = np.broadcast_to(c_rows[:, :, None], (RET_HEADS // 2, 2 * RET_DK, RET_DV))
    as32 = lambda t: np.ascontiguousarray(t, dtype=f32)
    return as32(decay), as32(pair_lanes(q_dec)), as32(pair_lanes(k_dec)), as32(c_rows)


def _rec_mix(proj, s0, h0, c0, ret_g, conv_w, conv_b, w_rg, b_rg, w_ig, b_ig, lam):
    b, t, _ = proj.shape
    rows = RET_CHUNK if t % RET_CHUNK == 0 else t
    assert t % rows == 0 and rows % SUBLANES == 0 and (t == rows or rows == RET_CHUNK)
    n = t // rows
    nb = REC_SEQS if b % REC_SEQS == 0 else 1
    dec, qdec, kdec, cdec = _ret_tables(rows)
    hp = RET_HEADS // 2
    w = LRU_WIDTH
    per_b3 = lambda i, c: (i, 0, 0)
    outs = pl.pallas_call(
        functools.partial(_rec_mix_kernel, rows=rows),
        grid=(b // nb, n),
        in_specs=[
            pl.BlockSpec((nb, rows, REC_IN), lambda i, c: (i, c, 0)),
            pl.BlockSpec((nb, hp, 2 * RET_DK, RET_DV), lambda i, c: (i, 0, 0, 0)),
            pl.BlockSpec((nb, 1, w), per_b3),
            pl.BlockSpec((nb, CONV_WIDTH - 1, w), per_b3),
            _const_spec(dec.shape), _const_spec(qdec.shape), _const_spec(kdec.shape), _const_spec(cdec.shape),
            _const_spec((RET_HEADS, RET_DV)), _const_spec((CONV_WIDTH, w)), _const_spec((1, w)),
            _const_spec((LRU_BLOCKS, LRU_BLOCK, LRU_BLOCK)), _const_spec((1, w)),
            _const_spec((LRU_BLOCKS, LRU_BLOCK, LRU_BLOCK)), _const_spec((1, w)), _const_spec((1, w)),
        ],
        out_specs=[
            pl.BlockSpec((nb, rows, D_MODEL), lambda i, c: (i, c, 0)),
            pl.BlockSpec((nb, hp, 2 * RET_DK, RET_DV), lambda i, c: (i, 0, 0, 0)),
            pl.BlockSpec((nb, 1, w), per_b3),
            pl.BlockSpec((nb, CONV_WIDTH - 1, w), per_b3),
        ],
        out_shape=[
            jax.ShapeDtypeStruct((b, t, D_MODEL), BF16),
            jax.ShapeDtypeStruct((b, hp, 2 * RET_DK, RET_DV), F32),
            jax.ShapeDtypeStruct((b, 1, w), F32),
            jax.ShapeDtypeStruct((b, CONV_WIDTH - 1, w), F32),
        ],
        scratch_shapes=[
            pltpu.VMEM((nb, hp, 2 * RET_DK, RET_DV), F32),
            pltpu.VMEM((nb, 1, w), F32),
            pltpu.VMEM((nb, SUBLANES, w), F32),
        ],
        compiler_params=_cparams(("arbitrary", "arbitrary")),
        name="rec_mix",
    )(proj, s0.reshape(b, hp, 2 * RET_DK, RET_DV), h0.reshape(b, 1, w), c0,
      dec, qdec, kdec, cdec, ret_g, conv_w, conv_b.reshape(1, w),
      w_rg.astype(BF16), b_rg.reshape(1, w), w_ig.astype(BF16), b_ig.reshape(1, w), lam.reshape(1, w))
    mix, s_ret, h_last, c_new = outs
    return mix, s_ret.reshape(b, RET_HEADS, RET_DK, RET_DV), h_last.reshape(b, w), c_new


def _out_ffn_kernel(mix_ref, x_ref, wo_ref, g_ref, wg_ref, wu_ref, wd_ref, o_ref, act_ref):
    for rs in _row_parts(x_ref.shape[0]):
        y = x_ref[rs, :] + jnp.dot(mix_ref[rs, :].astype(BF16), wo_ref[...], preferred_element_type=F32)
        h = _rmsnorm(y, g_ref[...]).astype(BF16)
        for c in range(wg_ref.shape[1] // FF_CHUNK):
            cols = slice(c * FF_CHUNK, (c + 1) * FF_CHUNK)
            gate = jnp.dot(h, wg_ref[:, cols], preferred_element_type=F32)
            up = jnp.dot(h, wu_ref[:, cols], preferred_element_type=F32)
            act_ref[rs, cols] = (gate * jax.nn.sigmoid(gate) * up).astype(BF16)
        o_ref[rs, :] = y + jnp.dot(act_ref[rs, :], wd_ref[...], preferred_element_type=F32)


def _layer_spec(stacked, layer):
    shape = stacked.shape[1:]
    return pl.BlockSpec((None,) + shape, lambda *_: (layer,) + (0,) * len(shape), pipeline_mode=pl.Buffered(1))


def _out_ffn(mix2d, x2d, w_out, g, ffn_stacked, layer):
    m, d = x2d.shape
    tm = min(WIDE_TILE, m)
    row = lambda i: (i, 0)
    return pl.pallas_call(
        _out_ffn_kernel,
        grid=(m // tm,),
        in_specs=[pl.BlockSpec((tm, d), row), pl.BlockSpec((tm, d), row), _const_spec(w_out.shape),
                  _const_spec((1, d))] + [_layer_spec(w, layer) for w in ffn_stacked],
        out_specs=pl.BlockSpec((tm, d), row),
        out_shape=jax.ShapeDtypeStruct((m, d), F32),
        scratch_shapes=[pltpu.VMEM((tm, ffn_stacked[0].shape[2]), BF16)],
        compiler_params=_cparams(("arbitrary",)),
        name="out_ffn",
    )(mix2d, x2d, w_out, g.reshape(1, d), *ffn_stacked)


def _pair_rms(a, g2, first):
    sq = a * a
    s_a = jnp.sum(jnp.where(first, sq, 0.0), axis=-1, keepdims=True)
    s_b = jnp.sum(jnp.where(first, 0.0, sq), axis=-1, keepdims=True)
    ms = jnp.where(first, s_a, s_b) * (1.0 / ATT_DH)
    return a * lax.rsqrt(ms + EPS) * g2


def _qkv_slabs(x, g_ref, w_ref, qg_ref, kg_ref):
    h = _rmsnorm(x, g_ref[...]).astype(BF16)
    first = lax.broadcasted_iota(jnp.int32, (x.shape[0], LANES), 1) < ATT_DH
    slabs = []
    for c in range(w_ref.shape[1] // MXU_COLS):
        part = jnp.dot(h, w_ref[:, c * MXU_COLS:(c + 1) * MXU_COLS], preferred_element_type=F32)
        slabs += [part[:, i * LANES:(i + 1) * LANES] for i in range(MXU_COLS // LANES)]
    q = [_pair_rms(slabs[p], qg_ref[...], first) for p in range(NPAIR)]
    k = [_pair_rms(slabs[NPAIR + p], kg_ref[...], first) for p in range(NPAIR)]
    v = slabs[2 * NPAIR:]
    return q, k, v


def _qkv_flat_kernel(x_ref, g_ref, w_ref, qg_ref, kg_ref, q_ref, k_ref, v_ref):
    q, k, v = _qkv_slabs(x_ref[...], g_ref, w_ref, qg_ref, kg_ref)
    q_ref[...] = jnp.concatenate(q, axis=1)
    k_ref[...] = jnp.concatenate(k, axis=1)
    v_ref[...] = jnp.concatenate(v, axis=1)


def _qkv_perm_kernel(x_ref, g_ref, w_ref, qg_ref, kg_ref, q_ref, k_ref, v_ref, xs_ref):
    for p in range(NPAIR):
        xs_ref[p] = x_ref[:, p * LANES:(p + 1) * LANES]
    for rs in _row_parts(x_ref.shape[0]):
        n = (rs.stop - rs.start) // MAX_DIL
        x = jnp.concatenate(
            [jnp.concatenate([xs_ref[p, pl.ds(rs.start + r, n, stride=MAX_DIL), :] for p in range(NPAIR)], axis=1)
             for r in range(MAX_DIL)], axis=0)
        q, k, v = _qkv_slabs(x, g_ref, w_ref, qg_ref, kg_ref)
        dst = slice(rs.start // MAX_DIL, rs.stop // MAX_DIL)
        for ref, slabs in ((q_ref, q), (k_ref, k), (v_ref, v)):
            for p in range(NPAIR):
                for r in range(MAX_DIL):
                    ref[p, r, dst] = slabs[p][r * n:(r + 1) * n]


def _head_gain(g):
    return jnp.tile(g.reshape(1, ATT_DH), (1, LANES // ATT_DH))


def _qkv_flat(x2d, g, w_bf, qg, kg):
    m, d = x2d.shape
    tm = min(TOKEN_TILE, m)
    row = lambda i: (i, 0)
    return pl.pallas_call(
        _qkv_flat_kernel,
        grid=(m // tm,),
        in_specs=[pl.BlockSpec((tm, d), row), _const_spec((1, d)), _const_spec(w_bf.shape),
                  _const_spec((1, LANES)), _const_spec((1, LANES))],
        out_specs=[pl.BlockSpec((tm, d), row)] * 3,
        out_shape=[jax.ShapeDtypeStruct((m, d), F32)] * 3,
        compiler_params=_cparams(("arbitrary",)),
        name="qkv_flat",
    )(x2d, g.reshape(1, d), w_bf, _head_gain(qg), _head_gain(kg))


def _qkv_perm(x, g, w_bf, qg, kg):
    b, t, d = x.shape
    ns = t // SUPER
    tm = WIDE_TILE
    nt = SUPER // tm
    n = tm // MAX_DIL
    out_spec = pl.BlockSpec((None, NPAIR, None, MAX_DIL, None, n, LANES), lambda i, s, j: (i, 0, s, 0, j, 0, 0))
    out_shape = jax.ShapeDtypeStruct((b, NPAIR, ns, MAX_DIL, nt, n, LANES), F32)
    q, k, v = pl.pallas_call(
        _qkv_perm_kernel,
        grid=(b, ns, nt),
        in_specs=[pl.BlockSpec((None, tm, d), lambda i, s, j: (i, s * nt + j, 0)),
                  _const_spec((1, d)), _const_spec(w_bf.shape), _const_spec((1, LANES)), _const_spec((1, LANES))],
        out_specs=[out_spec] * 3,
        out_shape=[out_shape] * 3,
        scratch_shapes=[pltpu.VMEM((NPAIR, tm, LANES), F32)],
        compiler_params=_cparams(("arbitrary",) * 3),
        name="qkv_perm",
    )(x, g.reshape(1, d), w_bf, _head_gain(qg), _head_gain(kg))
    return tuple(a.reshape(b, NPAIR, t, LANES) for a in (q, k, v))


def _alibi_slopes():
    return np.power(np.float32(2.0), np.float32(-8.0) * np.arange(1, ATT_HEADS + 1, dtype=np.float32) / ATT_HEADS)


PREV_SLOTS = 32
ATT_UNROLL = 16
LOG2E = 1.4426950408889634


def _band_bias():
    i = np.arange(ATT_BLOCK)
    c = np.arange(2 * ATT_BLOCK)
    q_idx = {1: MAX_DIL * (i % 8) + i // 8,
             4: 4 * (i % 32) + i // 32,
             16: i}
    k_idx = {1: (MAX_DIL * (c % 8) + c // 16, (c // 8) % 2 == 0),
             4: (4 * (c % 32) + c // 64, (c // 32) % 2 == 0),
             16: (c % ATT_BLOCK, c < ATT_BLOCK)}
    slopes = _alibi_slopes().reshape(ATT_HEADS // 2, 2, 1, 1)
    neg = np.float32(NEG)
    out = []
    for (w, d) in PATTERNS:
        span = w // d
        km, is_prev = k_idx[d]
        diff = q_idx[d][:, None] - (km - np.where(is_prev, ATT_BLOCK, 0))[None, :]
        valid = (diff >= 0) & (diff <= span)
        bias = -slopes * (d * diff).astype(np.float32)[None, None]
        bias = bias * np.float32(LOG2E)
        variants = [np.where(valid, bias, neg), np.where(valid & ~is_prev[None, :], bias, neg)]
        out.append(np.stack([v.reshape(ATT_HEADS // 2, 2 * ATT_BLOCK, 2 * ATT_BLOCK) for v in variants]))
    return np.stack(out).astype(np.float32)


def _attend_pair(q, k, v, bias, first):
    qs = jnp.concatenate([jnp.where(first, q, 0.0), jnp.where(first, 0.0, q)], axis=0).astype(BF16)
    s = lax.dot_general(qs, k.astype(BF16), (((1,), (1,)), ((), ())), preferred_element_type=F32) + bias
    m = jnp.max(s, axis=-1, keepdims=True)
    e = jnp.exp2(s - m)
    den = jnp.sum(e, axis=-1, keepdims=True)
    o = jnp.dot(e.astype(BF16), v.astype(BF16), preferred_element_type=F32)
    halves = lambda a: jnp.where(first, a[:ATT_BLOCK], a[ATT_BLOCK:])
    return halves(o), halves(m), halves(den)


def _attn_prompt_kernel(q_ref, kc_ref, kp_ref, vc_ref, vp_ref, bias_ref, o_ref, ko_ref, vo_ref,
                        kext, vext, oacc, macc, dacc, nat_ref):
    sb = pl.program_id(2)
    first_sb = (sb == 0).astype(jnp.int32)
    first = lax.broadcasted_iota(jnp.int32, (ATT_BLOCK, LANES), 1) < ATT_DH
    scale = (ATT_DH ** -0.5) * LOG2E
    ps = PREV_SLOTS

    for src_p, src_c, ext in ((kp_ref, kc_ref, kext), (vp_ref, vc_ref, vext)):
        for r in range(MAX_DIL):
            ext[r, 0:ps, :] = src_p[r * ATT_BLOCK + ATT_BLOCK - ps:(r + 1) * ATT_BLOCK, :]
            ext[r, ps:ps + ATT_BLOCK, :] = src_c[r * ATT_BLOCK:(r + 1) * ATT_BLOCK, :]

    def attend_store(p, variant, q_chunks, k, v):
        n = ATT_BLOCK // len(q_chunks)
        q = jnp.concatenate([q_ref[sl, :] for sl in q_chunks], axis=0) * scale
        o, m, den = _attend_pair(q, k, v, bias_ref[p, variant], first)
        for ci, sl in enumerate(q_chunks):
            oacc[p, sl, :] = o[ci * n:(ci + 1) * n]
            macc[p, sl, :] = m[ci * n:(ci + 1) * n]
            dacc[p, sl, :] = den[ci * n:(ci + 1) * n]

    n0 = ATT_BLOCK // MAX_DIL

    def block0(bi):
        base = pl.multiple_of(bi * n0, n0)
        ksl = pl.ds(base + ps - n0, 2 * n0)
        k = jnp.concatenate([kext[r, ksl, :] for r in range(MAX_DIL)], axis=0)
        v = jnp.concatenate([vext[r, ksl, :] for r in range(MAX_DIL)], axis=0)
        attend_store(0, first_sb * jnp.asarray(bi == 0, jnp.int32),
                     [pl.ds(r * ATT_BLOCK + base, n0) for r in range(MAX_DIL)], k, v)

    d1 = PATTERNS[1][1]
    nc1 = MAX_DIL // d1
    n1 = ATT_BLOCK // nc1

    def block1(idx):
        r4, mb = idx // (ATT_BLOCK // n1), idx % (ATT_BLOCK // n1)
        base = pl.multiple_of(mb * n1, n1)
        ksl = pl.ds(base + ps - n1, 2 * n1)
        k = jnp.concatenate([kext[r4 + d1 * c, ksl, :] for c in range(nc1)], axis=0)
        v = jnp.concatenate([vext[r4 + d1 * c, ksl, :] for c in range(nc1)], axis=0)
        attend_store(1, first_sb * jnp.asarray(mb == 0, jnp.int32),
                     [pl.ds(pl.multiple_of((r4 + d1 * c) * ATT_BLOCK + base, n1), n1) for c in range(nc1)], k, v)

    def block2(r):
        sl = pl.ds(pl.multiple_of(r * ATT_BLOCK, ATT_BLOCK), ATT_BLOCK)
        k = jnp.concatenate([kp_ref[sl, :], kc_ref[sl, :]], axis=0)
        v = jnp.concatenate([vp_ref[sl, :], vc_ref[sl, :]], axis=0)
        attend_store(2, first_sb, [sl], k, v)

    for block in (block0, block1, block2):
        def body(it, carry, block=block):
            for u in range(ATT_UNROLL):
                block(it * ATT_UNROLL + u)
            return carry

        lax.fori_loop(0, MAX_DIL // ATT_UNROLL, body, 0)

    def comb(r, carry):
        sl = pl.ds(pl.multiple_of(r * ATT_BLOCK, ATT_BLOCK), ATT_BLOCK)
        m0, m1, m2 = macc[0, sl, :], macc[1, sl, :], macc[2, sl, :]
        mx = jnp.maximum(jnp.maximum(m0, m1), m2)
        w0, w1, w2 = jnp.exp2(m0 - mx), jnp.exp2(m1 - mx), jnp.exp2(m2 - mx)
        num = w0 * oacc[0, sl, :] + w1 * oacc[1, sl, :] + w2 * oacc[2, sl, :]
        den = w0 * dacc[0, sl, :] + w1 * dacc[1, sl, :] + w2 * dacc[2, sl, :]
        o_ref[pl.ds(r, ATT_BLOCK, stride=MAX_DIL), :] = num / den
        return carry

    lax.fori_loop(0, SUPER // ATT_BLOCK, comb, 0)

    @pl.when(sb == pl.num_programs(2) - 1)
    def _():
        for src, dst in ((kc_ref, ko_ref), (vc_ref, vo_ref)):
            for r in range(MAX_DIL):
                nat_ref[pl.ds(r, ATT_BLOCK, stride=MAX_DIL), :] = src[r * ATT_BLOCK:(r + 1) * ATT_BLOCK, :]
            dst[...] = jnp.transpose(nat_ref[...])


def _attn_prompt(q, k, v):
    b, npair, t, _ = q.shape
    ns = t // SUPER
    bias = _band_bias()
    cur = pl.BlockSpec((None, None, SUPER, LANES), lambda i, p, s: (i, p, s, 0))
    prev = pl.BlockSpec((None, None, SUPER, LANES), lambda i, p, s: (i, p, jnp.maximum(s - 1, 0), 0))
    cache = pl.BlockSpec((None, LANES, SUPER), lambda i, p, s: (i, p, 0))
    cache_shape = jax.ShapeDtypeStruct((b, npair * LANES, SUPER), F32)
    return pl.pallas_call(
        _attn_prompt_kernel,
        grid=(b, npair, ns),
        in_specs=[cur, cur, prev, cur, prev,
                  pl.BlockSpec((len(PATTERNS), 2, None, 2 * ATT_BLOCK, 2 * ATT_BLOCK),
                               lambda i, p, s: (0, 0, p, 0, 0))],
        out_specs=[pl.BlockSpec((None, SUPER, LANES), lambda i, p, s: (i, s, p)), cache, cache],
        out_shape=[jax.ShapeDtypeStruct((b, t, npair * LANES), F32), cache_shape, cache_shape],
        scratch_shapes=[pltpu.VMEM((MAX_DIL, PREV_SLOTS + ATT_BLOCK, LANES), F32),
                        pltpu.VMEM((MAX_DIL, PREV_SLOTS + ATT_BLOCK, LANES), F32),
                        pltpu.VMEM((len(PATTERNS), SUPER, LANES), F32),
                        pltpu.VMEM((len(PATTERNS), SUPER, LANES), F32),
                        pltpu.VMEM((len(PATTERNS), SUPER, LANES), F32),
                        pltpu.VMEM((SUPER, LANES), F32)],
        compiler_params=_cparams(("arbitrary",) * 3),
        name="attn_prompt",
    )(q, k, k, v, v, bias)


QUAD = 4
NEW_PAD = 128


def _sample_bias(s_len, n_past):
    s = np.arange(s_len)
    col = np.arange(n_past + NEW_PAD)
    dist = (n_past + s)[:, None] - col[None, :]
    real = (col < n_past + s_len)[None, :]
    slopes = _alibi_slopes().reshape(ATT_HEADS // QUAD, QUAD, 1, 1)
    out = []
    for (w, d) in PATTERNS:
        valid = (dist >= 0) & (dist % d == 0) & (dist <= w) & real
        bias = np.where(valid[None, None], -slopes * dist.astype(np.float32)[None, None], np.float32(NEG))
        out.append(bias.reshape(ATT_HEADS // QUAD, QUAD * s_len, n_past + NEW_PAD))
    return np.stack(out).astype(np.float32)


def _attn_sample_kernel(q_ref, kn_ref, vn_ref, ck_ref, cv_ref, bias_ref, o_ref, *, s_len):
    n_past = ck_ref.shape[1]
    width = QUAD * ATT_DH
    rows = QUAD * s_len
    own = (lax.broadcasted_iota(jnp.int32, (rows, width), 0) // s_len
           == lax.broadcasted_iota(jnp.int32, (rows, width), 1) // ATT_DH)
    nt = (((1,), (1,)), ((), ()))
    for quad in range(q_ref.shape[1] // width):
        cols = slice(quad * width, (quad + 1) * width)
        q = q_ref[:, cols] * (ATT_DH ** -0.5)
        qm = jnp.where(own, jnp.concatenate([q] * QUAD, axis=0), 0.0).astype(BF16)
        kn = _pad_rows(kn_ref[:, cols], NEW_PAD).astype(BF16)
        vn = _pad_rows(vn_ref[:, cols], NEW_PAD).astype(BF16)
        s_c = jnp.dot(qm, ck_ref[cols, :].astype(BF16), preferred_element_type=F32)
        s_n = lax.dot_general(qm, kn, nt, preferred_element_type=F32)
        es_c, es_n, dens, lses = [], [], [], []
        for p in range(len(PATTERNS)):
            sc = s_c + bias_ref[p, quad, :, 0:n_past]
            sn = s_n + bias_ref[p, quad, :, n_past:n_past + NEW_PAD]
            m = jnp.maximum(jnp.max(sc, axis=-1, keepdims=True), jnp.max(sn, axis=-1, keepdims=True))
            ec, en = jnp.exp(sc - m), jnp.exp(sn - m)
            den = jnp.sum(ec, axis=-1, keepdims=True) + jnp.sum(en, axis=-1, keepdims=True)
            es_c.append(ec)
            es_n.append(en)
            dens.append(den)
            lses.append(m + jnp.log(den))
        mx = jnp.maximum(jnp.maximum(lses[0], lses[1]), lses[2])
        ws = [jnp.exp(l - mx) for l in lses]
        wsum = ws[0] + ws[1] + ws[2]
        coef = [w / (wsum * d) for w, d in zip(ws, dens)]
        c_c = coef[0] * es_c[0] + coef[1] * es_c[1] + coef[2] * es_c[2]
        c_n = coef[0] * es_n[0] + coef[1] * es_n[1] + coef[2] * es_n[2]
        o = (lax.dot_general(c_c.astype(BF16), cv_ref[cols, :].astype(BF16), nt, preferred_element_type=F32)
             + jnp.dot(c_n.astype(BF16), vn, preferred_element_type=F32))
        o = jnp.where(own, o, 0.0)
        acc = o[0:s_len]
        for hh in range(1, QUAD):
            acc = acc + o[hh * s_len:(hh + 1) * s_len]
        o_ref[:, cols] = acc.astype(o_ref.dtype)


def _attn_sample(q, kn, vn, ck, cv):
    b, s_len, d = q.shape
    n_past = ck.shape[2]
    bias = _sample_bias(s_len, n_past)
    new = pl.BlockSpec((None, s_len, d), lambda i: (i, 0, 0))
    old = pl.BlockSpec((None, d, n_past), lambda i: (i, 0, 0))
    return pl.pallas_call(
        functools.partial(_attn_sample_kernel, s_len=s_len),
        grid=(b,),
        in_specs=[new, new, new, old, old, _const_spec(bias.shape)],
        out_specs=new,
        out_shape=jax.ShapeDtypeStruct((b, s_len, d), BF16),
        compiler_params=_cparams(("arbitrary",)),
        name="attn_sample",
    )(q, kn, vn, ck, cv, bias)


def kernel(x_prompt, x_sample, state_ret, state_lru, state_conv, cache_k, cache_v, norm_mix, norm_ffn, w_in_rec, w_out_rec, ret_norm_g, conv_w, conv_b, w_rgate, b_rgate, w_igate, b_igate, lru_lambda, w_in_att, w_out_att, q_norm_g, k_norm_g, w_ffn_gate, w_ffn_up, w_ffn_down):
    bp, tp, d = x_prompt.shape
    bs, ts, _ = x_sample.shape
    depth = norm_mix.shape[0]
    assert tp % SUPER == 0 and d == D_MODEL
    yp, ys = x_prompt, x_sample
    ffn_stacked = (w_ffn_gate.astype(BF16), w_ffn_up.astype(BF16), w_ffn_down.astype(BF16))
    ret_p, lru_p, conv_p, kp, vp = [], [], [], [], []
    ret_s, lru_s, conv_s, ksl, vsl = [], [], [], [], []
    for l in range(depth):
        i = l // 2
        if l % 2 == 0:
            ffn = (w_out_rec[i].astype(BF16), norm_ffn[l], ffn_stacked, l)
            w_in = w_in_rec[i].astype(BF16)
            rec_prm = (ret_norm_g[i], conv_w[i], conv_b[i], w_rgate[i], b_rgate[i], w_igate[i], b_igate[i],
                       lru_lambda[i])
            proj = _norm_proj(yp.reshape(bp * tp, d), norm_mix[l], w_in).reshape(bp, tp, REC_IN)
            mix, sr, sl, sc = _rec_mix(proj, jnp.zeros((bp, RET_HEADS, RET_DK, RET_DV), F32),
                                       jnp.zeros((bp, LRU_WIDTH), F32),
                                       jnp.zeros((bp, CONV_WIDTH - 1, LRU_WIDTH), F32), *rec_prm)
            yp = _out_ffn(mix.reshape(bp * tp, d), yp.reshape(bp * tp, d), *ffn).reshape(bp, tp, d)
            ret_p.append(sr); lru_p.append(sl); conv_p.append(sc)
            proj = _norm_proj(ys.reshape(bs * ts, d), norm_mix[l], w_in).reshape(bs, ts, REC_IN)
            mix, sr, sl, sc = _rec_mix(proj, state_ret[i], state_lru[i], state_conv[i], *rec_prm)
            ys = _out_ffn(mix.reshape(bs * ts, d), ys.reshape(bs * ts, d), *ffn).reshape(bs, ts, d)
            ret_s.append(sr); lru_s.append(sl); conv_s.append(sc)
        else:
            ffn = (w_out_att[i].astype(BF16), norm_ffn[l], ffn_stacked, l)
            w_in = w_in_att[i].astype(BF16)
            q, k, v = _qkv_perm(yp, norm_mix[l], w_in, q_norm_g[i], k_norm_g[i])
            o, kc, vc = _attn_prompt(q, k, v)
            yp = _out_ffn(o.reshape(bp * tp, d), yp.reshape(bp * tp, d), *ffn).reshape(bp, tp, d)
            from_slabs = lambda c: jnp.transpose(c.reshape(bp, ATT_HEADS, ATT_DH, SUPER), (0, 3, 1, 2))
            kp.append(from_slabs(kc)); vp.append(from_slabs(vc))
            qs, kn, vn = _qkv_flat(ys.reshape(bs * ts, d), norm_mix[l], w_in, q_norm_g[i], k_norm_g[i])
            n_past = cache_k.shape[2]
            to_slabs = lambda c: jnp.transpose(c, (0, 2, 3, 1)).reshape(bs, d, n_past)
            o = _attn_sample(qs.reshape(bs, ts, d), kn.reshape(bs, ts, d), vn.reshape(bs, ts, d),
                             to_slabs(cache_k[i]), to_slabs(cache_v[i]))
            ys = _out_ffn(o.reshape(bs * ts, d), ys.reshape(bs * ts, d), *ffn).reshape(bs, ts, d)
            ksl.append(kn.reshape(bs, ts, ATT_HEADS, ATT_DH)); vsl.append(vn.reshape(bs, ts, ATT_HEADS, ATT_DH))
    return (yp, ys,
            jnp.stack(ret_p), jnp.stack(lru_p), jnp.stack(conv_p), jnp.stack(kp), jnp.stack(vp),
            jnp.stack(ret_s), jnp.stack(lru_s), jnp.stack(conv_s), jnp.stack(ksl), jnp.stack(vsl))
```

```python
import functools

import jax
import numpy as np
import jax.numpy as jnp
from jax import lax
from jax.experimental import pallas as pl
from jax.experimental.pallas import tpu as pltpu

F32 = jnp.float32
BF16 = jnp.bfloat16

EPS = 1e-6
D_MODEL = 1024
RET_HEADS = 4
RET_DK = 64
RET_DV = 128
RET_CHUNK = 128
LRU_WIDTH = 512
LRU_BLOCKS = 4
LRU_BLOCK = LRU_WIDTH // LRU_BLOCKS
CONV_WIDTH = 4
LRU_C = 8.0
GELU_C = 0.7978845608028654
REC_IN = 2 * RET_HEADS * RET_DK + 2 * RET_HEADS * RET_DV + 2 * LRU_WIDTH
ATT_HEADS = 16
ATT_DH = 64
PATTERNS = ((128, 1), (512, 4), (2048, 16))
ATT_BLOCK = 128
SUPER = 2048
MAX_DIL = 16
NEG = -1e30

LANES = 128
SUBLANES = 8
MXU_COLS = 256
NPAIR = D_MODEL // LANES
VMEM_LIMIT = 56 * 1024 * 1024

FF_CHUNK = 256
TOKEN_TILE = 512
WIDE_TILE = 1024
STREAM_ROWS = 512
REC_SEQS = 4


def _cparams(sem):
    return pltpu.CompilerParams(dimension_semantics=sem, vmem_limit_bytes=VMEM_LIMIT)


def _const_spec(shape):
    nd = len(shape)
    return pl.BlockSpec(shape, lambda *_: (0,) * nd, pipeline_mode=pl.Buffered(1))


def _rmsnorm(x, g):
    return x * lax.rsqrt(jnp.mean(x * x, axis=-1, keepdims=True) + EPS) * g


def _row_parts(rows):
    n = max(1, rows // STREAM_ROWS)
    return [slice(i * (rows // n), (i + 1) * (rows // n)) for i in range(n)]


def _norm_proj_kernel(x_ref, g_ref, w_ref, o_ref):
    for rs in _row_parts(x_ref.shape[0]):
        h = _rmsnorm(x_ref[rs, :], g_ref[...]).astype(BF16)
        o_ref[rs, :] = jnp.dot(h, w_ref[...], preferred_element_type=F32)


def _norm_proj(x2d, g, w_bf):
    m, d = x2d.shape
    n = w_bf.shape[1]
    tm = min(WIDE_TILE, m)
    return pl.pallas_call(
        _norm_proj_kernel,
        grid=(m // tm,),
        in_specs=[pl.BlockSpec((tm, d), lambda i: (i, 0)), _const_spec((1, d)), _const_spec((d, n))],
        out_specs=pl.BlockSpec((tm, n), lambda i: (i, 0)),
        out_shape=jax.ShapeDtypeStruct((m, n), F32),
        compiler_params=_cparams(("arbitrary",)),
        name="norm_proj",
    )(x2d, g.reshape(1, d), w_bf)


def _linear_scan(a, b, h0):
    rows, w = a.shape
    a = a.reshape(rows // SUBLANES, SUBLANES, w)
    b = b.reshape(rows // SUBLANES, SUBLANES, w)
    sub = lax.broadcasted_iota(jnp.int32, a.shape, 1)
    s = 1
    while s < SUBLANES:
        keep = sub >= s
        b = jnp.where(keep, a * pltpu.roll(b, s, axis=1), 0.0) + b
        a = jnp.where(keep, a * pltpu.roll(a, s, axis=1), a)
        s *= 2
    groups = []
    carry = h0
    for i in range(rows // SUBLANES):
        groups.append(b[i] + a[i] * carry)
        carry = groups[-1][SUBLANES - 1:SUBLANES]
    return jnp.concatenate(groups, axis=0)


def _pad_rows(x, rows):
    if x.shape[0] == rows:
        return x
    return jnp.concatenate([x, jnp.zeros((rows - x.shape[0], x.shape[1]), x.dtype)], axis=0)


def _mixer_init(s0_ref, h0_ref, c0_ref, s_scr, h_scr, xe_scr):
    s_scr[...] = s0_ref[...]
    h_scr[...] = h0_ref[...]
    xe_scr[...] = jnp.zeros(xe_scr.shape, F32)
    xe_scr[:, SUBLANES - (CONV_WIDTH - 1):SUBLANES, :] = c0_ref[...]


def _run_stages(*streams):
    for stages in streams:
        for stage in stages:
            stage()


def _mixer(*args):
    _run_stages(_mixer_stages(*args))


def _mixer_stages(get, put, nseq, rows, tables, params, s_scr, h_scr, xe_scr, sret_ref, hlast_ref, cnew_ref):
    dec_ref, qdec_ref, kdec_ref, cdec_ref = tables
    retg_ref, convw_ref, convb_ref, wrg_ref, brg_ref, wig_ref, big_ref, lam_ref = params
    cp = RET_CHUNK
    hk = RET_HEADS * RET_DK
    v_off, g_off = 2 * hk, 2 * hk + RET_HEADS * RET_DV
    xl_off = g_off + RET_HEADS * RET_DV
    yl_off = xl_off + LRU_WIDTH
    lane = lax.broadcasted_iota(jnp.int32, (cp, LANES), 1)
    first = lane < RET_DK
    srow_first = lax.broadcasted_iota(jnp.int32, (2 * RET_DK, RET_DV), 0) < RET_DK
    nl = -lam_ref[...]
    neg_c_softplus = (-LRU_C) * (jnp.maximum(nl, 0.0) + jnp.log1p(jnp.exp(-jnp.abs(nl))))

    def retention(bi):
        for p in range(RET_HEADS // 2):
            q2 = _pad_rows(get(bi, slice(p * LANES, (p + 1) * LANES)), cp)
            k2 = _pad_rows(get(bi, slice(hk + p * LANES, hk + (p + 1) * LANES)), cp) * (RET_DK ** -0.5)
            kb = k2.astype(BF16)
            kdt = jnp.transpose(k2 * kdec_ref[p]).astype(BF16)
            s2 = s_scr[bi, p]
            s2b = s2.astype(BF16)
            new_s = []
            for hh in range(2):
                h = 2 * p + hh
                sel = first if hh == 0 else jnp.logical_not(first)
                qh = jnp.where(sel, q2, 0.0)
                v = _pad_rows(get(bi, slice(v_off + h * RET_DV, v_off + (h + 1) * RET_DV)), cp)
                vb = v.astype(BF16)
                att = lax.dot_general(qh.astype(BF16), kb, (((1,), (1,)), ((), ())),
                                      preferred_element_type=F32) * dec_ref[h]
                o = (jnp.dot(att.astype(BF16), vb, preferred_element_type=F32)
                     + jnp.dot((qh * qdec_ref[p]).astype(BF16), s2b, preferred_element_type=F32))
                new_s.append(jnp.dot(kdt, vb, preferred_element_type=F32))
                o = o[:rows]
                o = o * lax.rsqrt(jnp.mean(o * o, axis=-1, keepdims=True) + EPS) * retg_ref[h:h + 1, :]
                gate = get(bi, slice(g_off + h * RET_DV, g_off + (h + 1) * RET_DV))
                put(bi, slice(h * RET_DV, (h + 1) * RET_DV), o * (gate * jax.nn.sigmoid(gate)))
            s_scr[bi, p] = s2 * cdec_ref[p] + jnp.where(srow_first, new_s[0], new_s[1])

    def rg_lru(bi):
        x = get(bi, slice(xl_off, xl_off + LRU_WIDTH))
        xe = jnp.concatenate([xe_scr[bi], x], axis=0)
        xc = convb_ref[...]
        for j in range(CONV_WIDTH):
            back = CONV_WIDTH - 1 - j
            xj = x if back == 0 else pltpu.roll(xe, back, axis=0)[SUBLANES:]
            xc = xc + xj * convw_ref[j:j + 1, :]
        xe_scr[bi] = xe[rows:]
        cnew_ref[bi] = xe_scr[bi, pl.ds(SUBLANES - (CONV_WIDTH - 1), CONV_WIDTH - 1), :]

        xcb = _pad_rows(xc, max(rows, 2 * SUBLANES)).astype(BF16)
        r_parts, i_parts = [], []
        for n in range(LRU_BLOCKS):
            xb = xcb[:, n * LRU_BLOCK:(n + 1) * LRU_BLOCK]
            r_parts.append(jnp.dot(xb, wrg_ref[n], preferred_element_type=F32)[:rows])
            i_parts.append(jnp.dot(xb, wig_ref[n], preferred_element_type=F32)[:rows])
        r = jax.nn.sigmoid(jnp.concatenate(r_parts, axis=1) + brg_ref[...])
        ig = jax.nn.sigmoid(jnp.concatenate(i_parts, axis=1) + big_ref[...])
        a = jnp.exp(r * neg_c_softplus)
        one_m = 1.0 - a * a
        root = jnp.where(one_m > 0.0, one_m * lax.rsqrt(one_m), 0.0)
        hseq = _linear_scan(a, root * (ig * xc), h_scr[bi])
        h_scr[bi] = hseq[rows - 1:rows, :]
        hlast_ref[bi] = hseq[rows - 1:rows, :]
        yl = get(bi, slice(yl_off, yl_off + LRU_WIDTH))
        half = 0.5 * yl
        gelu = half + half * jnp.tanh(yl * (GELU_C + (0.044715 * GELU_C) * (yl * yl)))
        put(bi, slice(RET_HEADS * RET_DV, RET_HEADS * RET_DV + LRU_WIDTH), hseq * gelu)

    def finish():
        sret_ref[...] = s_scr[...]

    stages = []
    for bi in range(nseq):
        stages += [functools.partial(retention, bi), functools.partial(rg_lru, bi)]
    return stages + [finish]


def _rec_mix_kernel(proj_ref, s0_ref, h0_ref, c0_ref, dec_ref, qdec_ref, kdec_ref, cdec_ref,
                    retg_ref, convw_ref, convb_ref, wrg_ref, brg_ref, wig_ref, big_ref, lam_ref,
                    mix_ref, sret_ref, hlast_ref, cnew_ref,
                    s_scr, h_scr, xe_scr, *, rows):
    @pl.when(pl.program_id(1) == 0)
    def _():
        _mixer_init(s0_ref, h0_ref, c0_ref, s_scr, h_scr, xe_scr)

    def put(bi, cols, val):
        mix_ref[bi, :, cols] = val.astype(mix_ref.dtype)

    _mixer(lambda bi, cols: proj_ref[bi, :, cols], put, proj_ref.shape[0], rows,
           (dec_ref, qdec_ref, kdec_ref, cdec_ref),
           (retg_ref, convw_ref, convb_ref, wrg_ref, brg_ref, wig_ref, big_ref, lam_ref),
           s_scr, h_scr, xe_scr, sret_ref, hlast_ref, cnew_ref)


def _ret_tables(t_eff):
    cp = RET_CHUNK
    f32 = np.float32
    log_g = np.log1p(-np.power(f32(2.0), f32(-5.0) - np.arange(RET_HEADS, dtype=f32)))
    idx = np.arange(cp, dtype=f32)
    live = idx < t_eff
    diff = idx[:, None] - idx[None, :]
    ok = (diff >= 0) & live[:, None] & live[None, :]
    decay = np.where(ok, np.exp(log_g[:, None, None] * np.maximum(diff, f32(0.0))), f32(0.0))
    q_dec = np.where(live, np.exp(log_g[:, None] * (idx + f32(1.0))), f32(0.0))
    k_dec = np.where(live, np.exp(log_g[:, None] * (f32(t_eff - 1.0) - idx)), f32(0.0))
    c_dec = np.exp(log_g * f32(t_eff))

    def pair_lanes(t):
        t = t.reshape(RET_HEADS // 2, 2, cp)
        return np.repeat(np.transpose(t, (0, 2, 1)), RET_DK, axis=2)

    c_rows = np.repeat(c_dec.reshape(RET_HEADS // 2, 2), RET_DK, axis=1)
    c_rows = np.broadcast_to(c_rows[:, :, None], (RET_HEADS // 2, 2 * RET_DK, RET_DV))
    as32 = lambda t: np.ascontiguousarray(t, dtype=f32)
    return as32(decay), as32(pair_lanes(q_dec)), as32(pair_lanes(k_dec)), as32(c_rows)


def _rec_mix(proj, s0, h0, c0, ret_g, conv_w, conv_b, w_rg, b_rg, w_ig, b_ig, lam):
    b, t, _ = proj.shape
    rows = RET_CHUNK if t % RET_CHUNK == 0 else t
    assert t % rows == 0 and rows % SUBLANES == 0 and (t == rows or rows == RET_CHUNK)
    n = t // rows
    nb = REC_SEQS if b % REC_SEQS == 0 else 1
    dec, qdec, kdec, cdec = _ret_tables(rows)
    hp = RET_HEADS // 2
    w = LRU_WIDTH
    per_b3 = lambda i, c: (i, 0, 0)
    outs = pl.pallas_call(
        functools.partial(_rec_mix_kernel, rows=rows),
        grid=(b // nb, n),
        in_specs=[
            pl.BlockSpec((nb, rows, REC_IN), lambda i, c: (i, c, 0)),
            pl.BlockSpec((nb, hp, 2 * RET_DK, RET_DV), lambda i, c: (i, 0, 0, 0)),
            pl.BlockSpec((nb, 1, w), per_b3),
            pl.BlockSpec((nb, CONV_WIDTH - 1, w), per_b3),
            _const_spec(dec.shape), _const_spec(qdec.shape), _const_spec(kdec.shape), _const_spec(cdec.shape),
            _const_spec((RET_HEADS, RET_DV)), _const_spec((CONV_WIDTH, w)), _const_spec((1, w)),
            _const_spec((LRU_BLOCKS, LRU_BLOCK, LRU_BLOCK)), _const_spec((1, w)),
            _const_spec((LRU_BLOCKS, LRU_BLOCK, LRU_BLOCK)), _const_spec((1, w)), _const_spec((1, w)),
        ],
        out_specs=[
            pl.BlockSpec((nb, rows, D_MODEL), lambda i, c: (i, c, 0)),
            pl.BlockSpec((nb, hp, 2 * RET_DK, RET_DV), lambda i, c: (i, 0, 0, 0)),
            pl.BlockSpec((nb, 1, w), per_b3),
            pl.BlockSpec((nb, CONV_WIDTH - 1, w), per_b3),
        ],
        out_shape=[
            jax.ShapeDtypeStruct((b, t, D_MODEL), BF16),
            jax.ShapeDtypeStruct((b, hp, 2 * RET_DK, RET_DV), F32),
            jax.ShapeDtypeStruct((b, 1, w), F32),
            jax.ShapeDtypeStruct((b, CONV_WIDTH - 1, w), F32),
        ],
        scratch_shapes=[
            pltpu.VMEM((nb, hp, 2 * RET_DK, RET_DV), F32),
            pltpu.VMEM((nb, 1, w), F32),
            pltpu.VMEM((nb, SUBLANES, w), F32),
        ],
        compiler_params=_cparams(("arbitrary", "arbitrary")),
        name="rec_mix",
    )(proj, s0.reshape(b, hp, 2 * RET_DK, RET_DV), h0.reshape(b, 1, w), c0,
      dec, qdec, kdec, cdec, ret_g, conv_w, conv_b.reshape(1, w),
      w_rg.astype(BF16), b_rg.reshape(1, w), w_ig.astype(BF16), b_ig.reshape(1, w), lam.reshape(1, w))
    mix, s_ret, h_last, c_new = outs
    return mix, s_ret.reshape(b, RET_HEADS, RET_DK, RET_DV), h_last.reshape(b, w), c_new


def _ffn_stages(load_mix, load_x, store, ffn_refs, act_ref, rs):
    wo_ref, g_ref, wg_ref, wu_ref, wd_ref = ffn_refs
    live = {}

    def prologue():
        live["y"] = load_x() + jnp.dot(load_mix(), wo_ref[...], preferred_element_type=F32)
        live["h"] = _rmsnorm(live["y"], g_ref[...]).astype(BF16)

    def gate_up(c):
        cols = slice(c * FF_CHUNK, (c + 1) * FF_CHUNK)
        gate = jnp.dot(live["h"], wg_ref[:, cols], preferred_element_type=F32)
        up = jnp.dot(live["h"], wu_ref[:, cols], preferred_element_type=F32)
        act_ref[rs, cols] = (gate * jax.nn.sigmoid(gate) * up).astype(BF16)

    def down():
        store(live["y"] + jnp.dot(act_ref[rs, :], wd_ref[...], preferred_element_type=F32))

    return ([prologue] + [functools.partial(gate_up, c) for c in range(wg_ref.shape[1] // FF_CHUNK)] + [down])


def _out_ffn_kernel(mix_ref, x_ref, wo_ref, g_ref, wg_ref, wu_ref, wd_ref, o_ref, act_ref):
    def store_rows(rs, val):
        o_ref[rs, :] = val

    for rs in _row_parts(x_ref.shape[0]):
        _run_stages(_ffn_stages(lambda rs=rs: mix_ref[rs, :].astype(BF16), lambda rs=rs: x_ref[rs, :],
                                functools.partial(store_rows, rs),
                                (wo_ref, g_ref, wg_ref, wu_ref, wd_ref), act_ref, rs))


def _layer_spec(stacked, layer):
    shape = stacked.shape[1:]
    return pl.BlockSpec((None,) + shape, lambda *_: (layer,) + (0,) * len(shape), pipeline_mode=pl.Buffered(1))


def _out_ffn(mix2d, x2d, w_out, g, ffn_stacked, layer):
    m, d = x2d.shape
    tm = min(WIDE_TILE, m)
    row = lambda i: (i, 0)
    return pl.pallas_call(
        _out_ffn_kernel,
        grid=(m // tm,),
        in_specs=[pl.BlockSpec((tm, d), row), pl.BlockSpec((tm, d), row), _const_spec(w_out.shape),
                  _const_spec((1, d))] + [_layer_spec(w, layer) for w in ffn_stacked],
        out_specs=pl.BlockSpec((tm, d), row),
        out_shape=jax.ShapeDtypeStruct((m, d), F32),
        scratch_shapes=[pltpu.VMEM((tm, ffn_stacked[0].shape[2]), BF16)],
        compiler_params=_cparams(("arbitrary",)),
        name="out_ffn",
    )(mix2d, x2d, w_out, g.reshape(1, d), *ffn_stacked)


def _layer0_kernel(xc_ref, xp_ref, gm_ref, win_ref, s0_ref, h0_ref, c0_ref,
                   dec_ref, qdec_ref, kdec_ref, cdec_ref,
                   retg_ref, convw_ref, convb_ref, wrg_ref, brg_ref, wig_ref, big_ref, lam_ref,
                   wo_ref, g_ref, wg_ref, wu_ref, wd_ref,
                   y_ref, sret_ref, hlast_ref, cnew_ref,
                   s_scr, h_scr, xe_scr, mix_scr, act_scr):
    c = pl.program_id(1)
    last = pl.num_programs(1) - 1
    nseq, rows, d = xc_ref.shape
    tables = (dec_ref, qdec_ref, kdec_ref, cdec_ref)
    params = (retg_ref, convw_ref, convb_ref, wrg_ref, brg_ref, wig_ref, big_ref, lam_ref)
    ffn_refs = (wo_ref, g_ref, wg_ref, wu_ref, wd_ref)

    def mix_stages(slot):
        live = {}

        def project():
            x = xc_ref[...].reshape(nseq * rows, d)
            live["proj"] = jnp.dot(_rmsnorm(x, gm_ref[...]).astype(BF16), win_ref[...], preferred_element_type=F32)

        def put(bi, cols, val):
            mix_scr[slot, bi * rows:(bi + 1) * rows, cols] = val.astype(mix_scr.dtype)

        return [project] + _mixer_stages(lambda bi, cols: live["proj"][bi * rows:(bi + 1) * rows, cols], put,
                                         nseq, rows, tables, params, s_scr, h_scr, xe_scr,
                                         sret_ref, hlast_ref, cnew_ref)

    def ffn_stages(slot):
        def store(val):
            y_ref[...] = val.reshape(nseq, rows, d)

        return _ffn_stages(lambda: mix_scr[slot], lambda: xp_ref[...].reshape(nseq * rows, d), store,
                           ffn_refs, act_scr, slice(0, nseq * rows))

    @pl.when(c == 0)
    def _():
        _mixer_init(s0_ref, h0_ref, c0_ref, s_scr, h_scr, xe_scr)
        _run_stages(mix_stages(0))

    @pl.when(jnp.logical_and(c > 0, c < last))
    def _():
        _run_stages(mix_stages(c % 2), ffn_stages(1 - c % 2))

    @pl.when(c == last)
    def _():
        _run_stages(ffn_stages(1 - last % 2))


def _layer0_prompt(x, g_mix, w_in, s0, h0, c0, rec_prm, w_out, g_ffn, ffn_stacked, layer):
    ret_g, conv_w, conv_b, w_rg, b_rg, w_ig, b_ig, lam = rec_prm
    b, t, d = x.shape
    rows = RET_CHUNK
    assert t % rows == 0 and b % REC_SEQS == 0
    n = t // rows
    nb = REC_SEQS
    dec, qdec, kdec, cdec = _ret_tables(rows)
    hp = RET_HEADS // 2
    w = LRU_WIDTH
    per_b3 = lambda i, c: (i, 0, 0)
    chunk = (nb, rows, d)
    outs = pl.pallas_call(
        _layer0_kernel,
        grid=(b // nb, n + 1),
        in_specs=[
            pl.BlockSpec(chunk, lambda i, c: (i, jnp.minimum(c, n - 1), 0)),
            pl.BlockSpec(chunk, lambda i, c: (i, jnp.maximum(c - 1, 0), 0)),
            _const_spec((1, d)), _const_spec(w_in.shape),
            pl.BlockSpec((nb, hp, 2 * RET_DK, RET_DV), lambda i, c: (i, 0, 0, 0)),
            pl.BlockSpec((nb, 1, w), per_b3),
            pl.BlockSpec((nb, CONV_WIDTH - 1, w), per_b3),
            _const_spec(dec.shape), _const_spec(qdec.shape), _const_spec(kdec.shape), _const_spec(cdec.shape),
            _const_spec((RET_HEADS, RET_DV)), _const_spec((CONV_WIDTH, w)), _const_spec((1, w)),
            _const_spec((LRU_BLOCKS, LRU_BLOCK, LRU_BLOCK)), _const_spec((1, w)),
            _const_spec((LRU_BLOCKS, LRU_BLOCK, LRU_BLOCK)), _const_spec((1, w)), _const_spec((1, w)),
            _const_spec(w_out.shape), _const_spec((1, d)),
        ] + [_layer_spec(wt, layer) for wt in ffn_stacked],
        out_specs=[
            pl.BlockSpec(chunk, lambda i, c: (i, jnp.maximum(c - 1, 0), 0)),
            pl.BlockSpec((nb, hp, 2 * RET_DK, RET_DV), lambda i, c: (i, 0, 0, 0)),
            pl.BlockSpec((nb, 1, w), per_b3),
            pl.BlockSpec((nb, CONV_WIDTH - 1, w), per_b3),
        ],
        out_shape=[
            jax.ShapeDtypeStruct((b, t, d), F32),
            jax.ShapeDtypeStruct((b, hp, 2 * RET_DK, RET_DV), F32),
            jax.ShapeDtypeStruct((b, 1, w), F32),
            jax.ShapeDtypeStruct((b, CONV_WIDTH - 1, w), F32),
        ],
        scratch_shapes=[
            pltpu.VMEM((nb, hp, 2 * RET_DK, RET_DV), F32),
            pltpu.VMEM((nb, 1, w), F32),
            pltpu.VMEM((nb, SUBLANES, w), F32),
            pltpu.VMEM((2, nb * rows, d), BF16),
            pltpu.VMEM((nb * rows, ffn_stacked[0].shape[2]), BF16),
        ],
        compiler_params=_cparams(("arbitrary", "arbitrary")),
        name="layer0",
    )(x, x, g_mix.reshape(1, d), w_in, s0.reshape(b, hp, 2 * RET_DK, RET_DV), h0.reshape(b, 1, w), c0,
      dec, qdec, kdec, cdec, ret_g, conv_w, conv_b.reshape(1, w),
      w_rg.astype(BF16), b_rg.reshape(1, w), w_ig.astype(BF16), b_ig.reshape(1, w), lam.reshape(1, w),
      w_out, g_ffn.reshape(1, d), *ffn_stacked)
    y, s_ret, h_last, c_new = outs
    return y, s_ret.reshape(b, RET_HEADS, RET_DK, RET_DV), h_last.reshape(b, w), c_new


def _pair_rms(a, g2, first):
    sq = a * a
    s_a = jnp.sum(jnp.where(first, sq, 0.0), axis=-1, keepdims=True)
    s_b = jnp.sum(jnp.where(first, 0.0, sq), axis=-1, keepdims=True)
    ms = jnp.where(first, s_a, s_b) * (1.0 / ATT_DH)
    return a * lax.rsqrt(ms + EPS) * g2


def _qkv_slabs(x, g_ref, w_ref, qg_ref, kg_ref):
    h = _rmsnorm(x, g_ref[...]).astype(BF16)
    first = lax.broadcasted_iota(jnp.int32, (x.shape[0], LANES), 1) < ATT_DH
    slabs = []
    for c in range(w_ref.shape[1] // MXU_COLS):
        part = jnp.dot(h, w_ref[:, c * MXU_COLS:(c + 1) * MXU_COLS], preferred_element_type=F32)
        slabs += [part[:, i * LANES:(i + 1) * LANES] for i in range(MXU_COLS // LANES)]
    q = [_pair_rms(slabs[p], qg_ref[...], first) for p in range(NPAIR)]
    k = [_pair_rms(slabs[NPAIR + p], kg_ref[...], first) for p in range(NPAIR)]
    v = slabs[2 * NPAIR:]
    return q, k, v


def _qkv_flat_kernel(x_ref, g_ref, w_ref, qg_ref, kg_ref, q_ref, k_ref, v_ref):
    q, k, v = _qkv_slabs(x_ref[...], g_ref, w_ref, qg_ref, kg_ref)
    q_ref[...] = jnp.concatenate(q, axis=1)
    k_ref[...] = jnp.concatenate(k, axis=1)
    v_ref[...] = jnp.concatenate(v, axis=1)


def _qkv_perm_kernel(x_ref, g_ref, w_ref, qg_ref, kg_ref, q_ref, k_ref, v_ref, xs_ref):
    for p in range(NPAIR):
        xs_ref[p] = x_ref[:, p * LANES:(p + 1) * LANES]
    for rs in _row_parts(x_ref.shape[0]):
        n = (rs.stop - rs.start) // MAX_DIL
        x = jnp.concatenate(
            [jnp.concatenate([xs_ref[p, pl.ds(rs.start + r, n, stride=MAX_DIL), :] for p in range(NPAIR)], axis=1)
             for r in range(MAX_DIL)], axis=0)
        q, k, v = _qkv_slabs(x, g_ref, w_ref, qg_ref, kg_ref)
        dst = slice(rs.start // MAX_DIL, rs.stop // MAX_DIL)
        for ref, slabs in ((q_ref, q), (k_ref, k), (v_ref, v)):
            for p in range(NPAIR):
                for r in range(MAX_DIL):
                    ref[p, r, dst] = slabs[p][r * n:(r + 1) * n]


def _head_gain(g):
    return jnp.tile(g.reshape(1, ATT_DH), (1, LANES // ATT_DH))


def _qkv_flat(x2d, g, w_bf, qg, kg):
    m, d = x2d.shape
    tm = min(TOKEN_TILE, m)
    row = lambda i: (i, 0)
    return pl.pallas_call(
        _qkv_flat_kernel,
        grid=(m // tm,),
        in_specs=[pl.BlockSpec((tm, d), row), _const_spec((1, d)), _const_spec(w_bf.shape),
                  _const_spec((1, LANES)), _const_spec((1, LANES))],
        out_specs=[pl.BlockSpec((tm, d), row)] * 3,
        out_shape=[jax.ShapeDtypeStruct((m, d), F32)] * 3,
        compiler_params=_cparams(("arbitrary",)),
        name="qkv_flat",
    )(x2d, g.reshape(1, d), w_bf, _head_gain(qg), _head_gain(kg))


def _qkv_perm(x, g, w_bf, qg, kg):
    b, t, d = x.shape
    ns = t // SUPER
    tm = WIDE_TILE
    nt = SUPER // tm
    n = tm // MAX_DIL
    out_spec = pl.BlockSpec((None, NPAIR, None, MAX_DIL, None, n, LANES), lambda i, s, j: (i, 0, s, 0, j, 0, 0))
    out_shape = jax.ShapeDtypeStruct((b, NPAIR, ns, MAX_DIL, nt, n, LANES), F32)
    q, k, v = pl.pallas_call(
        _qkv_perm_kernel,
        grid=(b, ns, nt),
        in_specs=[pl.BlockSpec((None, tm, d), lambda i, s, j: (i, s * nt + j, 0)),
                  _const_spec((1, d)), _const_spec(w_bf.shape), _const_spec((1, LANES)), _const_spec((1, LANES))],
        out_specs=[out_spec] * 3,
        out_shape=[out_shape] * 3,
        scratch_shapes=[pltpu.VMEM((NPAIR, tm, LANES), F32)],
        compiler_params=_cparams(("arbitrary",) * 3),
        name="qkv_perm",
    )(x, g.reshape(1, d), w_bf, _head_gain(qg), _head_gain(kg))
    return tuple(a.reshape(b, NPAIR, t, LANES) for a in (q, k, v))


def _alibi_slopes():
    return np.power(np.float32(2.0), np.float32(-8.0) * np.arange(1, ATT_HEADS + 1, dtype=np.float32) / ATT_HEADS)


PREV_SLOTS = 32
ATT_UNROLL = 16
LOG2E = 1.4426950408889634


def _band_bias():
    i = np.arange(ATT_BLOCK)
    c = np.arange(2 * ATT_BLOCK)
    q_idx = {1: MAX_DIL * (i % 8) + i // 8,
             4: 4 * (i % 32) + i // 32,
             16: i}
    k_idx = {1: (MAX_DIL * (c % 8) + c // 16, (c // 8) % 2 == 0),
             4: (4 * (c % 32) + c // 64, (c // 32) % 2 == 0),
             16: (c % ATT_BLOCK, c < ATT_BLOCK)}
    slopes = _alibi_slopes().reshape(ATT_HEADS // 2, 2, 1, 1)
    neg = np.float32(NEG)
    out = []
    for (w, d) in PATTERNS:
        span = w // d
        km, is_prev = k_idx[d]
        diff = q_idx[d][:, None] - (km - np.where(is_prev, ATT_BLOCK, 0))[None, :]
        valid = (diff >= 0) & (diff <= span)
        bias = -slopes * (d * diff).astype(np.float32)[None, None]
        bias = bias * np.float32(LOG2E)
        variants = [np.where(valid, bias, neg), np.where(valid & ~is_prev[None, :], bias, neg)]
        out.append(np.stack([v.reshape(ATT_HEADS // 2, 2 * ATT_BLOCK, 2 * ATT_BLOCK) for v in variants]))
    return np.stack(out).astype(np.float32)


def _attend_pair(q, k, v, bias, first):
    qs = jnp.concatenate([jnp.where(first, q, 0.0), jnp.where(first, 0.0, q)], axis=0).astype(BF16)
    s = lax.dot_general(qs, k.astype(BF16), (((1,), (1,)), ((), ())), preferred_element_type=F32) + bias
    m = jnp.max(s, axis=-1, keepdims=True)
    e = jnp.exp2(s - m)
    den = jnp.sum(e, axis=-1, keepdims=True)
    o = jnp.dot(e.astype(BF16), v.astype(BF16), preferred_element_type=F32)
    halves = lambda a: jnp.where(first, a[:ATT_BLOCK], a[ATT_BLOCK:])
    return halves(o), halves(m), halves(den)


def _attn_prompt_kernel(q_ref, kc_ref, kp_ref, vc_ref, vp_ref, bias_ref, o_ref, ko_ref, vo_ref,
                        kext, vext, oacc, macc, dacc, nat_ref):
    sb = pl.program_id(2)
    first_sb = (sb == 0).astype(jnp.int32)
    first = lax.broadcasted_iota(jnp.int32, (ATT_BLOCK, LANES), 1) < ATT_DH
    scale = (ATT_DH ** -0.5) * LOG2E
    ps = PREV_SLOTS

    for src_p, src_c, ext in ((kp_ref, kc_ref, kext), (vp_ref, vc_ref, vext)):
        for r in range(MAX_DIL):
            ext[r, 0:ps, :] = src_p[r * ATT_BLOCK + ATT_BLOCK - ps:(r + 1) * ATT_BLOCK, :]
            ext[r, ps:ps + ATT_BLOCK, :] = src_c[r * ATT_BLOCK:(r + 1) * ATT_BLOCK, :]

    def attend_store(p, variant, q_chunks, k, v):
        n = ATT_BLOCK // len(q_chunks)
        q = jnp.concatenate([q_ref[sl, :] for sl in q_chunks], axis=0) * scale
        o, m, den = _attend_pair(q, k, v, bias_ref[p, variant], first)
        for ci, sl in enumerate(q_chunks):
            oacc[p, sl, :] = o[ci * n:(ci + 1) * n]
            macc[p, sl, :] = m[ci * n:(ci + 1) * n]
            dacc[p, sl, :] = den[ci * n:(ci + 1) * n]

    n0 = ATT_BLOCK // MAX_DIL

    def block0(bi):
        base = pl.multiple_of(bi * n0, n0)
        ksl = pl.ds(base + ps - n0, 2 * n0)
        k = jnp.concatenate([kext[r, ksl, :] for r in range(MAX_DIL)], axis=0)
        v = jnp.concatenate([vext[r, ksl, :] for r in range(MAX_DIL)], axis=0)
        attend_store(0, first_sb * jnp.asarray(bi == 0, jnp.int32),
                     [pl.ds(r * ATT_BLOCK + base, n0) for r in range(MAX_DIL)], k, v)

    d1 = PATTERNS[1][1]
    nc1 = MAX_DIL // d1
    n1 = ATT_BLOCK // nc1

    def block1(idx):
        r4, mb = idx // (ATT_BLOCK // n1), idx % (ATT_BLOCK // n1)
        base = pl.multiple_of(mb * n1, n1)
        ksl = pl.ds(base + ps - n1, 2 * n1)
        k = jnp.concatenate([kext[r4 + d1 * c, ksl, :] for c in range(nc1)], axis=0)
        v = jnp.concatenate([vext[r4 + d1 * c, ksl, :] for c in range(nc1)], axis=0)
        attend_store(1, first_sb * jnp.asarray(mb == 0, jnp.int32),
                     [pl.ds(pl.multiple_of((r4 + d1 * c) * ATT_BLOCK + base, n1), n1) for c in range(nc1)], k, v)

    def block2(r):
        sl = pl.ds(pl.multiple_of(r * ATT_BLOCK, ATT_BLOCK), ATT_BLOCK)
        k = jnp.concatenate([kp_ref[sl, :], kc_ref[sl, :]], axis=0)
        v = jnp.concatenate([vp_ref[sl, :], vc_ref[sl, :]], axis=0)
        attend_store(2, first_sb, [sl], k, v)

    for block in (block0, block1, block2):
        def body(it, carry, block=block):
            for u in range(ATT_UNROLL):
                block(it * ATT_UNROLL + u)
            return carry

        lax.fori_loop(0, MAX_DIL // ATT_UNROLL, body, 0)

    def comb(r, carry):
        sl = pl.ds(pl.multiple_of(r * ATT_BLOCK, ATT_BLOCK), ATT_BLOCK)
        m0, m1, m2 = macc[0, sl, :], macc[1, sl, :], macc[2, sl, :]
        mx = jnp.maximum(jnp.maximum(m0, m1), m2)
        w0, w1, w2 = jnp.exp2(m0 - mx), jnp.exp2(m1 - mx), jnp.exp2(m2 - mx)
        num = w0 * oacc[0, sl, :] + w1 * oacc[1, sl, :] + w2 * oacc[2, sl, :]
        den = w0 * dacc[0, sl, :] + w1 * dacc[1, sl, :] + w2 * dacc[2, sl, :]
        o_ref[pl.ds(r, ATT_BLOCK, stride=MAX_DIL), :] = num / den
        return carry

    lax.fori_loop(0, SUPER // ATT_BLOCK, comb, 0)

    @pl.when(sb == pl.num_programs(2) - 1)
    def _():
        for src, dst in ((kc_ref, ko_ref), (vc_ref, vo_ref)):
            for r in range(MAX_DIL):
                nat_ref[pl.ds(r, ATT_BLOCK, stride=MAX_DIL), :] = src[r * ATT_BLOCK:(r + 1) * ATT_BLOCK, :]
            dst[...] = jnp.transpose(nat_ref[...])


def _attn_prompt(q, k, v):
    b, npair, t, _ = q.shape
    ns = t // SUPER
    bias = _band_bias()
    cur = pl.BlockSpec((None, None, SUPER, LANES), lambda i, p, s: (i, p, s, 0))
    prev = pl.BlockSpec((None, None, SUPER, LANES), lambda i, p, s: (i, p, jnp.maximum(s - 1, 0), 0))
    cache = pl.BlockSpec((None, LANES, SUPER), lambda i, p, s: (i, p, 0))
    cache_shape = jax.ShapeDtypeStruct((b, npair * LANES, SUPER), F32)
    return pl.pallas_call(
        _attn_prompt_kernel,
        grid=(b, npair, ns),
        in_specs=[cur, cur, prev, cur, prev,
                  pl.BlockSpec((len(PATTERNS), 2, None, 2 * ATT_BLOCK, 2 * ATT_BLOCK),
                               lambda i, p, s: (0, 0, p, 0, 0))],
        out_specs=[pl.BlockSpec((None, SUPER, LANES), lambda i, p, s: (i, s, p)), cache, cache],
        out_shape=[jax.ShapeDtypeStruct((b, t, npair * LANES), F32), cache_shape, cache_shape],
        scratch_shapes=[pltpu.VMEM((MAX_DIL, PREV_SLOTS + ATT_BLOCK, LANES), F32),
                        pltpu.VMEM((MAX_DIL, PREV_SLOTS + ATT_BLOCK, LANES), F32),
                        pltpu.VMEM((len(PATTERNS), SUPER, LANES), F32),
                        pltpu.VMEM((len(PATTERNS), SUPER, LANES), F32),
                        pltpu.VMEM((len(PATTERNS), SUPER, LANES), F32),
                        pltpu.VMEM((SUPER, LANES), F32)],
        compiler_params=_cparams(("arbitrary",) * 3),
        name="attn_prompt",
    )(q, k, k, v, v, bias)


QUAD = 4
NEW_PAD = 128


def _sample_bias(s_len, n_past):
    s = np.arange(s_len)
    col = np.arange(n_past + NEW_PAD)
    dist = (n_past + s)[:, None] - col[None, :]
    real = (col < n_past + s_len)[None, :]
    slopes = _alibi_slopes().reshape(ATT_HEADS // QUAD, QUAD, 1, 1)
    out = []
    for (w, d) in PATTERNS:
        valid = (dist >= 0) & (dist % d == 0) & (dist <= w) & real
        bias = np.where(valid[None, None], -slopes * dist.astype(np.float32)[None, None], np.float32(NEG))
        out.append(bias.reshape(ATT_HEADS // QUAD, QUAD * s_len, n_past + NEW_PAD))
    return np.stack(out).astype(np.float32)


def _attn_sample_kernel(q_ref, kn_ref, vn_ref, ck_ref, cv_ref, bias_ref, o_ref, *, s_len):
    n_past = ck_ref.shape[1]
    width = QUAD * ATT_DH
    rows = QUAD * s_len
    own = (lax.broadcasted_iota(jnp.int32, (rows, width), 0) // s_len
           == lax.broadcasted_iota(jnp.int32, (rows, width), 1) // ATT_DH)
    nt = (((1,), (1,)), ((), ()))
    for quad in range(q_ref.shape[1] // width):
        cols = slice(quad * width, (quad + 1) * width)
        q = q_ref[:, cols] * (ATT_DH ** -0.5)
        qm = jnp.where(own, jnp.concatenate([q] * QUAD, axis=0), 0.0).astype(BF16)
        kn = _pad_rows(kn_ref[:, cols], NEW_PAD).astype(BF16)
        vn = _pad_rows(vn_ref[:, cols], NEW_PAD).astype(BF16)
        s_c = jnp.dot(qm, ck_ref[cols, :].astype(BF16), preferred_element_type=F32)
        s_n = lax.dot_general(qm, kn, nt, preferred_element_type=F32)
        es_c, es_n, dens, lses = [], [], [], []
        for p in range(len(PATTERNS)):
            sc = s_c + bias_ref[p, quad, :, 0:n_past]
            sn = s_n + bias_ref[p, quad, :, n_past:n_past + NEW_PAD]
            m = jnp.maximum(jnp.max(sc, axis=-1, keepdims=True), jnp.max(sn, axis=-1, keepdims=True))
            ec, en = jnp.exp(sc - m), jnp.exp(sn - m)
            den = jnp.sum(ec, axis=-1, keepdims=True) + jnp.sum(en, axis=-1, keepdims=True)
            es_c.append(ec)
            es_n.append(en)
            dens.append(den)
            lses.append(m + jnp.log(den))
        mx = jnp.maximum(jnp.maximum(lses[0], lses[1]), lses[2])
        ws = [jnp.exp(l - mx) for l in lses]
        wsum = ws[0] + ws[1] + ws[2]
        coef = [w / (wsum * d) for w, d in zip(ws, dens)]
        c_c = coef[0] * es_c[0] + coef[1] * es_c[1] + coef[2] * es_c[2]
        c_n = coef[0] * es_n[0] + coef[1] * es_n[1] + coef[2] * es_n[2]
        o = (lax.dot_general(c_c.astype(BF16), cv_ref[cols, :].astype(BF16), nt, preferred_element_type=F32)
             + jnp.dot(c_n.astype(BF16), vn, preferred_element_type=F32))
        o = jnp.where(own, o, 0.0)
        acc = o[0:s_len]
        for hh in range(1, QUAD):
            acc = acc + o[hh * s_len:(hh + 1) * s_len]
        o_ref[:, cols] = acc.astype(o_ref.dtype)


def _attn_sample(q, kn, vn, ck, cv):
    b, s_len, d = q.shape
    n_past = ck.shape[2]
    bias = _sample_bias(s_len, n_past)
    new = pl.BlockSpec((None, s_len, d), lambda i: (i, 0, 0))
    old = pl.BlockSpec((None, d, n_past), lambda i: (i, 0, 0))
    return pl.pallas_call(
        functools.partial(_attn_sample_kernel, s_len=s_len),
        grid=(b,),
        in_specs=[new, new, new, old, old, _const_spec(bias.shape)],
        out_specs=new,
        out_shape=jax.ShapeDtypeStruct((b, s_len, d), BF16),
        compiler_params=_cparams(("arbitrary",)),
        name="attn_sample",
    )(q, kn, vn, ck, cv, bias)


def kernel(x_prompt, x_sample, state_ret, state_lru, state_conv, cache_k, cache_v, norm_mix, norm_ffn, w_in_rec, w_out_rec, ret_norm_g, conv_w, conv_b, w_rgate, b_rgate, w_igate, b_igate, lru_lambda, w_in_att, w_out_att, q_norm_g, k_norm_g, w_ffn_gate, w_ffn_up, w_ffn_down):
    bp, tp, d = x_prompt.shape
    bs, ts, _ = x_sample.shape
    depth = norm_mix.shape[0]
    assert tp % SUPER == 0 and d == D_MODEL
    yp, ys = x_prompt, x_sample
    ffn_stacked = (w_ffn_gate.astype(BF16), w_ffn_up.astype(BF16), w_ffn_down.astype(BF16))
    ret_p, lru_p, conv_p, kp, vp = [], [], [], [], []
    ret_s, lru_s, conv_s, ksl, vsl = [], [], [], [], []
    for l in range(depth):
        i = l // 2
        if l % 2 == 0:
            ffn = (w_out_rec[i].astype(BF16), norm_ffn[l], ffn_stacked, l)
            w_in = w_in_rec[i].astype(BF16)
            rec_prm = (ret_norm_g[i], conv_w[i], conv_b[i], w_rgate[i], b_rgate[i], w_igate[i], b_igate[i],
                       lru_lambda[i])
            yp, sr, sl, sc = _layer0_prompt(yp, norm_mix[l], w_in, jnp.zeros((bp, RET_HEADS, RET_DK, RET_DV), F32),
                                            jnp.zeros((bp, LRU_WIDTH), F32),
                                            jnp.zeros((bp, CONV_WIDTH - 1, LRU_WIDTH), F32), rec_prm, *ffn)
            ret_p.append(sr); lru_p.append(sl); conv_p.append(sc)
            proj = _norm_proj(ys.reshape(bs * ts, d), norm_mix[l], w_in).reshape(bs, ts, REC_IN)
            mix, sr, sl, sc = _rec_mix(proj, state_ret[i], state_lru[i], state_conv[i], *rec_prm)
            ys = _out_ffn(mix.reshape(bs * ts, d), ys.reshape(bs * ts, d), *ffn).reshape(bs, ts, d)
            ret_s.append(sr); lru_s.append(sl); conv_s.append(sc)
        else:
            ffn = (w_out_att[i].astype(BF16), norm_ffn[l], ffn_stacked, l)
            w_in = w_in_att[i].astype(BF16)
            q, k, v = _qkv_perm(yp, norm_mix[l], w_in, q_norm_g[i], k_norm_g[i])
            o, kc, vc = _attn_prompt(q, k, v)
            yp = _out_ffn(o.reshape(bp * tp, d), yp.reshape(bp * tp, d), *ffn).reshape(bp, tp, d)
            from_slabs = lambda c: jnp.transpose(c.reshape(bp, ATT_HEADS, ATT_DH, SUPER), (0, 3, 1, 2))
            kp.append(from_slabs(kc)); vp.append(from_slabs(vc))
            qs, kn, vn = _qkv_flat(ys.reshape(bs * ts, d), norm_mix[l], w_in, q_norm_g[i], k_norm_g[i])
            n_past = cache_k.shape[2]
            to_slabs = lambda c: jnp.transpose(c, (0, 2, 3, 1)).reshape(bs, d, n_past)
            o = _attn_sample(qs.reshape(bs, ts, d), kn.reshape(bs, ts, d), vn.reshape(bs, ts, d),
                             to_slabs(cache_k[i]), to_slabs(cache_v[i]))
            ys = _out_ffn(o.reshape(bs * ts, d), ys.reshape(bs * ts, d), *ffn).reshape(bs, ts, d)
            ksl.append(kn.reshape(bs, ts, ATT_HEADS, ATT_DH)); vsl.append(vn.reshape(bs, ts, ATT_HEADS, ATT_DH))
    return (yp, ys,
            jnp.stack(ret_p), jnp.stack(lru_p), jnp.stack(conv_p), jnp.stack(kp), jnp.stack(vp),
            jnp.stack(ret_s), jnp.stack(lru_s), jnp.stack(conv_s), jnp.stack(ksl), jnp.stack(vsl))
```

```python
import functools

import jax
import numpy as np
import jax.numpy as jnp
from jax import lax
from jax.experimental import pallas as pl
from jax.experimental.pallas import tpu as pltpu

F32 = jnp.float32
BF16 = jnp.bfloat16

EPS = 1e-6
D_MODEL = 1024
RET_HEADS = 4
RET_DK = 64
RET_DV = 128
RET_CHUNK = 128
LRU_WIDTH = 512
LRU_BLOCKS = 4
LRU_BLOCK = LRU_WIDTH // LRU_BLOCKS
CONV_WIDTH = 4
LRU_C = 8.0
GELU_C = 0.7978845608028654
REC_IN = 2 * RET_HEADS * RET_DK + 2 * RET_HEADS * RET_DV + 2 * LRU_WIDTH
ATT_HEADS = 16
ATT_DH = 64
PATTERNS = ((128, 1), (512, 4), (2048, 16))
ATT_BLOCK = 128
SUPER = 2048
MAX_DIL = 16
NEG = -1e30

LANES = 128
SUBLANES = 8
MXU_COLS = 256
NPAIR = D_MODEL // LANES
VMEM_LIMIT = 56 * 1024 * 1024

FF_CHUNK = 256
TOKEN_TILE = 512
WIDE_TILE = 1024
STREAM_ROWS = 512
REC_SEQS = 4


def _cparams(sem):
    return pltpu.CompilerParams(dimension_semantics=sem, vmem_limit_bytes=VMEM_LIMIT)


def _const_spec(shape):
    nd = len(shape)
    return pl.BlockSpec(shape, lambda *_: (0,) * nd, pipeline_mode=pl.Buffered(1))


def _rmsnorm(x, g):
    return x * lax.rsqrt(jnp.mean(x * x, axis=-1, keepdims=True) + EPS) * g


def _row_parts(rows):
    n = max(1, rows // STREAM_ROWS)
    return [slice(i * (rows // n), (i + 1) * (rows // n)) for i in range(n)]


def _norm_proj_kernel(x_ref, g_ref, w_ref, o_ref):
    for rs in _row_parts(x_ref.shape[0]):
        h = _rmsnorm(x_ref[rs, :], g_ref[...]).astype(BF16)
        o_ref[rs, :] = jnp.dot(h, w_ref[...], preferred_element_type=F32)


def _norm_proj(x2d, g, w_bf):
    m, d = x2d.shape
    n = w_bf.shape[1]
    tm = min(WIDE_TILE, m)
    return pl.pallas_call(
        _norm_proj_kernel,
        grid=(m // tm,),
        in_specs=[pl.BlockSpec((tm, d), lambda i: (i, 0)), _const_spec((1, d)), _const_spec((d, n))],
        out_specs=pl.BlockSpec((tm, n), lambda i: (i, 0)),
        out_shape=jax.ShapeDtypeStruct((m, n), F32),
        compiler_params=_cparams(("arbitrary",)),
        name="norm_proj",
    )(x2d, g.reshape(1, d), w_bf)


def _linear_scan(a, b, h0):
    rows, w = a.shape
    a = a.reshape(rows // SUBLANES, SUBLANES, w)
    b = b.reshape(rows // SUBLANES, SUBLANES, w)
    sub = lax.broadcasted_iota(jnp.int32, a.shape, 1)
    s = 1
    while s < SUBLANES:
        keep = sub >= s
        b = jnp.where(keep, a * pltpu.roll(b, s, axis=1), 0.0) + b
        a = jnp.where(keep, a * pltpu.roll(a, s, axis=1), a)
        s *= 2
    groups = []
    carry = h0
    for i in range(rows // SUBLANES):
        groups.append(b[i] + a[i] * carry)
        carry = groups[-1][SUBLANES - 1:SUBLANES]
    return jnp.concatenate(groups, axis=0)


def _pad_rows(x, rows):
    if x.shape[0] == rows:
        return x
    return jnp.concatenate([x, jnp.zeros((rows - x.shape[0], x.shape[1]), x.dtype)], axis=0)


def _mixer_init(s0_ref, h0_ref, c0_ref, s_scr, h_scr, xe_scr):
    s_scr[...] = s0_ref[...]
    h_scr[...] = h0_ref[...]
    xe_scr[...] = jnp.zeros(xe_scr.shape, F32)
    xe_scr[:, SUBLANES - (CONV_WIDTH - 1):SUBLANES, :] = c0_ref[...]


def _run_stages(*streams):
    for stages in streams:
        for stage in stages:
            stage()


def _run_alternating(a, b):
    ia = ib = 0
    while ia < len(a) or ib < len(b):
        if ib >= len(b) or (ia < len(a) and ia * len(b) <= ib * len(a)):
            a[ia]()
            ia += 1
        else:
            b[ib]()
            ib += 1


def _mixer(*args):
    _run_stages(_mixer_stages(*args))


def _mixer_stages(get, put, nseq, rows, tables, params, s_scr, h_scr, xe_scr, sret_ref, hlast_ref, cnew_ref):
    dec_ref, qdec_ref, kdec_ref, cdec_ref = tables
    retg_ref, convw_ref, convb_ref, wrg_ref, brg_ref, wig_ref, big_ref, lam_ref = params
    cp = RET_CHUNK
    hk = RET_HEADS * RET_DK
    v_off, g_off = 2 * hk, 2 * hk + RET_HEADS * RET_DV
    xl_off = g_off + RET_HEADS * RET_DV
    yl_off = xl_off + LRU_WIDTH
    lane = lax.broadcasted_iota(jnp.int32, (cp, LANES), 1)
    first = lane < RET_DK
    srow_first = lax.broadcasted_iota(jnp.int32, (2 * RET_DK, RET_DV), 0) < RET_DK
    nl = -lam_ref[...]
    neg_c_softplus = (-LRU_C) * (jnp.maximum(nl, 0.0) + jnp.log1p(jnp.exp(-jnp.abs(nl))))

    def retention(bi):
        for p in range(RET_HEADS // 2):
            q2 = _pad_rows(get(bi, slice(p * LANES, (p + 1) * LANES)), cp)
            k2 = _pad_rows(get(bi, slice(hk + p * LANES, hk + (p + 1) * LANES)), cp) * (RET_DK ** -0.5)
            kb = k2.astype(BF16)
            kdt = jnp.transpose(k2 * kdec_ref[p]).astype(BF16)
            s2 = s_scr[bi, p]
            s2b = s2.astype(BF16)
            new_s = []
            for hh in range(2):
                h = 2 * p + hh
                sel = first if hh == 0 else jnp.logical_not(first)
                qh = jnp.where(sel, q2, 0.0)
                v = _pad_rows(get(bi, slice(v_off + h * RET_DV, v_off + (h + 1) * RET_DV)), cp)
                vb = v.astype(BF16)
                att = lax.dot_general(qh.astype(BF16), kb, (((1,), (1,)), ((), ())),
                                      preferred_element_type=F32) * dec_ref[h]
                o = (jnp.dot(att.astype(BF16), vb, preferred_element_type=F32)
                     + jnp.dot((qh * qdec_ref[p]).astype(BF16), s2b, preferred_element_type=F32))
                new_s.append(jnp.dot(kdt, vb, preferred_element_type=F32))
                o = o[:rows]
                o = o * lax.rsqrt(jnp.mean(o * o, axis=-1, keepdims=True) + EPS) * retg_ref[h:h + 1, :]
                gate = get(bi, slice(g_off + h * RET_DV, g_off + (h + 1) * RET_DV))
                put(bi, slice(h * RET_DV, (h + 1) * RET_DV), o * (gate * jax.nn.sigmoid(gate)))
            s_scr[bi, p] = s2 * cdec_ref[p] + jnp.where(srow_first, new_s[0], new_s[1])

    def rg_lru(bi):
        x = get(bi, slice(xl_off, xl_off + LRU_WIDTH))
        xe = jnp.concatenate([xe_scr[bi], x], axis=0)
        xc = convb_ref[...]
        for j in range(CONV_WIDTH):
            back = CONV_WIDTH - 1 - j
            xj = x if back == 0 else pltpu.roll(xe, back, axis=0)[SUBLANES:]
            xc = xc + xj * convw_ref[j:j + 1, :]
        xe_scr[bi] = xe[rows:]
        cnew_ref[bi] = xe_scr[bi, pl.ds(SUBLANES - (CONV_WIDTH - 1), CONV_WIDTH - 1), :]

        xcb = _pad_rows(xc, max(rows, 2 * SUBLANES)).astype(BF16)
        r_parts, i_parts = [], []
        for n in range(LRU_BLOCKS):
            xb = xcb[:, n * LRU_BLOCK:(n + 1) * LRU_BLOCK]
            r_parts.append(jnp.dot(xb, wrg_ref[n], preferred_element_type=F32)[:rows])
            i_parts.append(jnp.dot(xb, wig_ref[n], preferred_element_type=F32)[:rows])
        r = jax.nn.sigmoid(jnp.concatenate(r_parts, axis=1) + brg_ref[...])
        ig = jax.nn.sigmoid(jnp.concatenate(i_parts, axis=1) + big_ref[...])
        a = jnp.exp(r * neg_c_softplus)
        one_m = 1.0 - a * a
        root = jnp.where(one_m > 0.0, one_m * lax.rsqrt(one_m), 0.0)
        hseq = _linear_scan(a, root * (ig * xc), h_scr[bi])
        h_scr[bi] = hseq[rows - 1:rows, :]
        hlast_ref[bi] = hseq[rows - 1:rows, :]
        yl = get(bi, slice(yl_off, yl_off + LRU_WIDTH))
        half = 0.5 * yl
        gelu = half + half * jnp.tanh(yl * (GELU_C + (0.044715 * GELU_C) * (yl * yl)))
        put(bi, slice(RET_HEADS * RET_DV, RET_HEADS * RET_DV + LRU_WIDTH), hseq * gelu)

    def finish():
        sret_ref[...] = s_scr[...]

    stages = []
    for bi in range(nseq):
        stages += [functools.partial(retention, bi), functools.partial(rg_lru, bi)]
    return stages + [finish]


def _rec_mix_kernel(proj_ref, s0_ref, h0_ref, c0_ref, dec_ref, qdec_ref, kdec_ref, cdec_ref,
                    retg_ref, convw_ref, convb_ref, wrg_ref, brg_ref, wig_ref, big_ref, lam_ref,
                    mix_ref, sret_ref, hlast_ref, cnew_ref,
                    s_scr, h_scr, xe_scr, *, rows):
    @pl.when(pl.program_id(1) == 0)
    def _():
        _mixer_init(s0_ref, h0_ref, c0_ref, s_scr, h_scr, xe_scr)

    def put(bi, cols, val):
        mix_ref[bi, :, cols] = val.astype(mix_ref.dtype)

    _mixer(lambda bi, cols: proj_ref[bi, :, cols], put, proj_ref.shape[0], rows,
           (dec_ref, qdec_ref, kdec_ref, cdec_ref),
           (retg_ref, convw_ref, convb_ref, wrg_ref, brg_ref, wig_ref, big_ref, lam_ref),
           s_scr, h_scr, xe_scr, sret_ref, hlast_ref, cnew_ref)


def _ret_tables(t_eff):
    cp = RET_CHUNK
    f32 = np.float32
    log_g = np.log1p(-np.power(f32(2.0), f32(-5.0) - np.arange(RET_HEADS, dtype=f32)))
    idx = np.arange(cp, dtype=f32)
    live = idx < t_eff
    diff = idx[:, None] - idx[None, :]
    ok = (diff >= 0) & live[:, None] & live[None, :]
    decay = np.where(ok, np.exp(log_g[:, None, None] * np.maximum(diff, f32(0.0))), f32(0.0))
    q_dec = np.where(live, np.exp(log_g[:, None] * (idx + f32(1.0))), f32(0.0))
    k_dec = np.where(live, np.exp(log_g[:, None] * (f32(t_eff - 1.0) - idx)), f32(0.0))
    c_dec = np.exp(log_g * f32(t_eff))

    def pair_lanes(t):
        t = t.reshape(RET_HEADS // 2, 2, cp)
        return np.repeat(np.transpose(t, (0, 2, 1)), RET_DK, axis=2)

    c_rows = np.repeat(c_dec.reshape(RET_HEADS // 2, 2), RET_DK, axis=1)
    c_rows = np.broadcast_to(c_rows[:, :, None], (RET_HEADS // 2, 2 * RET_DK, RET_DV))
    as32 = lambda t: np.ascontiguousarray(t, dtype=f32)
    return as32(decay), as32(pair_lanes(q_dec)), as32(pair_lanes(k_dec)), as32(c_rows)


def _rec_mix(proj, s0, h0, c0, ret_g, conv_w, conv_b, w_rg, b_rg, w_ig, b_ig, lam):
    b, t, _ = proj.shape
    rows = RET_CHUNK if t % RET_CHUNK == 0 else t
    assert t % rows == 0 and rows % SUBLANES == 0 and (t == rows or rows == RET_CHUNK)
    n = t // rows
    nb = REC_SEQS if b % REC_SEQS == 0 else 1
    dec, qdec, kdec, cdec = _ret_tables(rows)
    hp = RET_HEADS // 2
    w = LRU_WIDTH
    per_b3 = lambda i, c: (i, 0, 0)
    outs = pl.pallas_call(
        functools.partial(_rec_mix_kernel, rows=rows),
        grid=(b // nb, n),
        in_specs=[
            pl.BlockSpec((nb, rows, REC_IN), lambda i, c: (i, c, 0)),
            pl.BlockSpec((nb, hp, 2 * RET_DK, RET_DV), lambda i, c: (i, 0, 0, 0)),
            pl.BlockSpec((nb, 1, w), per_b3),
            pl.BlockSpec((nb, CONV_WIDTH - 1, w), per_b3),
            _const_spec(dec.shape), _const_spec(qdec.shape), _const_spec(kdec.shape), _const_spec(cdec.shape),
            _const_spec((RET_HEADS, RET_DV)), _const_spec((CONV_WIDTH, w)), _const_spec((1, w)),
            _const_spec((LRU_BLOCKS, LRU_BLOCK, LRU_BLOCK)), _const_spec((1, w)),
            _const_spec((LRU_BLOCKS, LRU_BLOCK, LRU_BLOCK)), _const_spec((1, w)), _const_spec((1, w)),
        ],
        out_specs=[
            pl.BlockSpec((nb, rows, D_MODEL), lambda i, c: (i, c, 0)),
            pl.BlockSpec((nb, hp, 2 * RET_DK, RET_DV), lambda i, c: (i, 0, 0, 0)),
            pl.BlockSpec((nb, 1, w), per_b3),
            pl.BlockSpec((nb, CONV_WIDTH - 1, w), per_b3),
        ],
        out_shape=[
            jax.ShapeDtypeStruct((b, t, D_MODEL), BF16),
            jax.ShapeDtypeStruct((b, hp, 2 * RET_DK, RET_DV), F32),
            jax.ShapeDtypeStruct((b, 1, w), F32),
            jax.ShapeDtypeStruct((b, CONV_WIDTH - 1, w), F32),
        ],
        scratch_shapes=[
            pltpu.VMEM((nb, hp, 2 * RET_DK, RET_DV), F32),
            pltpu.VMEM((nb, 1, w), F32),
            pltpu.VMEM((nb, SUBLANES, w), F32),
        ],
        compiler_params=_cparams(("arbitrary", "arbitrary")),
        name="rec_mix",
    )(proj, s0.reshape(b, hp, 2 * RET_DK, RET_DV), h0.reshape(b, 1, w), c0,
      dec, qdec, kdec, cdec, ret_g, conv_w, conv_b.reshape(1, w),
      w_rg.astype(BF16), b_rg.reshape(1, w), w_ig.astype(BF16), b_ig.reshape(1, w), lam.reshape(1, w))
    mix, s_ret, h_last, c_new = outs
    return mix, s_ret.reshape(b, RET_HEADS, RET_DK, RET_DV), h_last.reshape(b, w), c_new


def _ffn_stages(load_mix, load_x, store, ffn_refs, act_ref, rs):
    wo_ref, g_ref, wg_ref, wu_ref, wd_ref = ffn_refs
    live = {}

    def prologue():
        live["y"] = load_x() + jnp.dot(load_mix(), wo_ref[...], preferred_element_type=F32)
        live["h"] = _rmsnorm(live["y"], g_ref[...]).astype(BF16)

    def gate_up(c):
        cols = slice(c * FF_CHUNK, (c + 1) * FF_CHUNK)
        gate = jnp.dot(live["h"], wg_ref[:, cols], preferred_element_type=F32)
        up = jnp.dot(live["h"], wu_ref[:, cols], preferred_element_type=F32)
        act_ref[rs, cols] = (gate * jax.nn.sigmoid(gate) * up).astype(BF16)

    def down():
        store(live["y"] + jnp.dot(act_ref[rs, :], wd_ref[...], preferred_element_type=F32))

    return ([prologue] + [functools.partial(gate_up, c) for c in range(wg_ref.shape[1] // FF_CHUNK)] + [down])


def _out_ffn_kernel(mix_ref, x_ref, wo_ref, g_ref, wg_ref, wu_ref, wd_ref, o_ref, act_ref):
    def store_rows(rs, val):
        o_ref[rs, :] = val

    for rs in _row_parts(x_ref.shape[0]):
        _run_stages(_ffn_stages(lambda rs=rs: mix_ref[rs, :].astype(BF16), lambda rs=rs: x_ref[rs, :],
                                functools.partial(store_rows, rs),
                                (wo_ref, g_ref, wg_ref, wu_ref, wd_ref), act_ref, rs))


def _layer_spec(stacked, layer):
    shape = stacked.shape[1:]
    return pl.BlockSpec((None,) + shape, lambda *_: (layer,) + (0,) * len(shape), pipeline_mode=pl.Buffered(1))


def _out_ffn(mix2d, x2d, w_out, g, ffn_stacked, layer):
    m, d = x2d.shape
    tm = min(WIDE_TILE, m)
    row = lambda i: (i, 0)
    return pl.pallas_call(
        _out_ffn_kernel,
        grid=(m // tm,),
        in_specs=[pl.BlockSpec((tm, d), row), pl.BlockSpec((tm, d), row), _const_spec(w_out.shape),
                  _const_spec((1, d))] + [_layer_spec(w, layer) for w in ffn_stacked],
        out_specs=pl.BlockSpec((tm, d), row),
        out_shape=jax.ShapeDtypeStruct((m, d), F32),
        scratch_shapes=[pltpu.VMEM((tm, ffn_stacked[0].shape[2]), BF16)],
        compiler_params=_cparams(("arbitrary",)),
        name="out_ffn",
    )(mix2d, x2d, w_out, g.reshape(1, d), *ffn_stacked)


def _layer0_kernel(xc_ref, xp_ref, gm_ref, win_ref, s0_ref, h0_ref, c0_ref,
                   dec_ref, qdec_ref, kdec_ref, cdec_ref,
                   retg_ref, convw_ref, convb_ref, wrg_ref, brg_ref, wig_ref, big_ref, lam_ref,
                   wo_ref, g_ref, wg_ref, wu_ref, wd_ref,
                   y_ref, sret_ref, hlast_ref, cnew_ref,
                   s_scr, h_scr, xe_scr, mix_scr, act_scr):
    c = pl.program_id(1)
    last = pl.num_programs(1) - 1
    nseq, rows, d = xc_ref.shape
    tables = (dec_ref, qdec_ref, kdec_ref, cdec_ref)
    params = (retg_ref, convw_ref, convb_ref, wrg_ref, brg_ref, wig_ref, big_ref, lam_ref)
    ffn_refs = (wo_ref, g_ref, wg_ref, wu_ref, wd_ref)

    def mix_stages(slot):
        live = {}

        def project():
            x = xc_ref[...].reshape(nseq * rows, d)
            live["proj"] = jnp.dot(_rmsnorm(x, gm_ref[...]).astype(BF16), win_ref[...], preferred_element_type=F32)

        def put(bi, cols, val):
            mix_scr[slot, bi * rows:(bi + 1) * rows, cols] = val.astype(mix_scr.dtype)

        return [project] + _mixer_stages(lambda bi, cols: live["proj"][bi * rows:(bi + 1) * rows, cols], put,
                                         nseq, rows, tables, params, s_scr, h_scr, xe_scr,
                                         sret_ref, hlast_ref, cnew_ref)

    def ffn_stages(slot):
        def store(val):
            y_ref[...] = val.reshape(nseq, rows, d)

        return _ffn_stages(lambda: mix_scr[slot], lambda: xp_ref[...].reshape(nseq * rows, d), store,
                           ffn_refs, act_scr, slice(0, nseq * rows))

    @pl.when(c == 0)
    def _():
        _mixer_init(s0_ref, h0_ref, c0_ref, s_scr, h_scr, xe_scr)
        _run_stages(mix_stages(0))

    @pl.when(jnp.logical_and(c > 0, c < last))
    def _():
        _run_stages(mix_stages(c % 2), ffn_stages(1 - c % 2))

    @pl.when(c == last)
    def _():
        _run_stages(ffn_stages(1 - last % 2))


def _layer0_prompt(x, g_mix, w_in, s0, h0, c0, rec_prm, w_out, g_ffn, ffn_stacked, layer):
    ret_g, conv_w, conv_b, w_rg, b_rg, w_ig, b_ig, lam = rec_prm
    b, t, d = x.shape
    rows = RET_CHUNK
    assert t % rows == 0 and b % REC_SEQS == 0
    n = t // rows
    nb = REC_SEQS
    dec, qdec, kdec, cdec = _ret_tables(rows)
    hp = RET_HEADS // 2
    w = LRU_WIDTH
    per_b3 = lambda i, c: (i, 0, 0)
    chunk = (nb, rows, d)
    outs = pl.pallas_call(
        _layer0_kernel,
        grid=(b // nb, n + 1),
        in_specs=[
            pl.BlockSpec(chunk, lambda i, c: (i, jnp.minimum(c, n - 1), 0)),
            pl.BlockSpec(chunk, lambda i, c: (i, jnp.maximum(c - 1, 0), 0)),
            _const_spec((1, d)), _const_spec(w_in.shape),
            pl.BlockSpec((nb, hp, 2 * RET_DK, RET_DV), lambda i, c: (i, 0, 0, 0)),
            pl.BlockSpec((nb, 1, w), per_b3),
            pl.BlockSpec((nb, CONV_WIDTH - 1, w), per_b3),
            _const_spec(dec.shape), _const_spec(qdec.shape), _const_spec(kdec.shape), _const_spec(cdec.shape),
            _const_spec((RET_HEADS, RET_DV)), _const_spec((CONV_WIDTH, w)), _const_spec((1, w)),
            _const_spec((LRU_BLOCKS, LRU_BLOCK, LRU_BLOCK)), _const_spec((1, w)),
            _const_spec((LRU_BLOCKS, LRU_BLOCK, LRU_BLOCK)), _const_spec((1, w)), _const_spec((1, w)),
            _const_spec(w_out.shape), _const_spec((1, d)),
        ] + [_layer_spec(wt, layer) for wt in ffn_stacked],
        out_specs=[
            pl.BlockSpec(chunk, lambda i, c: (i, jnp.maximum(c - 1, 0), 0)),
            pl.BlockSpec((nb, hp, 2 * RET_DK, RET_DV), lambda i, c: (i, 0, 0, 0)),
            pl.BlockSpec((nb, 1, w), per_b3),
            pl.BlockSpec((nb, CONV_WIDTH - 1, w), per_b3),
        ],
        out_shape=[
            jax.ShapeDtypeStruct((b, t, d), F32),
            jax.ShapeDtypeStruct((b, hp, 2 * RET_DK, RET_DV), F32),
            jax.ShapeDtypeStruct((b, 1, w), F32),
            jax.ShapeDtypeStruct((b, CONV_WIDTH - 1, w), F32),
        ],
        scratch_shapes=[
            pltpu.VMEM((nb, hp, 2 * RET_DK, RET_DV), F32),
            pltpu.VMEM((nb, 1, w), F32),
            pltpu.VMEM((nb, SUBLANES, w), F32),
            pltpu.VMEM((2, nb * rows, d), BF16),
            pltpu.VMEM((nb * rows, ffn_stacked[0].shape[2]), BF16),
        ],
        compiler_params=_cparams(("arbitrary", "arbitrary")),
        name="layer0",
    )(x, x, g_mix.reshape(1, d), w_in, s0.reshape(b, hp, 2 * RET_DK, RET_DV), h0.reshape(b, 1, w), c0,
      dec, qdec, kdec, cdec, ret_g, conv_w, conv_b.reshape(1, w),
      w_rg.astype(BF16), b_rg.reshape(1, w), w_ig.astype(BF16), b_ig.reshape(1, w), lam.reshape(1, w),
      w_out, g_ffn.reshape(1, d), *ffn_stacked)
    y, s_ret, h_last, c_new = outs
    return y, s_ret.reshape(b, RET_HEADS, RET_DK, RET_DV), h_last.reshape(b, w), c_new


def _pair_rms(a, g2, first):
    sq = a * a
    s_a = jnp.sum(jnp.where(first, sq, 0.0), axis=-1, keepdims=True)
    s_b = jnp.sum(jnp.where(first, 0.0, sq), axis=-1, keepdims=True)
    ms = jnp.where(first, s_a, s_b) * (1.0 / ATT_DH)
    return a * lax.rsqrt(ms + EPS) * g2


def _qkv_slabs(x, g_ref, w_ref, qg_ref, kg_ref):
    h = _rmsnorm(x, g_ref[...]).astype(BF16)
    first = lax.broadcasted_iota(jnp.int32, (x.shape[0], LANES), 1) < ATT_DH
    slabs = []
    for c in range(w_ref.shape[1] // MXU_COLS):
        part = jnp.dot(h, w_ref[:, c * MXU_COLS:(c + 1) * MXU_COLS], preferred_element_type=F32)
        slabs += [part[:, i * LANES:(i + 1) * LANES] for i in range(MXU_COLS // LANES)]
    q = [_pair_rms(slabs[p], qg_ref[...], first) for p in range(NPAIR)]
    k = [_pair_rms(slabs[NPAIR + p], kg_ref[...], first) for p in range(NPAIR)]
    v = slabs[2 * NPAIR:]
    return q, k, v


def _qkv_flat_kernel(x_ref, g_ref, w_ref, qg_ref, kg_ref, q_ref, k_ref, v_ref):
    q, k, v = _qkv_slabs(x_ref[...], g_ref, w_ref, qg_ref, kg_ref)
    q_ref[...] = jnp.concatenate(q, axis=1)
    k_ref[...] = jnp.concatenate(k, axis=1)
    v_ref[...] = jnp.concatenate(v, axis=1)


def _qkv_perm_kernel(x_ref, g_ref, w_ref, qg_ref, kg_ref, q_ref, k_ref, v_ref, xs_ref):
    for p in range(NPAIR):
        xs_ref[p] = x_ref[:, p * LANES:(p + 1) * LANES]
    for rs in _row_parts(x_ref.shape[0]):
        n = (rs.stop - rs.start) // MAX_DIL
        x = jnp.concatenate(
            [jnp.concatenate([xs_ref[p, pl.ds(rs.start + r, n, stride=MAX_DIL), :] for p in range(NPAIR)], axis=1)
             for r in range(MAX_DIL)], axis=0)
        q, k, v = _qkv_slabs(x, g_ref, w_ref, qg_ref, kg_ref)
        dst = slice(rs.start // MAX_DIL, rs.stop // MAX_DIL)
        for ref, slabs in ((q_ref, q), (k_ref, k), (v_ref, v)):
            for p in range(NPAIR):
                for r in range(MAX_DIL):
                    ref[p, r, dst] = slabs[p][r * n:(r + 1) * n]


def _head_gain(g):
    return jnp.tile(g.reshape(1, ATT_DH), (1, LANES // ATT_DH))


def _qkv_flat(x2d, g, w_bf, qg, kg):
    m, d = x2d.shape
    tm = min(TOKEN_TILE, m)
    row = lambda i: (i, 0)
    return pl.pallas_call(
        _qkv_flat_kernel,
        grid=(m // tm,),
        in_specs=[pl.BlockSpec((tm, d), row), _const_spec((1, d)), _const_spec(w_bf.shape),
                  _const_spec((1, LANES)), _const_spec((1, LANES))],
        out_specs=[pl.BlockSpec((tm, d), row)] * 3,
        out_shape=[jax.ShapeDtypeStruct((m, d), F32)] * 3,
        compiler_params=_cparams(("arbitrary",)),
        name="qkv_flat",
    )(x2d, g.reshape(1, d), w_bf, _head_gain(qg), _head_gain(kg))


def _qkv_perm(x, g, w_bf, qg, kg):
    b, t, d = x.shape
    ns = t // SUPER
    tm = WIDE_TILE
    nt = SUPER // tm
    n = tm // MAX_DIL
    out_spec = pl.BlockSpec((None, NPAIR, None, MAX_DIL, None, n, LANES), lambda i, s, j: (i, 0, s, 0, j, 0, 0))
    out_shape = jax.ShapeDtypeStruct((b, NPAIR, ns, MAX_DIL, nt, n, LANES), F32)
    q, k, v = pl.pallas_call(
        _qkv_perm_kernel,
        grid=(b, ns, nt),
        in_specs=[pl.BlockSpec((None, tm, d), lambda i, s, j: (i, s * nt + j, 0)),
                  _const_spec((1, d)), _const_spec(w_bf.shape), _const_spec((1, LANES)), _const_spec((1, LANES))],
        out_specs=[out_spec] * 3,
        out_shape=[out_shape] * 3,
        scratch_shapes=[pltpu.VMEM((NPAIR, tm, LANES), F32)],
        compiler_params=_cparams(("arbitrary",) * 3),
        name="qkv_perm",
    )(x, g.reshape(1, d), w_bf, _head_gain(qg), _head_gain(kg))
    return tuple(a.reshape(b, NPAIR, t, LANES) for a in (q, k, v))


def _alibi_slopes():
    return np.power(np.float32(2.0), np.float32(-8.0) * np.arange(1, ATT_HEADS + 1, dtype=np.float32) / ATT_HEADS)


PREV_SLOTS = 32
ATT_UNROLL = 16
LOG2E = 1.4426950408889634


def _band_bias():
    i = np.arange(ATT_BLOCK)
    c = np.arange(2 * ATT_BLOCK)
    q_idx = {1: MAX_DIL * (i % 8) + i // 8,
             4: 4 * (i % 32) + i // 32,
             16: i}
    k_idx = {1: (MAX_DIL * (c % 8) + c // 16, (c // 8) % 2 == 0),
             4: (4 * (c % 32) + c // 64, (c // 32) % 2 == 0),
             16: (c % ATT_BLOCK, c < ATT_BLOCK)}
    slopes = _alibi_slopes().reshape(ATT_HEADS // 2, 2, 1, 1)
    neg = np.float32(NEG)
    out = []
    for (w, d) in PATTERNS:
        span = w // d
        km, is_prev = k_idx[d]
        diff = q_idx[d][:, None] - (km - np.where(is_prev, ATT_BLOCK, 0))[None, :]
        valid = (diff >= 0) & (diff <= span)
        bias = -slopes * (d * diff).astype(np.float32)[None, None]
        bias = bias * np.float32(LOG2E)
        variants = [np.where(valid, bias, neg), np.where(valid & ~is_prev[None, :], bias, neg)]
        out.append(np.stack([v.reshape(ATT_HEADS // 2, 2 * ATT_BLOCK, 2 * ATT_BLOCK) for v in variants]))
    return np.stack(out).astype(np.float32)


def _attend_pair(q, k, v, bias, first):
    qs = jnp.concatenate([jnp.where(first, q, 0.0), jnp.where(first, 0.0, q)], axis=0).astype(BF16)
    s = lax.dot_general(qs, k.astype(BF16), (((1,), (1,)), ((), ())), preferred_element_type=F32) + bias
    m = jnp.max(s, axis=-1, keepdims=True)
    e = jnp.exp2(s - m)
    den = jnp.sum(e, axis=-1, keepdims=True)
    o = jnp.dot(e.astype(BF16), v.astype(BF16), preferred_element_type=F32)
    halves = lambda a: jnp.where(first, a[:ATT_BLOCK], a[ATT_BLOCK:])
    return halves(o), halves(m), halves(den)


def _attn_prompt_kernel(q_ref, kc_ref, kp_ref, vc_ref, vp_ref, bias_ref, o_ref, ko_ref, vo_ref,
                        kext, vext, oacc, macc, dacc, nat_ref):
    sb = pl.program_id(2)
    first_sb = (sb == 0).astype(jnp.int32)
    first = lax.broadcasted_iota(jnp.int32, (ATT_BLOCK, LANES), 1) < ATT_DH
    scale = (ATT_DH ** -0.5) * LOG2E
    ps = PREV_SLOTS

    for src_p, src_c, ext in ((kp_ref, kc_ref, kext), (vp_ref, vc_ref, vext)):
        for r in range(MAX_DIL):
            ext[r, 0:ps, :] = src_p[r * ATT_BLOCK + ATT_BLOCK - ps:(r + 1) * ATT_BLOCK, :]
            ext[r, ps:ps + ATT_BLOCK, :] = src_c[r * ATT_BLOCK:(r + 1) * ATT_BLOCK, :]

    def attend_store(p, variant, q_chunks, k, v):
        n = ATT_BLOCK // len(q_chunks)
        q = jnp.concatenate([q_ref[sl, :] for sl in q_chunks], axis=0) * scale
        o, m, den = _attend_pair(q, k, v, bias_ref[p, variant], first)
        for ci, sl in enumerate(q_chunks):
            oacc[p, sl, :] = o[ci * n:(ci + 1) * n]
            macc[p, sl, :] = m[ci * n:(ci + 1) * n]
            dacc[p, sl, :] = den[ci * n:(ci + 1) * n]

    n0 = ATT_BLOCK // MAX_DIL

    def block0(bi):
        base = pl.multiple_of(bi * n0, n0)
        ksl = pl.ds(base + ps - n0, 2 * n0)
        k = jnp.concatenate([kext[r, ksl, :] for r in range(MAX_DIL)], axis=0)
        v = jnp.concatenate([vext[r, ksl, :] for r in range(MAX_DIL)], axis=0)
        attend_store(0, first_sb * jnp.asarray(bi == 0, jnp.int32),
                     [pl.ds(r * ATT_BLOCK + base, n0) for r in range(MAX_DIL)], k, v)

    d1 = PATTERNS[1][1]
    nc1 = MAX_DIL // d1
    n1 = ATT_BLOCK // nc1

    def block1(idx):
        r4, mb = idx // (ATT_BLOCK // n1), idx % (ATT_BLOCK // n1)
        base = pl.multiple_of(mb * n1, n1)
        ksl = pl.ds(base + ps - n1, 2 * n1)
        k = jnp.concatenate([kext[r4 + d1 * c, ksl, :] for c in range(nc1)], axis=0)
        v = jnp.concatenate([vext[r4 + d1 * c, ksl, :] for c in range(nc1)], axis=0)
        attend_store(1, first_sb * jnp.asarray(mb == 0, jnp.int32),
                     [pl.ds(pl.multiple_of((r4 + d1 * c) * ATT_BLOCK + base, n1), n1) for c in range(nc1)], k, v)

    def block2(r):
        sl = pl.ds(pl.multiple_of(r * ATT_BLOCK, ATT_BLOCK), ATT_BLOCK)
        k = jnp.concatenate([kp_ref[sl, :], kc_ref[sl, :]], axis=0)
        v = jnp.concatenate([vp_ref[sl, :], vc_ref[sl, :]], axis=0)
        attend_store(2, first_sb, [sl], k, v)

    for block in (block0, block1, block2):
        def body(it, carry, block=block):
            for u in range(ATT_UNROLL):
                block(it * ATT_UNROLL + u)
            return carry

        lax.fori_loop(0, MAX_DIL // ATT_UNROLL, body, 0)

    def comb(r, carry):
        sl = pl.ds(pl.multiple_of(r * ATT_BLOCK, ATT_BLOCK), ATT_BLOCK)
        m0, m1, m2 = macc[0, sl, :], macc[1, sl, :], macc[2, sl, :]
        mx = jnp.maximum(jnp.maximum(m0, m1), m2)
        w0, w1, w2 = jnp.exp2(m0 - mx), jnp.exp2(m1 - mx), jnp.exp2(m2 - mx)
        num = w0 * oacc[0, sl, :] + w1 * oacc[1, sl, :] + w2 * oacc[2, sl, :]
        den = w0 * dacc[0, sl, :] + w1 * dacc[1, sl, :] + w2 * dacc[2, sl, :]
        o_ref[pl.ds(r, ATT_BLOCK, stride=MAX_DIL), :] = num / den
        return carry

    lax.fori_loop(0, SUPER // ATT_BLOCK, comb, 0)

    @pl.when(sb == pl.num_programs(2) - 1)
    def _():
        for src, dst in ((kc_ref, ko_ref), (vc_ref, vo_ref)):
            for r in range(MAX_DIL):
                nat_ref[pl.ds(r, ATT_BLOCK, stride=MAX_DIL), :] = src[r * ATT_BLOCK:(r + 1) * ATT_BLOCK, :]
            dst[...] = jnp.transpose(nat_ref[...])


def _attn_prompt(q, k, v):
    b, npair, t, _ = q.shape
    ns = t // SUPER
    bias = _band_bias()
    cur = pl.BlockSpec((None, None, SUPER, LANES), lambda i, p, s: (i, p, s, 0))
    prev = pl.BlockSpec((None, None, SUPER, LANES), lambda i, p, s: (i, p, jnp.maximum(s - 1, 0), 0))
    cache = pl.BlockSpec((None, LANES, SUPER), lambda i, p, s: (i, p, 0))
    cache_shape = jax.ShapeDtypeStruct((b, npair * LANES, SUPER), F32)
    return pl.pallas_call(
        _attn_prompt_kernel,
        grid=(b, npair, ns),
        in_specs=[cur, cur, prev, cur, prev,
                  pl.BlockSpec((len(PATTERNS), 2, None, 2 * ATT_BLOCK, 2 * ATT_BLOCK),
                               lambda i, p, s: (0, 0, p, 0, 0))],
        out_specs=[pl.BlockSpec((None, SUPER, LANES), lambda i, p, s: (i, s, p)), cache, cache],
        out_shape=[jax.ShapeDtypeStruct((b, t, npair * LANES), F32), cache_shape, cache_shape],
        scratch_shapes=[pltpu.VMEM((MAX_DIL, PREV_SLOTS + ATT_BLOCK, LANES), F32),
                        pltpu.VMEM((MAX_DIL, PREV_SLOTS + ATT_BLOCK, LANES), F32),
                        pltpu.VMEM((len(PATTERNS), SUPER, LANES), F32),
                        pltpu.VMEM((len(PATTERNS), SUPER, LANES), F32),
                        pltpu.VMEM((len(PATTERNS), SUPER, LANES), F32),
                        pltpu.VMEM((SUPER, LANES), F32)],
        compiler_params=_cparams(("arbitrary",) * 3),
        name="attn_prompt",
    )(q, k, k, v, v, bias)


QUAD = 4
NEW_PAD = 128


def _sample_bias(s_len, n_past):
    s = np.arange(s_len)
    col = np.arange(n_past + NEW_PAD)
    dist = (n_past + s)[:, None] - col[None, :]
    real = (col < n_past + s_len)[None, :]
    slopes = _alibi_slopes().reshape(ATT_HEADS // QUAD, QUAD, 1, 1)
    out = []
    for (w, d) in PATTERNS:
        valid = (dist >= 0) & (dist % d == 0) & (dist <= w) & real
        bias = np.where(valid[None, None], -slopes * dist.astype(np.float32)[None, None], np.float32(NEG))
        out.append(bias.reshape(ATT_HEADS // QUAD, QUAD * s_len, n_past + NEW_PAD))
    return np.stack(out).astype(np.float32)


def _sample_stages(q_ref, kn_ref, vn_ref, ck_ref, cv_ref, bias_ref, o_ref, quad0, s_len):
    n_past = ck_ref.shape[1]
    width = QUAD * ATT_DH
    rows = QUAD * s_len
    nt = (((1,), (1,)), ((), ()))

    def own():
        return (lax.broadcasted_iota(jnp.int32, (rows, width), 0) // s_len
                == lax.broadcasted_iota(jnp.int32, (rows, width), 1) // ATT_DH)

    def scores(ql, live):
        cols = slice(ql * width, (ql + 1) * width)
        q = q_ref[:, cols] * (ATT_DH ** -0.5)
        qm = jnp.where(own(), jnp.concatenate([q] * QUAD, axis=0), 0.0).astype(BF16)
        kn = _pad_rows(kn_ref[:, cols], NEW_PAD).astype(BF16)
        live["s_c"] = jnp.dot(qm, ck_ref[cols, :].astype(BF16), preferred_element_type=F32)
        live["s_n"] = lax.dot_general(qm, kn, nt, preferred_element_type=F32)
        live["parts"] = []

    def pattern(ql, p, live):
        sc = live["s_c"] + bias_ref[p, quad0 + ql, :, 0:n_past]
        sn = live["s_n"] + bias_ref[p, quad0 + ql, :, n_past:n_past + NEW_PAD]
        m = jnp.maximum(jnp.max(sc, axis=-1, keepdims=True), jnp.max(sn, axis=-1, keepdims=True))
        ec, en = jnp.exp(sc - m), jnp.exp(sn - m)
        den = jnp.sum(ec, axis=-1, keepdims=True) + jnp.sum(en, axis=-1, keepdims=True)
        live["parts"].append((ec, en, den, m + jnp.log(den)))

    def combine(ql, live):
        cols = slice(ql * width, (ql + 1) * width)
        es_c, es_n, dens, lses = zip(*live["parts"])
        mx = jnp.maximum(jnp.maximum(lses[0], lses[1]), lses[2])
        ws = [jnp.exp(l - mx) for l in lses]
        wsum = ws[0] + ws[1] + ws[2]
        coef = [w / (wsum * d) for w, d in zip(ws, dens)]
        c_c = coef[0] * es_c[0] + coef[1] * es_c[1] + coef[2] * es_c[2]
        c_n = coef[0] * es_n[0] + coef[1] * es_n[1] + coef[2] * es_n[2]
        vn = _pad_rows(vn_ref[:, cols], NEW_PAD).astype(BF16)
        o = (lax.dot_general(c_c.astype(BF16), cv_ref[cols, :].astype(BF16), nt, preferred_element_type=F32)
             + jnp.dot(c_n.astype(BF16), vn, preferred_element_type=F32))
        o = jnp.where(own(), o, 0.0)
        acc = o[0:s_len]
        for hh in range(1, QUAD):
            acc = acc + o[hh * s_len:(hh + 1) * s_len]
        o_ref[:, cols] = acc.astype(o_ref.dtype)

    stages = []
    for ql in range(q_ref.shape[1] // width):
        live = {}
        stages.append(functools.partial(scores, ql, live))
        stages += [functools.partial(pattern, ql, p, live) for p in range(len(PATTERNS))]
        stages.append(functools.partial(combine, ql, live))
    return stages


SAMPLE_QUADS = 2


def _ffn_sample_kernel(mix_ref, x_ref, wo_ref, g_ref, wg_ref, wu_ref, wd_ref,
                       q_ref, kn_ref, vn_ref, ck_ref, cv_ref, bias_ref, o_ref, os_ref, act_ref, *, s_len):
    def store(val):
        o_ref[...] = val

    steps_per_seq = bias_ref.shape[1] // SAMPLE_QUADS
    _run_alternating(
        _ffn_stages(lambda: mix_ref[...].astype(BF16), lambda: x_ref[...], store,
                    (wo_ref, g_ref, wg_ref, wu_ref, wd_ref), act_ref, slice(0, x_ref.shape[0])),
        _sample_stages(q_ref, kn_ref, vn_ref, ck_ref, cv_ref, bias_ref, os_ref,
                       (pl.program_id(0) % steps_per_seq) * SAMPLE_QUADS, s_len))


def _out_ffn_and_sample_attn(mix2d, x2d, w_out, g, ffn_stacked, layer, q, kn, vn, ck, cv):
    m, d = x2d.shape
    b, s_len, _ = q.shape
    n_past = ck.shape[2]
    width = SAMPLE_QUADS * QUAD * ATT_DH
    sps = d // width
    steps = b * sps
    assert m % steps == 0 and (m // steps) % SUBLANES == 0
    tm = m // steps
    bias = _sample_bias(s_len, n_past)
    row = lambda i: (i, 0)
    new = pl.BlockSpec((None, s_len, width), lambda i: (i // sps, 0, i % sps))
    old = pl.BlockSpec((None, width, n_past), lambda i: (i // sps, i % sps, 0))
    return pl.pallas_call(
        functools.partial(_ffn_sample_kernel, s_len=s_len),
        grid=(steps,),
        in_specs=[pl.BlockSpec((tm, d), row), pl.BlockSpec((tm, d), row), _const_spec(w_out.shape),
                  _const_spec((1, d))] + [_layer_spec(w, layer) for w in ffn_stacked]
                 + [new, new, new, old, old, _const_spec(bias.shape)],
        out_specs=[pl.BlockSpec((tm, d), row), new],
        out_shape=[jax.ShapeDtypeStruct((m, d), F32), jax.ShapeDtypeStruct((b, s_len, d), BF16)],
        scratch_shapes=[pltpu.VMEM((tm, ffn_stacked[0].shape[2]), BF16)],
        compiler_params=_cparams(("arbitrary",)),
        name="ffn_sample_attn",
    )(mix2d, x2d, w_out, g.reshape(1, d), *ffn_stacked, q, kn, vn, ck, cv, bias)


def kernel(x_prompt, x_sample, state_ret, state_lru, state_conv, cache_k, cache_v, norm_mix, norm_ffn, w_in_rec, w_out_rec, ret_norm_g, conv_w, conv_b, w_rgate, b_rgate, w_igate, b_igate, lru_lambda, w_in_att, w_out_att, q_norm_g, k_norm_g, w_ffn_gate, w_ffn_up, w_ffn_down):
    bp, tp, d = x_prompt.shape
    bs, ts, _ = x_sample.shape
    depth = norm_mix.shape[0]
    assert tp % SUPER == 0 and d == D_MODEL
    yp, ys = x_prompt, x_sample
    ffn_stacked = (w_ffn_gate.astype(BF16), w_ffn_up.astype(BF16), w_ffn_down.astype(BF16))
    ret_p, lru_p, conv_p, kp, vp = [], [], [], [], []
    ret_s, lru_s, conv_s, ksl, vsl = [], [], [], [], []
    for l in range(depth):
        i = l // 2
        if l % 2 == 0:
            ffn = (w_out_rec[i].astype(BF16), norm_ffn[l], ffn_stacked, l)
            w_in = w_in_rec[i].astype(BF16)
            rec_prm = (ret_norm_g[i], conv_w[i], conv_b[i], w_rgate[i], b_rgate[i], w_igate[i], b_igate[i],
                       lru_lambda[i])
            yp, sr, sl, sc = _layer0_prompt(yp, norm_mix[l], w_in, jnp.zeros((bp, RET_HEADS, RET_DK, RET_DV), F32),
                                            jnp.zeros((bp, LRU_WIDTH), F32),
                                            jnp.zeros((bp, CONV_WIDTH - 1, LRU_WIDTH), F32), rec_prm, *ffn)
            ret_p.append(sr); lru_p.append(sl); conv_p.append(sc)
            proj = _norm_proj(ys.reshape(bs * ts, d), norm_mix[l], w_in).reshape(bs, ts, REC_IN)
            mix, sr, sl, sc = _rec_mix(proj, state_ret[i], state_lru[i], state_conv[i], *rec_prm)
            ys = _out_ffn(mix.reshape(bs * ts, d), ys.reshape(bs * ts, d), *ffn).reshape(bs, ts, d)
            ret_s.append(sr); lru_s.append(sl); conv_s.append(sc)
        else:
            ffn = (w_out_att[i].astype(BF16), norm_ffn[l], ffn_stacked, l)
            w_in = w_in_att[i].astype(BF16)
            q, k, v = _qkv_perm(yp, norm_mix[l], w_in, q_norm_g[i], k_norm_g[i])
            o, kc, vc = _attn_prompt(q, k, v)
            from_slabs = lambda c: jnp.transpose(c.reshape(bp, ATT_HEADS, ATT_DH, SUPER), (0, 3, 1, 2))
            kp.append(from_slabs(kc)); vp.append(from_slabs(vc))
            qs, kn, vn = _qkv_flat(ys.reshape(bs * ts, d), norm_mix[l], w_in, q_norm_g[i], k_norm_g[i])
            n_past = cache_k.shape[2]
            to_slabs = lambda c: jnp.transpose(c, (0, 2, 3, 1)).reshape(bs, d, n_past)
            yp, os_ = _out_ffn_and_sample_attn(
                o.reshape(bp * tp, d), yp.reshape(bp * tp, d), *ffn,
                qs.reshape(bs, ts, d), kn.reshape(bs, ts, d), vn.reshape(bs, ts, d),
                to_slabs(cache_k[i]), to_slabs(cache_v[i]))
            yp = yp.reshape(bp, tp, d)
            ys = _out_ffn(os_.reshape(bs * ts, d), ys.reshape(bs * ts, d), *ffn).reshape(bs, ts, d)
            ksl.append(kn.reshape(bs, ts, ATT_HEADS, ATT_DH)); vsl.append(vn.reshape(bs, ts, ATT_HEADS, ATT_DH))
    return (yp, ys,
            jnp.stack(ret_p), jnp.stack(lru_p), jnp.stack(conv_p), jnp.stack(kp), jnp.stack(vp),
            jnp.stack(ret_s), jnp.stack(lru_s), jnp.stack(conv_s), jnp.stack(ksl), jnp.stack(vsl))
```

```python
import functools

import jax
import numpy as np
import jax.numpy as jnp
from jax import lax
from jax.experimental import pallas as pl
from jax.experimental.pallas import tpu as pltpu

F32 = jnp.float32
BF16 = jnp.bfloat16

EPS = 1e-6
D_MODEL = 1024
RET_HEADS = 4
RET_DK = 64
RET_DV = 128
RET_CHUNK = 128
LRU_WIDTH = 512
LRU_BLOCKS = 4
LRU_BLOCK = LRU_WIDTH // LRU_BLOCKS
CONV_WIDTH = 4
LRU_C = 8.0
GELU_C = 0.7978845608028654
REC_IN = 2 * RET_HEADS * RET_DK + 2 * RET_HEADS * RET_DV + 2 * LRU_WIDTH
ATT_HEADS = 16
ATT_DH = 64
PATTERNS = ((128, 1), (512, 4), (2048, 16))
ATT_BLOCK = 128
SUPER = 2048
MAX_DIL = 16
NEG = -1e30

LANES = 128
SUBLANES = 8
MXU_COLS = 256
NPAIR = D_MODEL // LANES
VMEM_LIMIT = 56 * 1024 * 1024

FF_CHUNK = 256
TOKEN_TILE = 512
WIDE_TILE = 1024
STREAM_ROWS = 512
REC_SEQS = 4


def _cparams(sem):
    return pltpu.CompilerParams(dimension_semantics=sem, vmem_limit_bytes=VMEM_LIMIT)


def _const_spec(shape):
    nd = len(shape)
    return pl.BlockSpec(shape, lambda *_: (0,) * nd, pipeline_mode=pl.Buffered(1))


def _rmsnorm(x, g):
    return x * lax.rsqrt(jnp.mean(x * x, axis=-1, keepdims=True) + EPS) * g


def _row_parts(rows):
    n = max(1, rows // STREAM_ROWS)
    return [slice(i * (rows // n), (i + 1) * (rows // n)) for i in range(n)]


def _norm_proj_kernel(x_ref, g_ref, w_ref, o_ref):
    for rs in _row_parts(x_ref.shape[0]):
        h = _rmsnorm(x_ref[rs, :], g_ref[...]).astype(BF16)
        o_ref[rs, :] = jnp.dot(h, w_ref[...], preferred_element_type=F32)


def _norm_proj(x2d, g, w_bf):
    m, d = x2d.shape
    n = w_bf.shape[1]
    tm = min(WIDE_TILE, m)
    return pl.pallas_call(
        _norm_proj_kernel,
        grid=(m // tm,),
        in_specs=[pl.BlockSpec((tm, d), lambda i: (i, 0)), _const_spec((1, d)), _const_spec((d, n))],
        out_specs=pl.BlockSpec((tm, n), lambda i: (i, 0)),
        out_shape=jax.ShapeDtypeStruct((m, n), F32),
        compiler_params=_cparams(("arbitrary",)),
        name="norm_proj",
    )(x2d, g.reshape(1, d), w_bf)


def _linear_scan(a, b, h0):
    rows, w = a.shape
    a = a.reshape(rows // SUBLANES, SUBLANES, w)
    b = b.reshape(rows // SUBLANES, SUBLANES, w)
    sub = lax.broadcasted_iota(jnp.int32, a.shape, 1)
    s = 1
    while s < SUBLANES:
        keep = sub >= s
        b = jnp.where(keep, a * pltpu.roll(b, s, axis=1), 0.0) + b
        a = jnp.where(keep, a * pltpu.roll(a, s, axis=1), a)
        s *= 2
    groups = []
    carry = h0
    for i in range(rows // SUBLANES):
        groups.append(b[i] + a[i] * carry)
        carry = groups[-1][SUBLANES - 1:SUBLANES]
    return jnp.concatenate(groups, axis=0)


def _pad_rows(x, rows):
    if x.shape[0] == rows:
        return x
    return jnp.concatenate([x, jnp.zeros((rows - x.shape[0], x.shape[1]), x.dtype)], axis=0)


def _mixer_init(s0_ref, h0_ref, c0_ref, s_scr, h_scr, xe_scr):
    s_scr[...] = s0_ref[...]
    h_scr[...] = h0_ref[...]
    xe_scr[...] = jnp.zeros(xe_scr.shape, F32)
    xe_scr[:, SUBLANES - (CONV_WIDTH - 1):SUBLANES, :] = c0_ref[...]


def _run_stages(*streams):
    for stages in streams:
        for stage in stages:
            stage()


def _run_alternating(a, b):
    ia = ib = 0
    while ia < len(a) or ib < len(b):
        if ib >= len(b) or (ia < len(a) and ia * len(b) <= ib * len(a)):
            a[ia]()
            ia += 1
        else:
            b[ib]()
            ib += 1


def _mixer(*args):
    _run_stages(_mixer_stages(*args))


def _mixer_stages(get, put, nseq, rows, tables, params, s_scr, h_scr, xe_scr, sret_ref, hlast_ref, cnew_ref):
    dec_ref, qdec_ref, kdec_ref, cdec_ref = tables
    retg_ref, convw_ref, convb_ref, wrg_ref, brg_ref, wig_ref, big_ref, lam_ref = params
    cp = RET_CHUNK
    hk = RET_HEADS * RET_DK
    v_off, g_off = 2 * hk, 2 * hk + RET_HEADS * RET_DV
    xl_off = g_off + RET_HEADS * RET_DV
    yl_off = xl_off + LRU_WIDTH
    lane = lax.broadcasted_iota(jnp.int32, (cp, LANES), 1)
    first = lane < RET_DK
    srow_first = lax.broadcasted_iota(jnp.int32, (2 * RET_DK, RET_DV), 0) < RET_DK
    nl = -lam_ref[...]
    neg_c_softplus = (-LRU_C) * (jnp.maximum(nl, 0.0) + jnp.log1p(jnp.exp(-jnp.abs(nl))))

    def retention(bi):
        for p in range(RET_HEADS // 2):
            q2 = _pad_rows(get(bi, slice(p * LANES, (p + 1) * LANES)), cp)
            k2 = _pad_rows(get(bi, slice(hk + p * LANES, hk + (p + 1) * LANES)), cp) * (RET_DK ** -0.5)
            kb = k2.astype(BF16)
            kdt = jnp.transpose(k2 * kdec_ref[p]).astype(BF16)
            s2 = s_scr[bi, p]
            s2b = s2.astype(BF16)
            new_s = []
            for hh in range(2):
                h = 2 * p + hh
                sel = first if hh == 0 else jnp.logical_not(first)
                qh = jnp.where(sel, q2, 0.0)
                v = _pad_rows(get(bi, slice(v_off + h * RET_DV, v_off + (h + 1) * RET_DV)), cp)
                vb = v.astype(BF16)
                att = lax.dot_general(qh.astype(BF16), kb, (((1,), (1,)), ((), ())),
                                      preferred_element_type=F32) * dec_ref[h]
                o = (jnp.dot(att.astype(BF16), vb, preferred_element_type=F32)
                     + jnp.dot((qh * qdec_ref[p]).astype(BF16), s2b, preferred_element_type=F32))
                new_s.append(jnp.dot(kdt, vb, preferred_element_type=F32))
                o = o[:rows]
                o = o * lax.rsqrt(jnp.mean(o * o, axis=-1, keepdims=True) + EPS) * retg_ref[h:h + 1, :]
                gate = get(bi, slice(g_off + h * RET_DV, g_off + (h + 1) * RET_DV))
                put(bi, slice(h * RET_DV, (h + 1) * RET_DV), o * (gate * jax.nn.sigmoid(gate)))
            s_scr[bi, p] = s2 * cdec_ref[p] + jnp.where(srow_first, new_s[0], new_s[1])

    def rg_lru(bi):
        x = get(bi, slice(xl_off, xl_off + LRU_WIDTH))
        xe = jnp.concatenate([xe_scr[bi], x], axis=0)
        xc = convb_ref[...]
        for j in range(CONV_WIDTH):
            back = CONV_WIDTH - 1 - j
            xj = x if back == 0 else pltpu.roll(xe, back, axis=0)[SUBLANES:]
            xc = xc + xj * convw_ref[j:j + 1, :]
        xe_scr[bi] = xe[rows:]
        cnew_ref[bi] = xe_scr[bi, pl.ds(SUBLANES - (CONV_WIDTH - 1), CONV_WIDTH - 1), :]

        xcb = _pad_rows(xc, max(rows, 2 * SUBLANES)).astype(BF16)
        r_parts, i_parts = [], []
        for n in range(LRU_BLOCKS):
            xb = xcb[:, n * LRU_BLOCK:(n + 1) * LRU_BLOCK]
            r_parts.append(jnp.dot(xb, wrg_ref[n], preferred_element_type=F32)[:rows])
            i_parts.append(jnp.dot(xb, wig_ref[n], preferred_element_type=F32)[:rows])
        r = jax.nn.sigmoid(jnp.concatenate(r_parts, axis=1) + brg_ref[...])
        ig = jax.nn.sigmoid(jnp.concatenate(i_parts, axis=1) + big_ref[...])
        a = jnp.exp(r * neg_c_softplus)
        one_m = 1.0 - a * a
        root = jnp.where(one_m > 0.0, one_m * lax.rsqrt(one_m), 0.0)
        hseq = _linear_scan(a, root * (ig * xc), h_scr[bi])
        h_scr[bi] = hseq[rows - 1:rows, :]
        hlast_ref[bi] = hseq[rows - 1:rows, :]
        yl = get(bi, slice(yl_off, yl_off + LRU_WIDTH))
        half = 0.5 * yl
        gelu = half + half * jnp.tanh(yl * (GELU_C + (0.044715 * GELU_C) * (yl * yl)))
        put(bi, slice(RET_HEADS * RET_DV, RET_HEADS * RET_DV + LRU_WIDTH), hseq * gelu)

    def finish():
        sret_ref[...] = s_scr[...]

    stages = []
    for bi in range(nseq):
        stages += [functools.partial(retention, bi), functools.partial(rg_lru, bi)]
    return stages + [finish]


def _rec_mix_kernel(proj_ref, s0_ref, h0_ref, c0_ref, dec_ref, qdec_ref, kdec_ref, cdec_ref,
                    retg_ref, convw_ref, convb_ref, wrg_ref, brg_ref, wig_ref, big_ref, lam_ref,
                    mix_ref, sret_ref, hlast_ref, cnew_ref,
                    s_scr, h_scr, xe_scr, *, rows):
    @pl.when(pl.program_id(1) == 0)
    def _():
        _mixer_init(s0_ref, h0_ref, c0_ref, s_scr, h_scr, xe_scr)

    def put(bi, cols, val):
        mix_ref[bi, :, cols] = val.astype(mix_ref.dtype)

    _mixer(lambda bi, cols: proj_ref[bi, :, cols], put, proj_ref.shape[0], rows,
           (dec_ref, qdec_ref, kdec_ref, cdec_ref),
           (retg_ref, convw_ref, convb_ref, wrg_ref, brg_ref, wig_ref, big_ref, lam_ref),
           s_scr, h_scr, xe_scr, sret_ref, hlast_ref, cnew_ref)


def _ret_tables(t_eff):
    cp = RET_CHUNK
    f32 = np.float32
    log_g = np.log1p(-np.power(f32(2.0), f32(-5.0) - np.arange(RET_HEADS, dtype=f32)))
    idx = np.arange(cp, dtype=f32)
    live = idx < t_eff
    diff = idx[:, None] - idx[None, :]
    ok = (diff >= 0) & live[:, None] & live[None, :]
    decay = np.where(ok, np.exp(log_g[:, None, None] * np.maximum(diff, f32(0.0))), f32(0.0))
    q_dec = np.where(live, np.exp(log_g[:, None] * (idx + f32(1.0))), f32(0.0))
    k_dec = np.where(live, np.exp(log_g[:, None] * (f32(t_eff - 1.0) - idx)), f32(0.0))
    c_dec = np.exp(log_g * f32(t_eff))

    def pair_lanes(t):
        t = t.reshape(RET_HEADS // 2, 2, cp)
        return np.repeat(np.transpose(t, (0, 2, 1)), RET_DK, axis=2)

    c_rows = np.repeat(c_dec.reshape(RET_HEADS // 2, 2), RET_DK, axis=1)
    c_rows = np.broadcast_to(c_rows[:, :, None], (RET_HEADS // 2, 2 * RET_DK, RET_DV))
    as32 = lambda t: np.ascontiguousarray(t, dtype=f32)
    return as32(decay), as32(pair_lanes(q_dec)), as32(pair_lanes(k_dec)), as32(c_rows)


def _rec_mix(proj, s0, h0, c0, ret_g, conv_w, conv_b, w_rg, b_rg, w_ig, b_ig, lam):
    b, t, _ = proj.shape
    rows = RET_CHUNK if t % RET_CHUNK == 0 else t
    assert t % rows == 0 and rows % SUBLANES == 0 and (t == rows or rows == RET_CHUNK)
    n = t // rows
    nb = REC_SEQS if b % REC_SEQS == 0 else 1
    dec, qdec, kdec, cdec = _ret_tables(rows)
    hp = RET_HEADS // 2
    w = LRU_WIDTH
    per_b3 = lambda i, c: (i, 0, 0)
    outs = pl.pallas_call(
        functools.partial(_rec_mix_kernel, rows=rows),
        grid=(b // nb, n),
        in_specs=[
            pl.BlockSpec((nb, rows, REC_IN), lambda i, c: (i, c, 0)),
            pl.BlockSpec((nb, hp, 2 * RET_DK, RET_DV), lambda i, c: (i, 0, 0, 0)),
            pl.BlockSpec((nb, 1, w), per_b3),
            pl.BlockSpec((nb, CONV_WIDTH - 1, w), per_b3),
            _const_spec(dec.shape), _const_spec(qdec.shape), _const_spec(kdec.shape), _const_spec(cdec.shape),
            _const_spec((RET_HEADS, RET_DV)), _const_spec((CONV_WIDTH, w)), _const_spec((1, w)),
            _const_spec((LRU_BLOCKS, LRU_BLOCK, LRU_BLOCK)), _const_spec((1, w)),
            _const_spec((LRU_BLOCKS, LRU_BLOCK, LRU_BLOCK)), _const_spec((1, w)), _const_spec((1, w)),
        ],
        out_specs=[
            pl.BlockSpec((nb, rows, D_MODEL), lambda i, c: (i, c, 0)),
            pl.BlockSpec((nb, hp, 2 * RET_DK, RET_DV), lambda i, c: (i, 0, 0, 0)),
            pl.BlockSpec((nb, 1, w), per_b3),
            pl.BlockSpec((nb, CONV_WIDTH - 1, w), per_b3),
        ],
        out_shape=[
            jax.ShapeDtypeStruct((b, t, D_MODEL), BF16),
            jax.ShapeDtypeStruct((b, hp, 2 * RET_DK, RET_DV), F32),
            jax.ShapeDtypeStruct((b, 1, w), F32),
            jax.ShapeDtypeStruct((b, CONV_WIDTH - 1, w), F32),
        ],
        scratch_shapes=[
            pltpu.VMEM((nb, hp, 2 * RET_DK, RET_DV), F32),
            pltpu.VMEM((nb, 1, w), F32),
            pltpu.VMEM((nb, SUBLANES, w), F32),
        ],
        compiler_params=_cparams(("arbitrary", "arbitrary")),
        name="rec_mix",
    )(proj, s0.reshape(b, hp, 2 * RET_DK, RET_DV), h0.reshape(b, 1, w), c0,
      dec, qdec, kdec, cdec, ret_g, conv_w, conv_b.reshape(1, w),
      w_rg.astype(BF16), b_rg.reshape(1, w), w_ig.astype(BF16), b_ig.reshape(1, w), lam.reshape(1, w))
    mix, s_ret, h_last, c_new = outs
    return mix, s_ret.reshape(b, RET_HEADS, RET_DK, RET_DV), h_last.reshape(b, w), c_new


def _ffn_stages(load_mix, load_x, store, ffn_refs, act_ref, rs):
    wo_ref, g_ref, wg_ref, wu_ref, wd_ref = ffn_refs
    live = {}

    def prologue():
        live["y"] = load_x() + jnp.dot(load_mix(), wo_ref[...], preferred_element_type=F32)
        live["h"] = _rmsnorm(live["y"], g_ref[...]).astype(BF16)

    def gate_up(c):
        cols = slice(c * FF_CHUNK, (c + 1) * FF_CHUNK)
        gate = jnp.dot(live["h"], wg_ref[:, cols], preferred_element_type=F32)
        up = jnp.dot(live["h"], wu_ref[:, cols], preferred_element_type=F32)
        act_ref[rs, cols] = (gate * jax.nn.sigmoid(gate) * up).astype(BF16)

    def down():
        store(live["y"] + jnp.dot(act_ref[rs, :], wd_ref[...], preferred_element_type=F32))

    return ([prologue] + [functools.partial(gate_up, c) for c in range(wg_ref.shape[1] // FF_CHUNK)] + [down])


def _out_ffn_kernel(mix_ref, x_ref, wo_ref, g_ref, wg_ref, wu_ref, wd_ref, o_ref, act_ref):
    def store_rows(rs, val):
        o_ref[rs, :] = val

    for rs in _row_parts(x_ref.shape[0]):
        _run_stages(_ffn_stages(lambda rs=rs: mix_ref[rs, :].astype(BF16), lambda rs=rs: x_ref[rs, :],
                                functools.partial(store_rows, rs),
                                (wo_ref, g_ref, wg_ref, wu_ref, wd_ref), act_ref, rs))


def _layer_spec(stacked, layer):
    shape = stacked.shape[1:]
    return pl.BlockSpec((None,) + shape, lambda *_: (layer,) + (0,) * len(shape), pipeline_mode=pl.Buffered(1))


def _out_ffn(mix2d, x2d, w_out, g, ffn_stacked, layer):
    m, d = x2d.shape
    tm = min(WIDE_TILE, m)
    row = lambda i: (i, 0)
    return pl.pallas_call(
        _out_ffn_kernel,
        grid=(m // tm,),
        in_specs=[pl.BlockSpec((tm, d), row), pl.BlockSpec((tm, d), row), _const_spec(w_out.shape),
                  _const_spec((1, d))] + [_layer_spec(w, layer) for w in ffn_stacked],
        out_specs=pl.BlockSpec((tm, d), row),
        out_shape=jax.ShapeDtypeStruct((m, d), F32),
        scratch_shapes=[pltpu.VMEM((tm, ffn_stacked[0].shape[2]), BF16)],
        compiler_params=_cparams(("arbitrary",)),
        name="out_ffn",
    )(mix2d, x2d, w_out, g.reshape(1, d), *ffn_stacked)


def _layer0_kernel(xc_ref, xp_ref, gm_ref, win_ref, s0_ref, h0_ref, c0_ref,
                   dec_ref, qdec_ref, kdec_ref, cdec_ref,
                   retg_ref, convw_ref, convb_ref, wrg_ref, brg_ref, wig_ref, big_ref, lam_ref,
                   wo_ref, g_ref, wg_ref, wu_ref, wd_ref,
                   y_ref, sret_ref, hlast_ref, cnew_ref,
                   s_scr, h_scr, xe_scr, mix_scr, act_scr):
    c = pl.program_id(1)
    last = pl.num_programs(1) - 1
    nseq, rows, d = xc_ref.shape
    tables = (dec_ref, qdec_ref, kdec_ref, cdec_ref)
    params = (retg_ref, convw_ref, convb_ref, wrg_ref, brg_ref, wig_ref, big_ref, lam_ref)
    ffn_refs = (wo_ref, g_ref, wg_ref, wu_ref, wd_ref)

    def mix_stages(slot):
        live = {}

        def project():
            x = xc_ref[...].reshape(nseq * rows, d)
            live["proj"] = jnp.dot(_rmsnorm(x, gm_ref[...]).astype(BF16), win_ref[...], preferred_element_type=F32)

        def put(bi, cols, val):
            mix_scr[slot, bi * rows:(bi + 1) * rows, cols] = val.astype(mix_scr.dtype)

        return [project] + _mixer_stages(lambda bi, cols: live["proj"][bi * rows:(bi + 1) * rows, cols], put,
                                         nseq, rows, tables, params, s_scr, h_scr, xe_scr,
                                         sret_ref, hlast_ref, cnew_ref)

    def ffn_stages(slot):
        def store(val):
            y_ref[...] = val.reshape(nseq, rows, d)

        return _ffn_stages(lambda: mix_scr[slot], lambda: xp_ref[...].reshape(nseq * rows, d), store,
                           ffn_refs, act_scr, slice(0, nseq * rows))

    @pl.when(c == 0)
    def _():
        _mixer_init(s0_ref, h0_ref, c0_ref, s_scr, h_scr, xe_scr)
        _run_stages(mix_stages(0))

    @pl.when(jnp.logical_and(c > 0, c < last))
    def _():
        _run_stages(mix_stages(c % 2), ffn_stages(1 - c % 2))

    @pl.when(c == last)
    def _():
        _run_stages(ffn_stages(1 - last % 2))


def _layer0_prompt(x, g_mix, w_in, s0, h0, c0, rec_prm, w_out, g_ffn, ffn_stacked, layer):
    ret_g, conv_w, conv_b, w_rg, b_rg, w_ig, b_ig, lam = rec_prm
    b, t, d = x.shape
    rows = RET_CHUNK
    assert t % rows == 0 and b % REC_SEQS == 0
    n = t // rows
    nb = REC_SEQS
    dec, qdec, kdec, cdec = _ret_tables(rows)
    hp = RET_HEADS // 2
    w = LRU_WIDTH
    per_b3 = lambda i, c: (i, 0, 0)
    chunk = (nb, rows, d)
    outs = pl.pallas_call(
        _layer0_kernel,
        grid=(b // nb, n + 1),
        in_specs=[
            pl.BlockSpec(chunk, lambda i, c: (i, jnp.minimum(c, n - 1), 0)),
            pl.BlockSpec(chunk, lambda i, c: (i, jnp.maximum(c - 1, 0), 0)),
            _const_spec((1, d)), _const_spec(w_in.shape),
            pl.BlockSpec((nb, hp, 2 * RET_DK, RET_DV), lambda i, c: (i, 0, 0, 0)),
            pl.BlockSpec((nb, 1, w), per_b3),
            pl.BlockSpec((nb, CONV_WIDTH - 1, w), per_b3),
            _const_spec(dec.shape), _const_spec(qdec.shape), _const_spec(kdec.shape), _const_spec(cdec.shape),
            _const_spec((RET_HEADS, RET_DV)), _const_spec((CONV_WIDTH, w)), _const_spec((1, w)),
            _const_spec((LRU_BLOCKS, LRU_BLOCK, LRU_BLOCK)), _const_spec((1, w)),
            _const_spec((LRU_BLOCKS, LRU_BLOCK, LRU_BLOCK)), _const_spec((1, w)), _const_spec((1, w)),
            _const_spec(w_out.shape), _const_spec((1, d)),
        ] + [_layer_spec(wt, layer) for wt in ffn_stacked],
        out_specs=[
            pl.BlockSpec(chunk, lambda i, c: (i, jnp.maximum(c - 1, 0), 0)),
            pl.BlockSpec((nb, hp, 2 * RET_DK, RET_DV), lambda i, c: (i, 0, 0, 0)),
            pl.BlockSpec((nb, 1, w), per_b3),
            pl.BlockSpec((nb, CONV_WIDTH - 1, w), per_b3),
        ],
        out_shape=[
            jax.ShapeDtypeStruct((b, t, d), F32),
            jax.ShapeDtypeStruct((b, hp, 2 * RET_DK, RET_DV), F32),
            jax.ShapeDtypeStruct((b, 1, w), F32),
            jax.ShapeDtypeStruct((b, CONV_WIDTH - 1, w), F32),
        ],
        scratch_shapes=[
            pltpu.VMEM((nb, hp, 2 * RET_DK, RET_DV), F32),
            pltpu.VMEM((nb, 1, w), F32),
            pltpu.VMEM((nb, SUBLANES, w), F32),
            pltpu.VMEM((2, nb * rows, d), BF16),
            pltpu.VMEM((nb * rows, ffn_stacked[0].shape[2]), BF16),
        ],
        compiler_params=_cparams(("arbitrary", "arbitrary")),
        name="layer0",
    )(x, x, g_mix.reshape(1, d), w_in, s0.reshape(b, hp, 2 * RET_DK, RET_DV), h0.reshape(b, 1, w), c0,
      dec, qdec, kdec, cdec, ret_g, conv_w, conv_b.reshape(1, w),
      w_rg.astype(BF16), b_rg.reshape(1, w), w_ig.astype(BF16), b_ig.reshape(1, w), lam.reshape(1, w),
      w_out, g_ffn.reshape(1, d), *ffn_stacked)
    y, s_ret, h_last, c_new = outs
    return y, s_ret.reshape(b, RET_HEADS, RET_DK, RET_DV), h_last.reshape(b, w), c_new


def _pair_rms(a, g2, first):
    sq = a * a
    s_a = jnp.sum(jnp.where(first, sq, 0.0), axis=-1, keepdims=True)
    s_b = jnp.sum(jnp.where(first, 0.0, sq), axis=-1, keepdims=True)
    ms = jnp.where(first, s_a, s_b) * (1.0 / ATT_DH)
    return a * lax.rsqrt(ms + EPS) * g2


def _qkv_slabs(x, g_ref, w_ref, qg_ref, kg_ref):
    h = _rmsnorm(x, g_ref[...]).astype(BF16)
    first = lax.broadcasted_iota(jnp.int32, (x.shape[0], LANES), 1) < ATT_DH
    slabs = []
    for c in range(w_ref.shape[1] // MXU_COLS):
        part = jnp.dot(h, w_ref[:, c * MXU_COLS:(c + 1) * MXU_COLS], preferred_element_type=F32)
        slabs += [part[:, i * LANES:(i + 1) * LANES] for i in range(MXU_COLS // LANES)]
    q = [_pair_rms(slabs[p], qg_ref[...], first) for p in range(NPAIR)]
    k = [_pair_rms(slabs[NPAIR + p], kg_ref[...], first) for p in range(NPAIR)]
    v = slabs[2 * NPAIR:]
    return q, k, v


def _qkv_flat_kernel(x_ref, g_ref, w_ref, qg_ref, kg_ref, q_ref, k_ref, v_ref):
    q, k, v = _qkv_slabs(x_ref[...], g_ref, w_ref, qg_ref, kg_ref)
    q_ref[...] = jnp.concatenate(q, axis=1)
    k_ref[...] = jnp.concatenate(k, axis=1)
    v_ref[...] = jnp.concatenate(v, axis=1)


def _qkv_perm_kernel(x_ref, g_ref, w_ref, qg_ref, kg_ref, q_ref, k_ref, v_ref, xs_ref):
    for p in range(NPAIR):
        xs_ref[p] = x_ref[:, p * LANES:(p + 1) * LANES]
    for rs in _row_parts(x_ref.shape[0]):
        n = (rs.stop - rs.start) // MAX_DIL
        x = jnp.concatenate(
            [jnp.concatenate([xs_ref[p, pl.ds(rs.start + r, n, stride=MAX_DIL), :] for p in range(NPAIR)], axis=1)
             for r in range(MAX_DIL)], axis=0)
        q, k, v = _qkv_slabs(x, g_ref, w_ref, qg_ref, kg_ref)
        dst = slice(rs.start // MAX_DIL, rs.stop // MAX_DIL)
        for ref, slabs in ((q_ref, q), (k_ref, k), (v_ref, v)):
            for p in range(NPAIR):
                for r in range(MAX_DIL):
                    ref[p, r, dst] = slabs[p][r * n:(r + 1) * n]


def _head_gain(g):
    return jnp.tile(g.reshape(1, ATT_DH), (1, LANES // ATT_DH))


def _qkv_flat(x2d, g, w_bf, qg, kg):
    m, d = x2d.shape
    tm = min(TOKEN_TILE, m)
    row = lambda i: (i, 0)
    return pl.pallas_call(
        _qkv_flat_kernel,
        grid=(m // tm,),
        in_specs=[pl.BlockSpec((tm, d), row), _const_spec((1, d)), _const_spec(w_bf.shape),
                  _const_spec((1, LANES)), _const_spec((1, LANES))],
        out_specs=[pl.BlockSpec((tm, d), row)] * 3,
        out_shape=[jax.ShapeDtypeStruct((m, d), F32)] * 3,
        compiler_params=_cparams(("arbitrary",)),
        name="qkv_flat",
    )(x2d, g.reshape(1, d), w_bf, _head_gain(qg), _head_gain(kg))


def _qkv_perm(x, g, w_bf, qg, kg):
    b, t, d = x.shape
    ns = t // SUPER
    tm = WIDE_TILE
    nt = SUPER // tm
    n = tm // MAX_DIL
    out_spec = pl.BlockSpec((None, NPAIR, None, MAX_DIL, None, n, LANES), lambda i, s, j: (i, 0, s, 0, j, 0, 0))
    out_shape = jax.ShapeDtypeStruct((b, NPAIR, ns, MAX_DIL, nt, n, LANES), F32)
    q, k, v = pl.pallas_call(
        _qkv_perm_kernel,
        grid=(b, ns, nt),
        in_specs=[pl.BlockSpec((None, tm, d), lambda i, s, j: (i, s * nt + j, 0)),
                  _const_spec((1, d)), _const_spec(w_bf.shape), _const_spec((1, LANES)), _const_spec((1, LANES))],
        out_specs=[out_spec] * 3,
        out_shape=[out_shape] * 3,
        scratch_shapes=[pltpu.VMEM((NPAIR, tm, LANES), F32)],
        compiler_params=_cparams(("arbitrary",) * 3),
        name="qkv_perm",
    )(x, g.reshape(1, d), w_bf, _head_gain(qg), _head_gain(kg))
    return tuple(a.reshape(b, NPAIR, t, LANES) for a in (q, k, v))


def _alibi_slopes():
    return np.power(np.float32(2.0), np.float32(-8.0) * np.arange(1, ATT_HEADS + 1, dtype=np.float32) / ATT_HEADS)


PREV_SLOTS = 32
ATT_UNROLL = 16
LOG2E = 1.4426950408889634


def _band_bias():
    i = np.arange(ATT_BLOCK)
    c = np.arange(2 * ATT_BLOCK)
    q_idx = {1: MAX_DIL * (i % 8) + i // 8,
             4: 4 * (i % 32) + i // 32,
             16: i}
    k_idx = {1: (MAX_DIL * (c % 8) + c // 16, (c // 8) % 2 == 0),
             4: (4 * (c % 32) + c // 64, (c // 32) % 2 == 0),
             16: (c % ATT_BLOCK, c < ATT_BLOCK)}
    slopes = _alibi_slopes().reshape(ATT_HEADS // 2, 2, 1, 1)
    neg = np.float32(NEG)
    out = []
    for (w, d) in PATTERNS:
        span = w // d
        km, is_prev = k_idx[d]
        diff = q_idx[d][:, None] - (km - np.where(is_prev, ATT_BLOCK, 0))[None, :]
        valid = (diff >= 0) & (diff <= span)
        bias = -slopes * (d * diff).astype(np.float32)[None, None]
        bias = bias * np.float32(LOG2E)
        variants = [np.where(valid, bias, neg), np.where(valid & ~is_prev[None, :], bias, neg)]
        out.append(np.stack([v.reshape(ATT_HEADS // 2, 2 * ATT_BLOCK, 2 * ATT_BLOCK) for v in variants]))
    return np.stack(out).astype(np.float32)


def _attend_pair(q, k, v, bias, first):
    qs = jnp.concatenate([jnp.where(first, q, 0.0), jnp.where(first, 0.0, q)], axis=0).astype(BF16)
    s = lax.dot_general(qs, k.astype(BF16), (((1,), (1,)), ((), ())), preferred_element_type=F32) + bias
    m = jnp.max(s, axis=-1, keepdims=True)
    e = jnp.exp2(s - m).astype(BF16)
    vb = v.astype(BF16)
    r = jnp.dot(e, jnp.concatenate([vb, jnp.ones_like(vb)], axis=1), preferred_element_type=F32)
    halves = lambda a: jnp.where(first, a[:ATT_BLOCK], a[ATT_BLOCK:])
    return halves(r[:, :LANES]), halves(m), halves(r[:, LANES:])


def _attn_prompt_kernel(q_ref, kc_ref, kp_ref, vc_ref, vp_ref, bias_ref, o_ref, ko_ref, vo_ref,
                        kext, vext, oacc, macc, dacc, nat_ref):
    sb = pl.program_id(2)
    first_sb = (sb == 0).astype(jnp.int32)
    first = lax.broadcasted_iota(jnp.int32, (ATT_BLOCK, LANES), 1) < ATT_DH
    scale = (ATT_DH ** -0.5) * LOG2E
    ps = PREV_SLOTS

    for src_p, src_c, ext in ((kp_ref, kc_ref, kext), (vp_ref, vc_ref, vext)):
        for r in range(MAX_DIL):
            ext[r, 0:ps, :] = src_p[r * ATT_BLOCK + ATT_BLOCK - ps:(r + 1) * ATT_BLOCK, :]
            ext[r, ps:ps + ATT_BLOCK, :] = src_c[r * ATT_BLOCK:(r + 1) * ATT_BLOCK, :]

    def attend_store(p, variant, q_chunks, k, v):
        n = ATT_BLOCK // len(q_chunks)
        q = jnp.concatenate([q_ref[sl, :] for sl in q_chunks], axis=0) * scale
        o, m, den = _attend_pair(q, k, v, bias_ref[p, variant], first)
        for ci, sl in enumerate(q_chunks):
            oacc[p, sl, :] = o[ci * n:(ci + 1) * n]
            macc[p, sl, :] = m[ci * n:(ci + 1) * n]
            dacc[p, sl, :] = den[ci * n:(ci + 1) * n]

    n0 = ATT_BLOCK // MAX_DIL

    def block0(bi):
        base = pl.multiple_of(bi * n0, n0)
        ksl = pl.ds(base + ps - n0, 2 * n0)
        k = jnp.concatenate([kext[r, ksl, :] for r in range(MAX_DIL)], axis=0)
        v = jnp.concatenate([vext[r, ksl, :] for r in range(MAX_DIL)], axis=0)
        attend_store(0, first_sb * jnp.asarray(bi == 0, jnp.int32),
                     [pl.ds(r * ATT_BLOCK + base, n0) for r in range(MAX_DIL)], k, v)

    d1 = PATTERNS[1][1]
    nc1 = MAX_DIL // d1
    n1 = ATT_BLOCK // nc1

    def block1(idx):
        r4, mb = idx // (ATT_BLOCK // n1), idx % (ATT_BLOCK // n1)
        base = pl.multiple_of(mb * n1, n1)
        ksl = pl.ds(base + ps - n1, 2 * n1)
        k = jnp.concatenate([kext[r4 + d1 * c, ksl, :] for c in range(nc1)], axis=0)
        v = jnp.concatenate([vext[r4 + d1 * c, ksl, :] for c in range(nc1)], axis=0)
        attend_store(1, first_sb * jnp.asarray(mb == 0, jnp.int32),
                     [pl.ds(pl.multiple_of((r4 + d1 * c) * ATT_BLOCK + base, n1), n1) for c in range(nc1)], k, v)

    def block2(r):
        sl = pl.ds(pl.multiple_of(r * ATT_BLOCK, ATT_BLOCK), ATT_BLOCK)
        k = jnp.concatenate([kp_ref[sl, :], kc_ref[sl, :]], axis=0)
        v = jnp.concatenate([vp_ref[sl, :], vc_ref[sl, :]], axis=0)
        attend_store(2, first_sb, [sl], k, v)

    for block in (block0, block1, block2):
        def body(it, carry, block=block):
            for u in range(ATT_UNROLL):
                block(it * ATT_UNROLL + u)
            return carry

        lax.fori_loop(0, MAX_DIL // ATT_UNROLL, body, 0)

    def comb(r, carry):
        sl = pl.ds(pl.multiple_of(r * ATT_BLOCK, ATT_BLOCK), ATT_BLOCK)
        m0, m1, m2 = macc[0, sl, :], macc[1, sl, :], macc[2, sl, :]
        mx = jnp.maximum(jnp.maximum(m0, m1), m2)
        w0, w1, w2 = jnp.exp2(m0 - mx), jnp.exp2(m1 - mx), jnp.exp2(m2 - mx)
        num = w0 * oacc[0, sl, :] + w1 * oacc[1, sl, :] + w2 * oacc[2, sl, :]
        den = w0 * dacc[0, sl, :] + w1 * dacc[1, sl, :] + w2 * dacc[2, sl, :]
        o_ref[pl.ds(r, ATT_BLOCK, stride=MAX_DIL), :] = num / den
        return carry

    lax.fori_loop(0, SUPER // ATT_BLOCK, comb, 0)

    @pl.when(sb == pl.num_programs(2) - 1)
    def _():
        for src, dst in ((kc_ref, ko_ref), (vc_ref, vo_ref)):
            for r in range(MAX_DIL):
                nat_ref[pl.ds(r, ATT_BLOCK, stride=MAX_DIL), :] = src[r * ATT_BLOCK:(r + 1) * ATT_BLOCK, :]
            dst[...] = jnp.transpose(nat_ref[...])


def _attn_prompt(q, k, v):
    b, npair, t, _ = q.shape
    ns = t // SUPER
    bias = _band_bias()
    cur = pl.BlockSpec((None, None, SUPER, LANES), lambda i, p, s: (i, p, s, 0))
    prev = pl.BlockSpec((None, None, SUPER, LANES), lambda i, p, s: (i, p, jnp.maximum(s - 1, 0), 0))
    cache = pl.BlockSpec((None, LANES, SUPER), lambda i, p, s: (i, p, 0))
    cache_shape = jax.ShapeDtypeStruct((b, npair * LANES, SUPER), F32)
    return pl.pallas_call(
        _attn_prompt_kernel,
        grid=(b, npair, ns),
        in_specs=[cur, cur, prev, cur, prev,
                  pl.BlockSpec((len(PATTERNS), 2, None, 2 * ATT_BLOCK, 2 * ATT_BLOCK),
                               lambda i, p, s: (0, 0, p, 0, 0))],
        out_specs=[pl.BlockSpec((None, SUPER, LANES), lambda i, p, s: (i, s, p)), cache, cache],
        out_shape=[jax.ShapeDtypeStruct((b, t, npair * LANES), F32), cache_shape, cache_shape],
        scratch_shapes=[pltpu.VMEM((MAX_DIL, PREV_SLOTS + ATT_BLOCK, LANES), F32),
                        pltpu.VMEM((MAX_DIL, PREV_SLOTS + ATT_BLOCK, LANES), F32),
                        pltpu.VMEM((len(PATTERNS), SUPER, LANES), F32),
                        pltpu.VMEM((len(PATTERNS), SUPER, LANES), F32),
                        pltpu.VMEM((len(PATTERNS), SUPER, LANES), F32),
                        pltpu.VMEM((SUPER, LANES), F32)],
        compiler_params=_cparams(("arbitrary",) * 3),
        name="attn_prompt",
    )(q, k, k, v, v, bias)


QUAD = 4
NEW_PAD = 128


def _sample_bias(s_len, n_past):
    s = np.arange(s_len)
    col = np.arange(n_past + NEW_PAD)
    dist = (n_past + s)[:, None] - col[None, :]
    real = (col < n_past + s_len)[None, :]
    slopes = _alibi_slopes().reshape(ATT_HEADS // QUAD, QUAD, 1, 1)
    out = []
    for (w, d) in PATTERNS:
        valid = (dist >= 0) & (dist % d == 0) & (dist <= w) & real
        bias = np.where(valid[None, None], -slopes * dist.astype(np.float32)[None, None], np.float32(NEG))
        out.append(bias.reshape(ATT_HEADS // QUAD, QUAD * s_len, n_past + NEW_PAD))
    return np.stack(out).astype(np.float32)


def _sample_stages(q_ref, kn_ref, vn_ref, ck_ref, cv_ref, bias_ref, o_ref, quad0, s_len):
    n_past = ck_ref.shape[1]
    width = QUAD * ATT_DH
    rows = QUAD * s_len
    nt = (((1,), (1,)), ((), ()))

    def own():
        return (lax.broadcasted_iota(jnp.int32, (rows, width), 0) // s_len
                == lax.broadcasted_iota(jnp.int32, (rows, width), 1) // ATT_DH)

    def scores(ql, live):
        cols = slice(ql * width, (ql + 1) * width)
        q = q_ref[:, cols] * (ATT_DH ** -0.5)
        qm = jnp.where(own(), jnp.concatenate([q] * QUAD, axis=0), 0.0).astype(BF16)
        kn = _pad_rows(kn_ref[:, cols], NEW_PAD).astype(BF16)
        live["s_c"] = jnp.dot(qm, ck_ref[cols, :].astype(BF16), preferred_element_type=F32)
        live["s_n"] = lax.dot_general(qm, kn, nt, preferred_element_type=F32)
        live["parts"] = []

    def pattern(ql, p, live):
        sc = live["s_c"] + bias_ref[p, quad0 + ql, :, 0:n_past]
        sn = live["s_n"] + bias_ref[p, quad0 + ql, :, n_past:n_past + NEW_PAD]
        m = jnp.maximum(jnp.max(sc, axis=-1, keepdims=True), jnp.max(sn, axis=-1, keepdims=True))
        ec, en = jnp.exp(sc - m), jnp.exp(sn - m)
        den = jnp.sum(ec, axis=-1, keepdims=True) + jnp.sum(en, axis=-1, keepdims=True)
        live["parts"].append((ec, en, den, m + jnp.log(den)))

    def combine(ql, live):
        cols = slice(ql * width, (ql + 1) * width)
        es_c, es_n, dens, lses = zip(*live["parts"])
        mx = jnp.maximum(jnp.maximum(lses[0], lses[1]), lses[2])
        ws = [jnp.exp(l - mx) for l in lses]
        wsum = ws[0] + ws[1] + ws[2]
        coef = [w / (wsum * d) for w, d in zip(ws, dens)]
        c_c = coef[0] * es_c[0] + coef[1] * es_c[1] + coef[2] * es_c[2]
        c_n = coef[0] * es_n[0] + coef[1] * es_n[1] + coef[2] * es_n[2]
        vn = _pad_rows(vn_ref[:, cols], NEW_PAD).astype(BF16)
        o = (lax.dot_general(c_c.astype(BF16), cv_ref[cols, :].astype(BF16), nt, preferred_element_type=F32)
             + jnp.dot(c_n.astype(BF16), vn, preferred_element_type=F32))
        o = jnp.where(own(), o, 0.0)
        acc = o[0:s_len]
        for hh in range(1, QUAD):
            acc = acc + o[hh * s_len:(hh + 1) * s_len]
        o_ref[:, cols] = acc.astype(o_ref.dtype)

    stages = []
    for ql in range(q_ref.shape[1] // width):
        live = {}
        stages.append(functools.partial(scores, ql, live))
        stages += [functools.partial(pattern, ql, p, live) for p in range(len(PATTERNS))]
        stages.append(functools.partial(combine, ql, live))
    return stages


SAMPLE_QUADS = 2


def _ffn_sample_kernel(mix_ref, x_ref, wo_ref, g_ref, wg_ref, wu_ref, wd_ref,
                       q_ref, kn_ref, vn_ref, ck_ref, cv_ref, bias_ref, o_ref, os_ref, act_ref, *, s_len):
    def store(val):
        o_ref[...] = val

    steps_per_seq = bias_ref.shape[1] // SAMPLE_QUADS
    _run_alternating(
        _ffn_stages(lambda: mix_ref[...].astype(BF16), lambda: x_ref[...], store,
                    (wo_ref, g_ref, wg_ref, wu_ref, wd_ref), act_ref, slice(0, x_ref.shape[0])),
        _sample_stages(q_ref, kn_ref, vn_ref, ck_ref, cv_ref, bias_ref, os_ref,
                       (pl.program_id(0) % steps_per_seq) * SAMPLE_QUADS, s_len))


def _out_ffn_and_sample_attn(mix2d, x2d, w_out, g, ffn_stacked, layer, q, kn, vn, ck, cv):
    m, d = x2d.shape
    b, s_len, _ = q.shape
    n_past = ck.shape[2]
    width = SAMPLE_QUADS * QUAD * ATT_DH
    sps = d // width
    steps = b * sps
    assert m % steps == 0 and (m // steps) % SUBLANES == 0
    tm = m // steps
    bias = _sample_bias(s_len, n_past)
    row = lambda i: (i, 0)
    new = pl.BlockSpec((None, s_len, width), lambda i: (i // sps, 0, i % sps))
    old = pl.BlockSpec((None, width, n_past), lambda i: (i // sps, i % sps, 0))
    return pl.pallas_call(
        functools.partial(_ffn_sample_kernel, s_len=s_len),
        grid=(steps,),
        in_specs=[pl.BlockSpec((tm, d), row), pl.BlockSpec((tm, d), row), _const_spec(w_out.shape),
                  _const_spec((1, d))] + [_layer_spec(w, layer) for w in ffn_stacked]
                 + [new, new, new, old, old, _const_spec(bias.shape)],
        out_specs=[pl.BlockSpec((tm, d), row), new],
        out_shape=[jax.ShapeDtypeStruct((m, d), F32), jax.ShapeDtypeStruct((b, s_len, d), BF16)],
        scratch_shapes=[pltpu.VMEM((tm, ffn_stacked[0].shape[2]), BF16)],
        compiler_params=_cparams(("arbitrary",)),
        name="ffn_sample_attn",
    )(mix2d, x2d, w_out, g.reshape(1, d), *ffn_stacked, q, kn, vn, ck, cv, bias)


def kernel(x_prompt, x_sample, state_ret, state_lru, state_conv, cache_k, cache_v, norm_mix, norm_ffn, w_in_rec, w_out_rec, ret_norm_g, conv_w, conv_b, w_rgate, b_rgate, w_igate, b_igate, lru_lambda, w_in_att, w_out_att, q_norm_g, k_norm_g, w_ffn_gate, w_ffn_up, w_ffn_down):
    bp, tp, d = x_prompt.shape
    bs, ts, _ = x_sample.shape
    depth = norm_mix.shape[0]
    assert tp % SUPER == 0 and d == D_MODEL
    yp, ys = x_prompt, x_sample
    ffn_stacked = (w_ffn_gate.astype(BF16), w_ffn_up.astype(BF16), w_ffn_down.astype(BF16))
    ret_p, lru_p, conv_p, kp, vp = [], [], [], [], []
    ret_s, lru_s, conv_s, ksl, vsl = [], [], [], [], []
    for l in range(depth):
        i = l // 2
        if l % 2 == 0:
            ffn = (w_out_rec[i].astype(BF16), norm_ffn[l], ffn_stacked, l)
            w_in = w_in_rec[i].astype(BF16)
            rec_prm = (ret_norm_g[i], conv_w[i], conv_b[i], w_rgate[i], b_rgate[i], w_igate[i], b_igate[i],
                       lru_lambda[i])
            yp, sr, sl, sc = _layer0_prompt(yp, norm_mix[l], w_in, jnp.zeros((bp, RET_HEADS, RET_DK, RET_DV), F32),
                                            jnp.zeros((bp, LRU_WIDTH), F32),
                                            jnp.zeros((bp, CONV_WIDTH - 1, LRU_WIDTH), F32), rec_prm, *ffn)
            ret_p.append(sr); lru_p.append(sl); conv_p.append(sc)
            proj = _norm_proj(ys.reshape(bs * ts, d), norm_mix[l], w_in).reshape(bs, ts, REC_IN)
            mix, sr, sl, sc = _rec_mix(proj, state_ret[i], state_lru[i], state_conv[i], *rec_prm)
            ys = _out_ffn(mix.reshape(bs * ts, d), ys.reshape(bs * ts, d), *ffn).reshape(bs, ts, d)
            ret_s.append(sr); lru_s.append(sl); conv_s.append(sc)
        else:
            ffn = (w_out_att[i].astype(BF16), norm_ffn[l], ffn_stacked, l)
            w_in = w_in_att[i].astype(BF16)
            q, k, v = _qkv_perm(yp, norm_mix[l], w_in, q_norm_g[i], k_norm_g[i])
            o, kc, vc = _attn_prompt(q, k, v)
            from_slabs = lambda c: jnp.transpose(c.reshape(bp, ATT_HEADS, ATT_DH, SUPER), (0, 3, 1, 2))
            kp.append(from_slabs(kc)); vp.append(from_slabs(vc))
            qs, kn, vn = _qkv_flat(ys.reshape(bs * ts, d), norm_mix[l], w_in, q_norm_g[i], k_norm_g[i])
            n_past = cache_k.shape[2]
            to_slabs = lambda c: jnp.transpose(c, (0, 2, 3, 1)).reshape(bs, d, n_past)
            yp, os_ = _out_ffn_and_sample_attn(
                o.reshape(bp * tp, d), yp.reshape(bp * tp, d), *ffn,
                qs.reshape(bs, ts, d), kn.reshape(bs, ts, d), vn.reshape(bs, ts, d),
                to_slabs(cache_k[i]), to_slabs(cache_v[i]))
            yp = yp.reshape(bp, tp, d)
            ys = _out_ffn(os_.reshape(bs * ts, d), ys.reshape(bs * ts, d), *ffn).reshape(bs, ts, d)
            ksl.append(kn.reshape(bs, ts, ATT_HEADS, ATT_DH)); vsl.append(vn.reshape(bs, ts, ATT_HEADS, ATT_DH))
    return (yp, ys,
            jnp.stack(ret_p), jnp.stack(lru_p), jnp.stack(conv_p), jnp.stack(kp), jnp.stack(vp),
            jnp.stack(ret_s), jnp.stack(lru_s), jnp.stack(conv_s), jnp.stack(ksl), jnp.stack(vsl))
```

```python
import functools

import jax
import numpy as np
import jax.numpy as jnp
from jax import lax
from jax.experimental import pallas as pl
from jax.experimental.pallas import tpu as pltpu

F32 = jnp.float32
BF16 = jnp.bfloat16

EPS = 1e-6
D_MODEL = 1024
RET_HEADS = 4
RET_DK = 64
RET_DV = 128
RET_CHUNK = 128
LRU_WIDTH = 512
LRU_BLOCKS = 4
LRU_BLOCK = LRU_WIDTH // LRU_BLOCKS
CONV_WIDTH = 4
LRU_C = 8.0
GELU_C = 0.7978845608028654
REC_IN = 2 * RET_HEADS * RET_DK + 2 * RET_HEADS * RET_DV + 2 * LRU_WIDTH
ATT_HEADS = 16
ATT_DH = 64
PATTERNS = ((128, 1), (512, 4), (2048, 16))
ATT_BLOCK = 128
SUPER = 2048
MAX_DIL = 16
NEG = -1e30

LANES = 128
SUBLANES = 8
MXU_COLS = 256
NPAIR = D_MODEL // LANES
VMEM_LIMIT = 56 * 1024 * 1024

FF_CHUNK = 256
TOKEN_TILE = 512
WIDE_TILE = 1024
STREAM_ROWS = 512
REC_SEQS = 4


def _cparams(sem):
    return pltpu.CompilerParams(dimension_semantics=sem, vmem_limit_bytes=VMEM_LIMIT)


def _const_spec(shape):
    nd = len(shape)
    return pl.BlockSpec(shape, lambda *_: (0,) * nd, pipeline_mode=pl.Buffered(1))


def _rmsnorm(x, g):
    return x * lax.rsqrt(jnp.mean(x * x, axis=-1, keepdims=True) + EPS) * g


def _row_parts(rows):
    n = max(1, rows // STREAM_ROWS)
    return [slice(i * (rows // n), (i + 1) * (rows // n)) for i in range(n)]


def _norm_proj_kernel(x_ref, g_ref, w_ref, o_ref):
    for rs in _row_parts(x_ref.shape[0]):
        h = _rmsnorm(x_ref[rs, :], g_ref[...]).astype(BF16)
        o_ref[rs, :] = jnp.dot(h, w_ref[...], preferred_element_type=F32)


def _norm_proj(x2d, g, w_bf):
    m, d = x2d.shape
    n = w_bf.shape[1]
    tm = min(WIDE_TILE, m)
    return pl.pallas_call(
        _norm_proj_kernel,
        grid=(m // tm,),
        in_specs=[pl.BlockSpec((tm, d), lambda i: (i, 0)), _const_spec((1, d)), _const_spec((d, n))],
        out_specs=pl.BlockSpec((tm, n), lambda i: (i, 0)),
        out_shape=jax.ShapeDtypeStruct((m, n), F32),
        compiler_params=_cparams(("arbitrary",)),
        name="norm_proj",
    )(x2d, g.reshape(1, d), w_bf)


def _linear_scan(a, b, h0):
    rows, w = a.shape
    a = a.reshape(rows // SUBLANES, SUBLANES, w)
    b = b.reshape(rows // SUBLANES, SUBLANES, w)
    sub = lax.broadcasted_iota(jnp.int32, a.shape, 1)
    s = 1
    while s < SUBLANES:
        keep = sub >= s
        b = jnp.where(keep, a * pltpu.roll(b, s, axis=1), 0.0) + b
        a = jnp.where(keep, a * pltpu.roll(a, s, axis=1), a)
        s *= 2
    groups = []
    carry = h0
    for i in range(rows // SUBLANES):
        groups.append(b[i] + a[i] * carry)
        carry = groups[-1][SUBLANES - 1:SUBLANES]
    return jnp.concatenate(groups, axis=0)


def _pad_rows(x, rows):
    if x.shape[0] == rows:
        return x
    return jnp.concatenate([x, jnp.zeros((rows - x.shape[0], x.shape[1]), x.dtype)], axis=0)


def _mixer_init(s0_ref, h0_ref, c0_ref, s_scr, h_scr, xe_scr):
    s_scr[...] = s0_ref[...]
    h_scr[...] = h0_ref[...]
    xe_scr[...] = jnp.zeros(xe_scr.shape, F32)
    xe_scr[:, SUBLANES - (CONV_WIDTH - 1):SUBLANES, :] = c0_ref[...]


def _run_stages(*streams):
    for stages in streams:
        for stage in stages:
            stage()


def _run_alternating(a, b):
    ia = ib = 0
    while ia < len(a) or ib < len(b):
        if ib >= len(b) or (ia < len(a) and ia * len(b) <= ib * len(a)):
            a[ia]()
            ia += 1
        else:
            b[ib]()
            ib += 1


def _mixer(*args):
    _run_stages(_mixer_stages(*args))


def _mixer_stages(get, put, nseq, rows, tables, params, s_scr, h_scr, xe_scr, sret_ref, hlast_ref, cnew_ref):
    dec_ref, qdec_ref, kdec_ref, cdec_ref = tables
    retg_ref, convw_ref, convb_ref, wrg_ref, brg_ref, wig_ref, big_ref, lam_ref = params
    cp = RET_CHUNK
    hk = RET_HEADS * RET_DK
    v_off, g_off = 2 * hk, 2 * hk + RET_HEADS * RET_DV
    xl_off = g_off + RET_HEADS * RET_DV
    yl_off = xl_off + LRU_WIDTH
    lane = lax.broadcasted_iota(jnp.int32, (cp, LANES), 1)
    first = lane < RET_DK
    srow_first = lax.broadcasted_iota(jnp.int32, (2 * RET_DK, RET_DV), 0) < RET_DK
    nl = -lam_ref[...]
    neg_c_softplus = (-LRU_C) * (jnp.maximum(nl, 0.0) + jnp.log1p(jnp.exp(-jnp.abs(nl))))

    def retention(bi, pairs):
        for p in pairs:
            q2 = _pad_rows(get(bi, slice(p * LANES, (p + 1) * LANES)), cp)
            k2 = _pad_rows(get(bi, slice(hk + p * LANES, hk + (p + 1) * LANES)), cp) * (RET_DK ** -0.5)
            kb = k2.astype(BF16)
            kdt = jnp.transpose(k2 * kdec_ref[p]).astype(BF16)
            s2 = s_scr[bi, p]
            s2b = s2.astype(BF16)
            new_s = []
            for hh in range(2):
                h = 2 * p + hh
                sel = first if hh == 0 else jnp.logical_not(first)
                qh = jnp.where(sel, q2, 0.0)
                v = _pad_rows(get(bi, slice(v_off + h * RET_DV, v_off + (h + 1) * RET_DV)), cp)
                vb = v.astype(BF16)
                att = lax.dot_general(qh.astype(BF16), kb, (((1,), (1,)), ((), ())),
                                      preferred_element_type=F32) * dec_ref[h]
                o = (jnp.dot(att.astype(BF16), vb, preferred_element_type=F32)
                     + jnp.dot((qh * qdec_ref[p]).astype(BF16), s2b, preferred_element_type=F32))
                new_s.append(jnp.dot(kdt, vb, preferred_element_type=F32))
                o = o[:rows]
                o = o * lax.rsqrt(jnp.mean(o * o, axis=-1, keepdims=True) + EPS) * retg_ref[h:h + 1, :]
                gate = get(bi, slice(g_off + h * RET_DV, g_off + (h + 1) * RET_DV))
                put(bi, slice(h * RET_DV, (h + 1) * RET_DV), o * (gate * jax.nn.sigmoid(gate)))
            s_scr[bi, p] = s2 * cdec_ref[p] + jnp.where(srow_first, new_s[0], new_s[1])

    def conv_gates(bi, live):
        x = get(bi, slice(xl_off, xl_off + LRU_WIDTH))
        xe = jnp.concatenate([xe_scr[bi], x], axis=0)
        xc = convb_ref[...]
        for j in range(CONV_WIDTH):
            back = CONV_WIDTH - 1 - j
            xj = x if back == 0 else pltpu.roll(xe, back, axis=0)[SUBLANES:]
            xc = xc + xj * convw_ref[j:j + 1, :]
        xe_scr[bi] = xe[rows:]
        cnew_ref[bi] = xe_scr[bi, pl.ds(SUBLANES - (CONV_WIDTH - 1), CONV_WIDTH - 1), :]

        xcb = _pad_rows(xc, max(rows, 2 * SUBLANES)).astype(BF16)
        r_parts, i_parts = [], []
        for n in range(LRU_BLOCKS):
            xb = xcb[:, n * LRU_BLOCK:(n + 1) * LRU_BLOCK]
            r_parts.append(jnp.dot(xb, wrg_ref[n], preferred_element_type=F32)[:rows])
            i_parts.append(jnp.dot(xb, wig_ref[n], preferred_element_type=F32)[:rows])
        live["xc"] = xc
        live["r"] = jax.nn.sigmoid(jnp.concatenate(r_parts, axis=1) + brg_ref[...])
        live["ig"] = jax.nn.sigmoid(jnp.concatenate(i_parts, axis=1) + big_ref[...])

    def scan(bi, live):
        xc, r, ig = live["xc"], live["r"], live["ig"]
        a = jnp.exp(r * neg_c_softplus)
        one_m = 1.0 - a * a
        root = jnp.where(one_m > 0.0, one_m * lax.rsqrt(one_m), 0.0)
        hseq = _linear_scan(a, root * (ig * xc), h_scr[bi])
        h_scr[bi] = hseq[rows - 1:rows, :]
        hlast_ref[bi] = hseq[rows - 1:rows, :]
        live["hseq"] = hseq

    def gated_out(bi, live):
        hseq = live["hseq"]
        yl = get(bi, slice(yl_off, yl_off + LRU_WIDTH))
        half = 0.5 * yl
        gelu = half + half * jnp.tanh(yl * (GELU_C + (0.044715 * GELU_C) * (yl * yl)))
        put(bi, slice(RET_HEADS * RET_DV, RET_HEADS * RET_DV + LRU_WIDTH), hseq * gelu)

    def finish():
        sret_ref[...] = s_scr[...]

    stages = []
    for bi in range(nseq):
        live = {}
        stages += [functools.partial(retention, bi, (p,)) for p in range(RET_HEADS // 2)]
        stages += [functools.partial(f, bi, live) for f in (conv_gates, scan, gated_out)]
    return stages + [finish]


def _rec_mix_kernel(proj_ref, s0_ref, h0_ref, c0_ref, dec_ref, qdec_ref, kdec_ref, cdec_ref,
                    retg_ref, convw_ref, convb_ref, wrg_ref, brg_ref, wig_ref, big_ref, lam_ref,
                    mix_ref, sret_ref, hlast_ref, cnew_ref,
                    s_scr, h_scr, xe_scr, *, rows):
    @pl.when(pl.program_id(1) == 0)
    def _():
        _mixer_init(s0_ref, h0_ref, c0_ref, s_scr, h_scr, xe_scr)

    def put(bi, cols, val):
        mix_ref[bi, :, cols] = val.astype(mix_ref.dtype)

    _mixer(lambda bi, cols: proj_ref[bi, :, cols], put, proj_ref.shape[0], rows,
           (dec_ref, qdec_ref, kdec_ref, cdec_ref),
           (retg_ref, convw_ref, convb_ref, wrg_ref, brg_ref, wig_ref, big_ref, lam_ref),
           s_scr, h_scr, xe_scr, sret_ref, hlast_ref, cnew_ref)


def _ret_tables(t_eff):
    cp = RET_CHUNK
    f32 = np.float32
    log_g = np.log1p(-np.power(f32(2.0), f32(-5.0) - np.arange(RET_HEADS, dtype=f32)))
    idx = np.arange(cp, dtype=f32)
    live = idx < t_eff
    diff = idx[:, None] - idx[None, :]
    ok = (diff >= 0) & live[:, None] & live[None, :]
    decay = np.where(ok, np.exp(log_g[:, None, None] * np.maximum(diff, f32(0.0))), f32(0.0))
    q_dec = np.where(live, np.exp(log_g[:, None] * (idx + f32(1.0))), f32(0.0))
    k_dec = np.where(live, np.exp(log_g[:, None] * (f32(t_eff - 1.0) - idx)), f32(0.0))
    c_dec = np.exp(log_g * f32(t_eff))

    def pair_lanes(t):
        t = t.reshape(RET_HEADS // 2, 2, cp)
        return np.repeat(np.transpose(t, (0, 2, 1)), RET_DK, axis=2)

    c_rows = np.repeat(c_dec.reshape(RET_HEADS // 2, 2), RET_DK, axis=1)
    c_rows = np.broadcast_to(c_rows[:, :, None], (RET_HEADS // 2, 2 * RET_DK, RET_DV))
    as32 = lambda t: np.ascontiguousarray(t, dtype=f32)
    return as32(decay), as32(pair_lanes(q_dec)), as32(pair_lanes(k_dec)), as32(c_rows)


def _rec_mix(proj, s0, h0, c0, ret_g, conv_w, conv_b, w_rg, b_rg, w_ig, b_ig, lam):
    b, t, _ = proj.shape
    rows = RET_CHUNK if t % RET_CHUNK == 0 else t
    assert t % rows == 0 and rows % SUBLANES == 0 and (t == rows or rows == RET_CHUNK)
    n = t // rows
    nb = REC_SEQS if b % REC_SEQS == 0 else 1
    dec, qdec, kdec, cdec = _ret_tables(rows)
    hp = RET_HEADS // 2
    w = LRU_WIDTH
    per_b3 = lambda i, c: (i, 0, 0)
    outs = pl.pallas_call(
        functools.partial(_rec_mix_kernel, rows=rows),
        grid=(b // nb, n),
        in_specs=[
            pl.BlockSpec((nb, rows, REC_IN), lambda i, c: (i, c, 0)),
            pl.BlockSpec((nb, hp, 2 * RET_DK, RET_DV), lambda i, c: (i, 0, 0, 0)),
            pl.BlockSpec((nb, 1, w), per_b3),
            pl.BlockSpec((nb, CONV_WIDTH - 1, w), per_b3),
            _const_spec(dec.shape), _const_spec(qdec.shape), _const_spec(kdec.shape), _const_spec(cdec.shape),
            _const_spec((RET_HEADS, RET_DV)), _const_spec((CONV_WIDTH, w)), _const_spec((1, w)),
            _const_spec((LRU_BLOCKS, LRU_BLOCK, LRU_BLOCK)), _const_spec((1, w)),
            _const_spec((LRU_BLOCKS, LRU_BLOCK, LRU_BLOCK)), _const_spec((1, w)), _const_spec((1, w)),
        ],
        out_specs=[
            pl.BlockSpec((nb, rows, D_MODEL), lambda i, c: (i, c, 0)),
            pl.BlockSpec((nb, hp, 2 * RET_DK, RET_DV), lambda i, c: (i, 0, 0, 0)),
            pl.BlockSpec((nb, 1, w), per_b3),
            pl.BlockSpec((nb, CONV_WIDTH - 1, w), per_b3),
        ],
        out_shape=[
            jax.ShapeDtypeStruct((b, t, D_MODEL), BF16),
            jax.ShapeDtypeStruct((b, hp, 2 * RET_DK, RET_DV), F32),
            jax.ShapeDtypeStruct((b, 1, w), F32),
            jax.ShapeDtypeStruct((b, CONV_WIDTH - 1, w), F32),
        ],
        scratch_shapes=[
            pltpu.VMEM((nb, hp, 2 * RET_DK, RET_DV), F32),
            pltpu.VMEM((nb, 1, w), F32),
            pltpu.VMEM((nb, SUBLANES, w), F32),
        ],
        compiler_params=_cparams(("arbitrary", "arbitrary")),
        name="rec_mix",
    )(proj, s0.reshape(b, hp, 2 * RET_DK, RET_DV), h0.reshape(b, 1, w), c0,
      dec, qdec, kdec, cdec, ret_g, conv_w, conv_b.reshape(1, w),
      w_rg.astype(BF16), b_rg.reshape(1, w), w_ig.astype(BF16), b_ig.reshape(1, w), lam.reshape(1, w))
    mix, s_ret, h_last, c_new = outs
    return mix, s_ret.reshape(b, RET_HEADS, RET_DK, RET_DV), h_last.reshape(b, w), c_new


def _ffn_stages(load_mix, load_x, store, ffn_refs, act_ref, rs):
    wo_ref, g_ref, wg_ref, wu_ref, wd_ref = ffn_refs
    live = {}

    def prologue():
        live["y"] = load_x() + jnp.dot(load_mix(), wo_ref[...], preferred_element_type=F32)
        live["h"] = _rmsnorm(live["y"], g_ref[...]).astype(BF16)

    def gate(c):
        live["gate"] = jnp.dot(live["h"], wg_ref[:, c * FF_CHUNK:(c + 1) * FF_CHUNK], preferred_element_type=F32)

    def up(c):
        cols = slice(c * FF_CHUNK, (c + 1) * FF_CHUNK)
        g = live.pop("gate")
        act_ref[rs, cols] = (g * jax.nn.sigmoid(g) * jnp.dot(live["h"], wu_ref[:, cols],
                                                                preferred_element_type=F32)).astype(BF16)

    def down(c):
        cols = slice(c * MXU_COLS, (c + 1) * MXU_COLS)
        store(cols, live["y"][:, cols] + jnp.dot(act_ref[rs, :], wd_ref[:, cols], preferred_element_type=F32))

    stages = [prologue]
    for c in range(wg_ref.shape[1] // FF_CHUNK):
        stages += [functools.partial(gate, c), functools.partial(up, c)]
    return stages + [functools.partial(down, c) for c in range(wd_ref.shape[1] // MXU_COLS)]


def _out_ffn_kernel(mix_ref, x_ref, wo_ref, g_ref, wg_ref, wu_ref, wd_ref, o_ref, act_ref):
    def store_rows(rs, cols, val):
        o_ref[rs, cols] = val

    for rs in _row_parts(x_ref.shape[0]):
        _run_stages(_ffn_stages(lambda rs=rs: mix_ref[rs, :].astype(BF16), lambda rs=rs: x_ref[rs, :],
                                functools.partial(store_rows, rs),
                                (wo_ref, g_ref, wg_ref, wu_ref, wd_ref), act_ref, rs))


def _layer_spec(stacked, layer):
    shape = stacked.shape[1:]
    return pl.BlockSpec((None,) + shape, lambda *_: (layer,) + (0,) * len(shape), pipeline_mode=pl.Buffered(1))


def _out_ffn(mix2d, x2d, w_out, g, ffn_stacked, layer):
    m, d = x2d.shape
    tm = min(WIDE_TILE, m)
    row = lambda i: (i, 0)
    return pl.pallas_call(
        _out_ffn_kernel,
        grid=(m // tm,),
        in_specs=[pl.BlockSpec((tm, d), row), pl.BlockSpec((tm, d), row), _const_spec(w_out.shape),
                  _const_spec((1, d))] + [_layer_spec(w, layer) for w in ffn_stacked],
        out_specs=pl.BlockSpec((tm, d), row),
        out_shape=jax.ShapeDtypeStruct((m, d), F32),
        scratch_shapes=[pltpu.VMEM((tm, ffn_stacked[0].shape[2]), BF16)],
        compiler_params=_cparams(("arbitrary",)),
        name="out_ffn",
    )(mix2d, x2d, w_out, g.reshape(1, d), *ffn_stacked)


def _layer0_kernel(xc_ref, xp_ref, gm_ref, win_ref, s0_ref, h0_ref, c0_ref,
                   dec_ref, qdec_ref, kdec_ref, cdec_ref,
                   retg_ref, convw_ref, convb_ref, wrg_ref, brg_ref, wig_ref, big_ref, lam_ref,
                   wo_ref, g_ref, wg_ref, wu_ref, wd_ref,
                   y_ref, sret_ref, hlast_ref, cnew_ref,
                   s_scr, h_scr, xe_scr, mix_scr, act_scr):
    c = pl.program_id(1)
    last = pl.num_programs(1) - 1
    nseq, rows, d = xc_ref.shape
    tables = (dec_ref, qdec_ref, kdec_ref, cdec_ref)
    params = (retg_ref, convw_ref, convb_ref, wrg_ref, brg_ref, wig_ref, big_ref, lam_ref)
    ffn_refs = (wo_ref, g_ref, wg_ref, wu_ref, wd_ref)

    def mix_stages(slot):
        live = {}

        def project():
            x = xc_ref[...].reshape(nseq * rows, d)
            live["proj"] = jnp.dot(_rmsnorm(x, gm_ref[...]).astype(BF16), win_ref[...], preferred_element_type=F32)

        def put(bi, cols, val):
            mix_scr[slot, bi * rows:(bi + 1) * rows, cols] = val.astype(mix_scr.dtype)

        return [project] + _mixer_stages(lambda bi, cols: live["proj"][bi * rows:(bi + 1) * rows, cols], put,
                                         nseq, rows, tables, params, s_scr, h_scr, xe_scr,
                                         sret_ref, hlast_ref, cnew_ref)

    def ffn_stages(slot):
        def store(cols, val):
            y_ref[:, :, cols] = val.reshape(nseq, rows, val.shape[1])

        return _ffn_stages(lambda: mix_scr[slot], lambda: xp_ref[...].reshape(nseq * rows, d), store,
                           ffn_refs, act_scr, slice(0, nseq * rows))

    @pl.when(c == 0)
    def _():
        _mixer_init(s0_ref, h0_ref, c0_ref, s_scr, h_scr, xe_scr)
        _run_stages(mix_stages(0))

    @pl.when(jnp.logical_and(c > 0, c < last))
    def _():
        _run_alternating(mix_stages(c % 2), ffn_stages(1 - c % 2))

    @pl.when(c == last)
    def _():
        _run_stages(ffn_stages(1 - last % 2))


def _layer0_prompt(x, g_mix, w_in, s0, h0, c0, rec_prm, w_out, g_ffn, ffn_stacked, layer):
    ret_g, conv_w, conv_b, w_rg, b_rg, w_ig, b_ig, lam = rec_prm
    b, t, d = x.shape
    rows = RET_CHUNK
    assert t % rows == 0 and b % REC_SEQS == 0
    n = t // rows
    nb = REC_SEQS
    dec, qdec, kdec, cdec = _ret_tables(rows)
    hp = RET_HEADS // 2
    w = LRU_WIDTH
    per_b3 = lambda i, c: (i, 0, 0)
    chunk = (nb, rows, d)
    outs = pl.pallas_call(
        _layer0_kernel,
        grid=(b // nb, n + 1),
        in_specs=[
            pl.BlockSpec(chunk, lambda i, c: (i, jnp.minimum(c, n - 1), 0)),
            pl.BlockSpec(chunk, lambda i, c: (i, jnp.maximum(c - 1, 0), 0)),
            _const_spec((1, d)), _const_spec(w_in.shape),
            pl.BlockSpec((nb, hp, 2 * RET_DK, RET_DV), lambda i, c: (i, 0, 0, 0)),
            pl.BlockSpec((nb, 1, w), per_b3),
            pl.BlockSpec((nb, CONV_WIDTH - 1, w), per_b3),
            _const_spec(dec.shape), _const_spec(qdec.shape), _const_spec(kdec.shape), _const_spec(cdec.shape),
            _const_spec((RET_HEADS, RET_DV)), _const_spec((CONV_WIDTH, w)), _const_spec((1, w)),
            _const_spec((LRU_BLOCKS, LRU_BLOCK, LRU_BLOCK)), _const_spec((1, w)),
            _const_spec((LRU_BLOCKS, LRU_BLOCK, LRU_BLOCK)), _const_spec((1, w)), _const_spec((1, w)),
            _const_spec(w_out.shape), _const_spec((1, d)),
        ] + [_layer_spec(wt, layer) for wt in ffn_stacked],
        out_specs=[
            pl.BlockSpec(chunk, lambda i, c: (i, jnp.maximum(c - 1, 0), 0)),
            pl.BlockSpec((nb, hp, 2 * RET_DK, RET_DV), lambda i, c: (i, 0, 0, 0)),
            pl.BlockSpec((nb, 1, w), per_b3),
            pl.BlockSpec((nb, CONV_WIDTH - 1, w), per_b3),
        ],
        out_shape=[
            jax.ShapeDtypeStruct((b, t, d), F32),
            jax.ShapeDtypeStruct((b, hp, 2 * RET_DK, RET_DV), F32),
            jax.ShapeDtypeStruct((b, 1, w), F32),
            jax.ShapeDtypeStruct((b, CONV_WIDTH - 1, w), F32),
        ],
        scratch_shapes=[
            pltpu.VMEM((nb, hp, 2 * RET_DK, RET_DV), F32),
            pltpu.VMEM((nb, 1, w), F32),
            pltpu.VMEM((nb, SUBLANES, w), F32),
            pltpu.VMEM((2, nb * rows, d), BF16),
            pltpu.VMEM((nb * rows, ffn_stacked[0].shape[2]), BF16),
        ],
        compiler_params=_cparams(("arbitrary", "arbitrary")),
        name="layer0",
    )(x, x, g_mix.reshape(1, d), w_in, s0.reshape(b, hp, 2 * RET_DK, RET_DV), h0.reshape(b, 1, w), c0,
      dec, qdec, kdec, cdec, ret_g, conv_w, conv_b.reshape(1, w),
      w_rg.astype(BF16), b_rg.reshape(1, w), w_ig.astype(BF16), b_ig.reshape(1, w), lam.reshape(1, w),
      w_out, g_ffn.reshape(1, d), *ffn_stacked)
    y, s_ret, h_last, c_new = outs
    return y, s_ret.reshape(b, RET_HEADS, RET_DK, RET_DV), h_last.reshape(b, w), c_new


def _pair_rms(a, g2, first):
    sq = a * a
    s_a = jnp.sum(jnp.where(first, sq, 0.0), axis=-1, keepdims=True)
    s_b = jnp.sum(jnp.where(first, 0.0, sq), axis=-1, keepdims=True)
    ms = jnp.where(first, s_a, s_b) * (1.0 / ATT_DH)
    return a * lax.rsqrt(ms + EPS) * g2


def _qkv_slabs(x, g_ref, w_ref, qg_ref, kg_ref):
    h = _rmsnorm(x, g_ref[...]).astype(BF16)
    first = lax.broadcasted_iota(jnp.int32, (x.shape[0], LANES), 1) < ATT_DH
    slabs = []
    for c in range(w_ref.shape[1] // MXU_COLS):
        part = jnp.dot(h, w_ref[:, c * MXU_COLS:(c + 1) * MXU_COLS], preferred_element_type=F32)
        slabs += [part[:, i * LANES:(i + 1) * LANES] for i in range(MXU_COLS // LANES)]
    q = [_pair_rms(slabs[p], qg_ref[...], first) for p in range(NPAIR)]
    k = [_pair_rms(slabs[NPAIR + p], kg_ref[...], first) for p in range(NPAIR)]
    v = slabs[2 * NPAIR:]
    return q, k, v


def _qkv_flat_kernel(x_ref, g_ref, w_ref, qg_ref, kg_ref, q_ref, k_ref, v_ref):
    q, k, v = _qkv_slabs(x_ref[...], g_ref, w_ref, qg_ref, kg_ref)
    q_ref[...] = jnp.concatenate(q, axis=1)
    k_ref[...] = jnp.concatenate(k, axis=1)
    v_ref[...] = jnp.concatenate(v, axis=1)


def _qkv_perm_kernel(x_ref, g_ref, w_ref, qg_ref, kg_ref, q_ref, k_ref, v_ref, xs_ref):
    for p in range(NPAIR):
        xs_ref[p] = x_ref[:, p * LANES:(p + 1) * LANES]
    for rs in _row_parts(x_ref.shape[0]):
        n = (rs.stop - rs.start) // MAX_DIL
        x = jnp.concatenate(
            [jnp.concatenate([xs_ref[p, pl.ds(rs.start + r, n, stride=MAX_DIL), :] for p in range(NPAIR)], axis=1)
             for r in range(MAX_DIL)], axis=0)
        q, k, v = _qkv_slabs(x, g_ref, w_ref, qg_ref, kg_ref)
        dst = slice(rs.start // MAX_DIL, rs.stop // MAX_DIL)
        for ref, slabs in ((q_ref, q), (k_ref, k), (v_ref, v)):
            for p in range(NPAIR):
                for r in range(MAX_DIL):
                    ref[p, r, dst] = slabs[p][r * n:(r + 1) * n]


def _head_gain(g):
    return jnp.tile(g.reshape(1, ATT_DH), (1, LANES // ATT_DH))


def _qkv_flat(x2d, g, w_bf, qg, kg):
    m, d = x2d.shape
    tm = min(TOKEN_TILE, m)
    row = lambda i: (i, 0)
    return pl.pallas_call(
        _qkv_flat_kernel,
        grid=(m // tm,),
        in_specs=[pl.BlockSpec((tm, d), row), _const_spec((1, d)), _const_spec(w_bf.shape),
                  _const_spec((1, LANES)), _const_spec((1, LANES))],
        out_specs=[pl.BlockSpec((tm, d), row)] * 3,
        out_shape=[jax.ShapeDtypeStruct((m, d), F32)] * 3,
        compiler_params=_cparams(("arbitrary",)),
        name="qkv_flat",
    )(x2d, g.reshape(1, d), w_bf, _head_gain(qg), _head_gain(kg))


def _qkv_perm(x, g, w_bf, qg, kg):
    b, t, d = x.shape
    ns = t // SUPER
    tm = WIDE_TILE
    nt = SUPER // tm
    n = tm // MAX_DIL
    out_spec = pl.BlockSpec((None, NPAIR, None, MAX_DIL, None, n, LANES), lambda i, s, j: (i, 0, s, 0, j, 0, 0))
    out_shape = jax.ShapeDtypeStruct((b, NPAIR, ns, MAX_DIL, nt, n, LANES), F32)
    q, k, v = pl.pallas_call(
        _qkv_perm_kernel,
        grid=(b, ns, nt),
        in_specs=[pl.BlockSpec((None, tm, d), lambda i, s, j: (i, s * nt + j, 0)),
                  _const_spec((1, d)), _const_spec(w_bf.shape), _const_spec((1, LANES)), _const_spec((1, LANES))],
        out_specs=[out_spec] * 3,
        out_shape=[out_shape] * 3,
        scratch_shapes=[pltpu.VMEM((NPAIR, tm, LANES), F32)],
        compiler_params=_cparams(("arbitrary",) * 3),
        name="qkv_perm",
    )(x, g.reshape(1, d), w_bf, _head_gain(qg), _head_gain(kg))
    return tuple(a.reshape(b, NPAIR, t, LANES) for a in (q, k, v))


def _alibi_slopes():
    return np.power(np.float32(2.0), np.float32(-8.0) * np.arange(1, ATT_HEADS + 1, dtype=np.float32) / ATT_HEADS)


PREV_SLOTS = 32
ATT_UNROLL = 16
LOG2E = 1.4426950408889634


def _band_bias():
    i = np.arange(ATT_BLOCK)
    c = np.arange(2 * ATT_BLOCK)
    q_idx = {1: MAX_DIL * (i % 8) + i // 8,
             4: 4 * (i % 32) + i // 32,
             16: i}
    k_idx = {1: (MAX_DIL * (c % 8) + c // 16, (c // 8) % 2 == 0),
             4: (4 * (c % 32) + c // 64, (c // 32) % 2 == 0),
             16: (c % ATT_BLOCK, c < ATT_BLOCK)}
    slopes = _alibi_slopes().reshape(ATT_HEADS // 2, 2, 1, 1)
    neg = np.float32(NEG)
    out = []
    for (w, d) in PATTERNS:
        span = w // d
        km, is_prev = k_idx[d]
        diff = q_idx[d][:, None] - (km - np.where(is_prev, ATT_BLOCK, 0))[None, :]
        valid = (diff >= 0) & (diff <= span)
        bias = -slopes * (d * diff).astype(np.float32)[None, None]
        bias = bias * np.float32(LOG2E)
        variants = [np.where(valid, bias, neg), np.where(valid & ~is_prev[None, :], bias, neg)]
        out.append(np.stack([v.reshape(ATT_HEADS // 2, 2 * ATT_BLOCK, 2 * ATT_BLOCK) for v in variants]))
    return np.stack(out).astype(np.float32)


def _attend_pair(q, k, v, bias, first):
    qs = jnp.concatenate([jnp.where(first, q, 0.0), jnp.where(first, 0.0, q)], axis=0).astype(BF16)
    s = lax.dot_general(qs, k.astype(BF16), (((1,), (1,)), ((), ())), preferred_element_type=F32) + bias
    m = jnp.max(s, axis=-1, keepdims=True)
    e = jnp.exp2(s - m).astype(BF16)
    vb = v.astype(BF16)
    r = jnp.dot(e, jnp.concatenate([vb, jnp.ones_like(vb)], axis=1), preferred_element_type=F32)
    halves = lambda a: jnp.where(first, a[:ATT_BLOCK], a[ATT_BLOCK:])
    return halves(r[:, :LANES]), halves(m), halves(r[:, LANES:])


def _attn_prompt_kernel(q_ref, kc_ref, kp_ref, vc_ref, vp_ref, bias_ref, o_ref, ko_ref, vo_ref,
                        kext, vext, oacc, macc, dacc, nat_ref):
    sb = pl.program_id(2)
    first_sb = (sb == 0).astype(jnp.int32)
    first = lax.broadcasted_iota(jnp.int32, (ATT_BLOCK, LANES), 1) < ATT_DH
    scale = (ATT_DH ** -0.5) * LOG2E
    ps = PREV_SLOTS

    for src_p, src_c, ext in ((kp_ref, kc_ref, kext), (vp_ref, vc_ref, vext)):
        for r in range(MAX_DIL):
            ext[r, 0:ps, :] = src_p[r * ATT_BLOCK + ATT_BLOCK - ps:(r + 1) * ATT_BLOCK, :]
            ext[r, ps:ps + ATT_BLOCK, :] = src_c[r * ATT_BLOCK:(r + 1) * ATT_BLOCK, :]

    def attend_store(p, variant, q_chunks, k, v):
        n = ATT_BLOCK // len(q_chunks)
        q = jnp.concatenate([q_ref[sl, :] for sl in q_chunks], axis=0) * scale
        o, m, den = _attend_pair(q, k, v, bias_ref[p, variant], first)
        for ci, sl in enumerate(q_chunks):
            oacc[p, sl, :] = o[ci * n:(ci + 1) * n]
            macc[p, sl, :] = m[ci * n:(ci + 1) * n]
            dacc[p, sl, :] = den[ci * n:(ci + 1) * n]

    n0 = ATT_BLOCK // MAX_DIL

    def block0(bi):
        base = pl.multiple_of(bi * n0, n0)
        ksl = pl.ds(base + ps - n0, 2 * n0)
        k = jnp.concatenate([kext[r, ksl, :] for r in range(MAX_DIL)], axis=0)
        v = jnp.concatenate([vext[r, ksl, :] for r in range(MAX_DIL)], axis=0)
        attend_store(0, first_sb * jnp.asarray(bi == 0, jnp.int32),
                     [pl.ds(r * ATT_BLOCK + base, n0) for r in range(MAX_DIL)], k, v)

    d1 = PATTERNS[1][1]
    nc1 = MAX_DIL // d1
    n1 = ATT_BLOCK // nc1

    def block1(idx):
        r4, mb = idx // (ATT_BLOCK // n1), idx % (ATT_BLOCK // n1)
        base = pl.multiple_of(mb * n1, n1)
        ksl = pl.ds(base + ps - n1, 2 * n1)
        k = jnp.concatenate([kext[r4 + d1 * c, ksl, :] for c in range(nc1)], axis=0)
        v = jnp.concatenate([vext[r4 + d1 * c, ksl, :] for c in range(nc1)], axis=0)
        attend_store(1, first_sb * jnp.asarray(mb == 0, jnp.int32),
                     [pl.ds(pl.multiple_of((r4 + d1 * c) * ATT_BLOCK + base, n1), n1) for c in range(nc1)], k, v)

    def block2(r):
        sl = pl.ds(pl.multiple_of(r * ATT_BLOCK, ATT_BLOCK), ATT_BLOCK)
        k = jnp.concatenate([kp_ref[sl, :], kc_ref[sl, :]], axis=0)
        v = jnp.concatenate([vp_ref[sl, :], vc_ref[sl, :]], axis=0)
        attend_store(2, first_sb, [sl], k, v)

    for block in (block0, block1, block2):
        def body(it, carry, block=block):
            for u in range(ATT_UNROLL):
                block(it * ATT_UNROLL + u)
            return carry

        lax.fori_loop(0, MAX_DIL // ATT_UNROLL, body, 0)

    def comb(r, carry):
        sl = pl.ds(pl.multiple_of(r * ATT_BLOCK, ATT_BLOCK), ATT_BLOCK)
        m0, m1, m2 = macc[0, sl, :], macc[1, sl, :], macc[2, sl, :]
        mx = jnp.maximum(jnp.maximum(m0, m1), m2)
        w0, w1, w2 = jnp.exp2(m0 - mx), jnp.exp2(m1 - mx), jnp.exp2(m2 - mx)
        num = w0 * oacc[0, sl, :] + w1 * oacc[1, sl, :] + w2 * oacc[2, sl, :]
        den = w0 * dacc[0, sl, :] + w1 * dacc[1, sl, :] + w2 * dacc[2, sl, :]
        o_ref[pl.ds(r, ATT_BLOCK, stride=MAX_DIL), :] = num / den
        return carry

    lax.fori_loop(0, SUPER // ATT_BLOCK, comb, 0)

    @pl.when(sb == pl.num_programs(2) - 1)
    def _():
        for src, dst in ((kc_ref, ko_ref), (vc_ref, vo_ref)):
            for r in range(MAX_DIL):
                nat_ref[pl.ds(r, ATT_BLOCK, stride=MAX_DIL), :] = src[r * ATT_BLOCK:(r + 1) * ATT_BLOCK, :]
            dst[...] = jnp.transpose(nat_ref[...])


def _attn_prompt(q, k, v):
    b, npair, t, _ = q.shape
    ns = t // SUPER
    bias = _band_bias()
    cur = pl.BlockSpec((None, None, SUPER, LANES), lambda i, p, s: (i, p, s, 0))
    prev = pl.BlockSpec((None, None, SUPER, LANES), lambda i, p, s: (i, p, jnp.maximum(s - 1, 0), 0))
    cache = pl.BlockSpec((None, LANES, SUPER), lambda i, p, s: (i, p, 0))
    cache_shape = jax.ShapeDtypeStruct((b, npair * LANES, SUPER), F32)
    return pl.pallas_call(
        _attn_prompt_kernel,
        grid=(b, npair, ns),
        in_specs=[cur, cur, prev, cur, prev,
                  pl.BlockSpec((len(PATTERNS), 2, None, 2 * ATT_BLOCK, 2 * ATT_BLOCK),
                               lambda i, p, s: (0, 0, p, 0, 0))],
        out_specs=[pl.BlockSpec((None, SUPER, LANES), lambda i, p, s: (i, s, p)), cache, cache],
        out_shape=[jax.ShapeDtypeStruct((b, t, npair * LANES), F32), cache_shape, cache_shape],
        scratch_shapes=[pltpu.VMEM((MAX_DIL, PREV_SLOTS + ATT_BLOCK, LANES), F32),
                        pltpu.VMEM((MAX_DIL, PREV_SLOTS + ATT_BLOCK, LANES), F32),
                        pltpu.VMEM((len(PATTERNS), SUPER, LANES), F32),
                        pltpu.VMEM((len(PATTERNS), SUPER, LANES), F32),
                        pltpu.VMEM((len(PATTERNS), SUPER, LANES), F32),
                        pltpu.VMEM((SUPER, LANES), F32)],
        compiler_params=_cparams(("arbitrary",) * 3),
        name="attn_prompt",
    )(q, k, k, v, v, bias)


QUAD = 4
NEW_PAD = 128


def _sample_bias(s_len, n_past):
    s = np.arange(s_len)
    col = np.arange(n_past + NEW_PAD)
    dist = (n_past + s)[:, None] - col[None, :]
    real = (col < n_past + s_len)[None, :]
    slopes = _alibi_slopes().reshape(ATT_HEADS // QUAD, QUAD, 1, 1)
    out = []
    for (w, d) in PATTERNS:
        valid = (dist >= 0) & (dist % d == 0) & (dist <= w) & real
        bias = np.where(valid[None, None], -slopes * dist.astype(np.float32)[None, None], np.float32(NEG))
        out.append(bias.reshape(ATT_HEADS // QUAD, QUAD * s_len, n_past + NEW_PAD))
    return np.stack(out).astype(np.float32)


def _sample_stages(q_ref, kn_ref, vn_ref, ck_ref, cv_ref, bias_ref, o_ref, quad0, s_len):
    n_past = ck_ref.shape[1]
    width = QUAD * ATT_DH
    rows = QUAD * s_len
    nt = (((1,), (1,)), ((), ()))

    def own():
        return (lax.broadcasted_iota(jnp.int32, (rows, width), 0) // s_len
                == lax.broadcasted_iota(jnp.int32, (rows, width), 1) // ATT_DH)

    def scores(ql, live):
        cols = slice(ql * width, (ql + 1) * width)
        q = q_ref[:, cols] * (ATT_DH ** -0.5)
        qm = jnp.where(own(), jnp.concatenate([q] * QUAD, axis=0), 0.0).astype(BF16)
        kn = _pad_rows(kn_ref[:, cols], NEW_PAD).astype(BF16)
        live["s_c"] = jnp.dot(qm, ck_ref[cols, :].astype(BF16), preferred_element_type=F32)
        live["s_n"] = lax.dot_general(qm, kn, nt, preferred_element_type=F32)
        live["parts"] = []

    def pattern(ql, p, live):
        sc = live["s_c"] + bias_ref[p, quad0 + ql, :, 0:n_past]
        sn = live["s_n"] + bias_ref[p, quad0 + ql, :, n_past:n_past + NEW_PAD]
        m = jnp.maximum(jnp.max(sc, axis=-1, keepdims=True), jnp.max(sn, axis=-1, keepdims=True))
        ec, en = jnp.exp(sc - m), jnp.exp(sn - m)
        den = jnp.sum(ec, axis=-1, keepdims=True) + jnp.sum(en, axis=-1, keepdims=True)
        live["parts"].append((ec, en, den, m + jnp.log(den)))

    def combine(ql, live):
        cols = slice(ql * width, (ql + 1) * width)
        es_c, es_n, dens, lses = zip(*live["parts"])
        mx = jnp.maximum(jnp.maximum(lses[0], lses[1]), lses[2])
        ws = [jnp.exp(l - mx) for l in lses]
        wsum = ws[0] + ws[1] + ws[2]
        coef = [w / (wsum * d) for w, d in zip(ws, dens)]
        c_c = coef[0] * es_c[0] + coef[1] * es_c[1] + coef[2] * es_c[2]
        c_n = coef[0] * es_n[0] + coef[1] * es_n[1] + coef[2] * es_n[2]
        vn = _pad_rows(vn_ref[:, cols], NEW_PAD).astype(BF16)
        o = (lax.dot_general(c_c.astype(BF16), cv_ref[cols, :].astype(BF16), nt, preferred_element_type=F32)
             + jnp.dot(c_n.astype(BF16), vn, preferred_element_type=F32))
        o = jnp.where(own(), o, 0.0)
        acc = o[0:s_len]
        for hh in range(1, QUAD):
            acc = acc + o[hh * s_len:(hh + 1) * s_len]
        o_ref[:, cols] = acc.astype(o_ref.dtype)

    stages = []
    for ql in range(q_ref.shape[1] // width):
        live = {}
        stages.append(functools.partial(scores, ql, live))
        stages += [functools.partial(pattern, ql, p, live) for p in range(len(PATTERNS))]
        stages.append(functools.partial(combine, ql, live))
    return stages


SAMPLE_QUADS = 2


def _ffn_sample_kernel(mix_ref, x_ref, wo_ref, g_ref, wg_ref, wu_ref, wd_ref,
                       q_ref, kn_ref, vn_ref, ck_ref, cv_ref, bias_ref, o_ref, os_ref, act_ref, *, s_len):
    def store(cols, val):
        o_ref[:, cols] = val

    steps_per_seq = bias_ref.shape[1] // SAMPLE_QUADS
    _run_alternating(
        _ffn_stages(lambda: mix_ref[...].astype(BF16), lambda: x_ref[...], store,
                    (wo_ref, g_ref, wg_ref, wu_ref, wd_ref), act_ref, slice(0, x_ref.shape[0])),
        _sample_stages(q_ref, kn_ref, vn_ref, ck_ref, cv_ref, bias_ref, os_ref,
                       (pl.program_id(0) % steps_per_seq) * SAMPLE_QUADS, s_len))


def _out_ffn_and_sample_attn(mix2d, x2d, w_out, g, ffn_stacked, layer, q, kn, vn, ck, cv):
    m, d = x2d.shape
    b, s_len, _ = q.shape
    n_past = ck.shape[2]
    width = SAMPLE_QUADS * QUAD * ATT_DH
    sps = d // width
    steps = b * sps
    assert m % steps == 0 and (m // steps) % SUBLANES == 0
    tm = m // steps
    bias = _sample_bias(s_len, n_past)
    row = lambda i: (i, 0)
    new = pl.BlockSpec((None, s_len, width), lambda i: (i // sps, 0, i % sps))
    old = pl.BlockSpec((None, width, n_past), lambda i: (i // sps, i % sps, 0))
    return pl.pallas_call(
        functools.partial(_ffn_sample_kernel, s_len=s_len),
        grid=(steps,),
        in_specs=[pl.BlockSpec((tm, d), row), pl.BlockSpec((tm, d), row), _const_spec(w_out.shape),
                  _const_spec((1, d))] + [_layer_spec(w, layer) for w in ffn_stacked]
                 + [new, new, new, old, old, _const_spec(bias.shape)],
        out_specs=[pl.BlockSpec((tm, d), row), new],
        out_shape=[jax.ShapeDtypeStruct((m, d), F32), jax.ShapeDtypeStruct((b, s_len, d), BF16)],
        scratch_shapes=[pltpu.VMEM((tm, ffn_stacked[0].shape[2]), BF16)],
        compiler_params=_cparams(("arbitrary",)),
        name="ffn_sample_attn",
    )(mix2d, x2d, w_out, g.reshape(1, d), *ffn_stacked, q, kn, vn, ck, cv, bias)


def kernel(x_prompt, x_sample, state_ret, state_lru, state_conv, cache_k, cache_v, norm_mix, norm_ffn, w_in_rec, w_out_rec, ret_norm_g, conv_w, conv_b, w_rgate, b_rgate, w_igate, b_igate, lru_lambda, w_in_att, w_out_att, q_norm_g, k_norm_g, w_ffn_gate, w_ffn_up, w_ffn_down):
    bp, tp, d = x_prompt.shape
    bs, ts, _ = x_sample.shape
    depth = norm_mix.shape[0]
    assert tp % SUPER == 0 and d == D_MODEL
    yp, ys = x_prompt, x_sample
    ffn_stacked = (w_ffn_gate.astype(BF16), w_ffn_up.astype(BF16), w_ffn_down.astype(BF16))
    ret_p, lru_p, conv_p, kp, vp = [], [], [], [], []
    ret_s, lru_s, conv_s, ksl, vsl = [], [], [], [], []
    for l in range(depth):
        i = l // 2
        if l % 2 == 0:
            ffn = (w_out_rec[i].astype(BF16), norm_ffn[l], ffn_stacked, l)
            w_in = w_in_rec[i].astype(BF16)
            rec_prm = (ret_norm_g[i], conv_w[i], conv_b[i], w_rgate[i], b_rgate[i], w_igate[i], b_igate[i],
                       lru_lambda[i])
            yp, sr, sl, sc = _layer0_prompt(yp, norm_mix[l], w_in, jnp.zeros((bp, RET_HEADS, RET_DK, RET_DV), F32),
                                            jnp.zeros((bp, LRU_WIDTH), F32),
                                            jnp.zeros((bp, CONV_WIDTH - 1, LRU_WIDTH), F32), rec_prm, *ffn)
            ret_p.append(sr); lru_p.append(sl); conv_p.append(sc)
            proj = _norm_proj(ys.reshape(bs * ts, d), norm_mix[l], w_in).reshape(bs, ts, REC_IN)
            mix, sr, sl, sc = _rec_mix(proj, state_ret[i], state_lru[i], state_conv[i], *rec_prm)
            ys = _out_ffn(mix.reshape(bs * ts, d), ys.reshape(bs * ts, d), *ffn).reshape(bs, ts, d)
            ret_s.append(sr); lru_s.append(sl); conv_s.append(sc)
        else:
            ffn = (w_out_att[i].astype(BF16), norm_ffn[l], ffn_stacked, l)
            w_in = w_in_att[i].astype(BF16)
            q, k, v = _qkv_perm(yp, norm_mix[l], w_in, q_norm_g[i], k_norm_g[i])
            o, kc, vc = _attn_prompt(q, k, v)
            from_slabs = lambda c: jnp.transpose(c.reshape(bp, ATT_HEADS, ATT_DH, SUPER), (0, 3, 1, 2))
            kp.append(from_slabs(kc)); vp.append(from_slabs(vc))
            qs, kn, vn = _qkv_flat(ys.reshape(bs * ts, d), norm_mix[l], w_in, q_norm_g[i], k_norm_g[i])
            n_past = cache_k.shape[2]
            to_slabs = lambda c: jnp.transpose(c, (0, 2, 3, 1)).reshape(bs, d, n_past)
            yp, os_ = _out_ffn_and_sample_attn(
                o.reshape(bp * tp, d), yp.reshape(bp * tp, d), *ffn,
                qs.reshape(bs, ts, d), kn.reshape(bs, ts, d), vn.reshape(bs, ts, d),
                to_slabs(cache_k[i]), to_slabs(cache_v[i]))
            yp = yp.reshape(bp, tp, d)
            ys = _out_ffn(os_.reshape(bs * ts, d), ys.reshape(bs * ts, d), *ffn).reshape(bs, ts, d)
            ksl.append(kn.reshape(bs, ts, ATT_HEADS, ATT_DH)); vsl.append(vn.reshape(bs, ts, ATT_HEADS, ATT_DH))
    return (yp, ys,
            jnp.stack(ret_p), jnp.stack(lru_p), jnp.stack(conv_p), jnp.stack(kp), jnp.stack(vp),
            jnp.stack(ret_s), jnp.stack(lru_s), jnp.stack(conv_s), jnp.stack(ksl), jnp.stack(vsl))
```

```python
import functools

import jax
import numpy as np
import jax.numpy as jnp
from jax import lax
from jax.experimental import pallas as pl
from jax.experimental.pallas import tpu as pltpu

F32 = jnp.float32
BF16 = jnp.bfloat16

EPS = 1e-6
D_MODEL = 1024
RET_HEADS = 4
RET_DK = 64
RET_DV = 128
RET_CHUNK = 128
LRU_WIDTH = 512
LRU_BLOCKS = 4
LRU_BLOCK = LRU_WIDTH // LRU_BLOCKS
CONV_WIDTH = 4
LRU_C = 8.0
GELU_C = 0.7978845608028654
REC_IN = 2 * RET_HEADS * RET_DK + 2 * RET_HEADS * RET_DV + 2 * LRU_WIDTH
ATT_HEADS = 16
ATT_DH = 64
PATTERNS = ((128, 1), (512, 4), (2048, 16))
ATT_BLOCK = 128
SUPER = 2048
MAX_DIL = 16
NEG = -1e30

LANES = 128
SUBLANES = 8
MXU_COLS = 256
NPAIR = D_MODEL // LANES
VMEM_LIMIT = 56 * 1024 * 1024

FF_CHUNK = 256
TOKEN_TILE = 512
WIDE_TILE = 1024
STREAM_ROWS = 512
REC_SEQS = 4


def _cparams(sem):
    return pltpu.CompilerParams(dimension_semantics=sem, vmem_limit_bytes=VMEM_LIMIT)


def _const_spec(shape):
    nd = len(shape)
    return pl.BlockSpec(shape, lambda *_: (0,) * nd, pipeline_mode=pl.Buffered(1))


def _rmsnorm(x, g):
    return x * lax.rsqrt(jnp.mean(x * x, axis=-1, keepdims=True) + EPS) * g


def _row_parts(rows):
    n = max(1, rows // STREAM_ROWS)
    return [slice(i * (rows // n), (i + 1) * (rows // n)) for i in range(n)]


def _norm_proj_kernel(x_ref, g_ref, w_ref, o_ref):
    for rs in _row_parts(x_ref.shape[0]):
        h = _rmsnorm(x_ref[rs, :], g_ref[...]).astype(BF16)
        o_ref[rs, :] = jnp.dot(h, w_ref[...], preferred_element_type=F32)


def _norm_proj(x2d, g, w_bf):
    m, d = x2d.shape
    n = w_bf.shape[1]
    tm = min(WIDE_TILE, m)
    return pl.pallas_call(
        _norm_proj_kernel,
        grid=(m // tm,),
        in_specs=[pl.BlockSpec((tm, d), lambda i: (i, 0)), _const_spec((1, d)), _const_spec((d, n))],
        out_specs=pl.BlockSpec((tm, n), lambda i: (i, 0)),
        out_shape=jax.ShapeDtypeStruct((m, n), F32),
        compiler_params=_cparams(("arbitrary",)),
        name="norm_proj",
    )(x2d, g.reshape(1, d), w_bf)


def _linear_scan(a, b, h0):
    rows, w = a.shape
    a = a.reshape(rows // SUBLANES, SUBLANES, w)
    b = b.reshape(rows // SUBLANES, SUBLANES, w)
    sub = lax.broadcasted_iota(jnp.int32, a.shape, 1)
    s = 1
    while s < SUBLANES:
        keep = sub >= s
        b = jnp.where(keep, a * pltpu.roll(b, s, axis=1), 0.0) + b
        a = jnp.where(keep, a * pltpu.roll(a, s, axis=1), a)
        s *= 2
    groups = []
    carry = h0
    for i in range(rows // SUBLANES):
        groups.append(b[i] + a[i] * carry)
        carry = groups[-1][SUBLANES - 1:SUBLANES]
    return jnp.concatenate(groups, axis=0)


def _pad_rows(x, rows):
    if x.shape[0] == rows:
        return x
    return jnp.concatenate([x, jnp.zeros((rows - x.shape[0], x.shape[1]), x.dtype)], axis=0)


def _mixer_init(s0_ref, h0_ref, c0_ref, s_scr, h_scr, xe_scr):
    s_scr[...] = s0_ref[...]
    h_scr[...] = h0_ref[...]
    xe_scr[...] = jnp.zeros(xe_scr.shape, F32)
    xe_scr[:, SUBLANES - (CONV_WIDTH - 1):SUBLANES, :] = c0_ref[...]


def _run_stages(*streams):
    for stages in streams:
        for stage in stages:
            stage()


def _run_alternating(a, b):
    ia = ib = 0
    while ia < len(a) or ib < len(b):
        if ib >= len(b) or (ia < len(a) and ia * len(b) <= ib * len(a)):
            a[ia]()
            ia += 1
        else:
            b[ib]()
            ib += 1


def _mixer(*args):
    _run_stages(_mixer_stages(*args))


def _mixer_stages(get, put, nseq, rows, tables, params, s_scr, h_scr, xe_scr, sret_ref, hlast_ref, cnew_ref):
    dec_ref, qdec_ref, kdec_ref, cdec_ref = tables
    retg_ref, convw_ref, convb_ref, wrg_ref, brg_ref, wig_ref, big_ref, lam_ref = params
    cp = RET_CHUNK
    hk = RET_HEADS * RET_DK
    v_off, g_off = 2 * hk, 2 * hk + RET_HEADS * RET_DV
    xl_off = g_off + RET_HEADS * RET_DV
    yl_off = xl_off + LRU_WIDTH
    lane = lax.broadcasted_iota(jnp.int32, (cp, LANES), 1)
    first = lane < RET_DK
    srow_first = lax.broadcasted_iota(jnp.int32, (2 * RET_DK, RET_DV), 0) < RET_DK
    nl = -lam_ref[...]
    neg_c_softplus = (-LRU_C) * (jnp.maximum(nl, 0.0) + jnp.log1p(jnp.exp(-jnp.abs(nl))))

    def retention(bi, pairs):
        for p in pairs:
            q2 = _pad_rows(get(bi, slice(p * LANES, (p + 1) * LANES)), cp)
            k2 = _pad_rows(get(bi, slice(hk + p * LANES, hk + (p + 1) * LANES)), cp) * (RET_DK ** -0.5)
            kb = k2.astype(BF16)
            kdt = jnp.transpose(k2 * kdec_ref[p]).astype(BF16)
            s2 = s_scr[bi, p]
            s2b = s2.astype(BF16)
            new_s = []
            for hh in range(2):
                h = 2 * p + hh
                sel = first if hh == 0 else jnp.logical_not(first)
                qh = jnp.where(sel, q2, 0.0)
                v = _pad_rows(get(bi, slice(v_off + h * RET_DV, v_off + (h + 1) * RET_DV)), cp)
                vb = v.astype(BF16)
                att = lax.dot_general(qh.astype(BF16), kb, (((1,), (1,)), ((), ())),
                                      preferred_element_type=F32) * dec_ref[h]
                o = (jnp.dot(att.astype(BF16), vb, preferred_element_type=F32)
                     + jnp.dot((qh * qdec_ref[p]).astype(BF16), s2b, preferred_element_type=F32))
                new_s.append(jnp.dot(kdt, vb, preferred_element_type=F32))
                o = o[:rows]
                o = o * lax.rsqrt(jnp.mean(o * o, axis=-1, keepdims=True) + EPS) * retg_ref[h:h + 1, :]
                gate = get(bi, slice(g_off + h * RET_DV, g_off + (h + 1) * RET_DV))
                put(bi, slice(h * RET_DV, (h + 1) * RET_DV), o * (gate * jax.nn.sigmoid(gate)))
            s_scr[bi, p] = s2 * cdec_ref[p] + jnp.where(srow_first, new_s[0], new_s[1])

    def conv_gates(bi, live):
        x = get(bi, slice(xl_off, xl_off + LRU_WIDTH))
        xe = jnp.concatenate([xe_scr[bi], x], axis=0)
        xc = convb_ref[...]
        for j in range(CONV_WIDTH):
            back = CONV_WIDTH - 1 - j
            xj = x if back == 0 else pltpu.roll(xe, back, axis=0)[SUBLANES:]
            xc = xc + xj * convw_ref[j:j + 1, :]
        xe_scr[bi] = xe[rows:]
        cnew_ref[bi] = xe_scr[bi, pl.ds(SUBLANES - (CONV_WIDTH - 1), CONV_WIDTH - 1), :]

        xcb = _pad_rows(xc, max(rows, 2 * SUBLANES)).astype(BF16)
        r_parts, i_parts = [], []
        for n in range(LRU_BLOCKS):
            xb = xcb[:, n * LRU_BLOCK:(n + 1) * LRU_BLOCK]
            r_parts.append(jnp.dot(xb, wrg_ref[n], preferred_element_type=F32)[:rows])
            i_parts.append(jnp.dot(xb, wig_ref[n], preferred_element_type=F32)[:rows])
        live["xc"] = xc
        live["r"] = jax.nn.sigmoid(jnp.concatenate(r_parts, axis=1) + brg_ref[...])
        live["ig"] = jax.nn.sigmoid(jnp.concatenate(i_parts, axis=1) + big_ref[...])

    def scan(bi, live):
        xc, r, ig = live["xc"], live["r"], live["ig"]
        a = jnp.exp(r * neg_c_softplus)
        one_m = 1.0 - a * a
        root = jnp.where(one_m > 0.0, one_m * lax.rsqrt(one_m), 0.0)
        hseq = _linear_scan(a, root * (ig * xc), h_scr[bi])
        h_scr[bi] = hseq[rows - 1:rows, :]
        hlast_ref[bi] = hseq[rows - 1:rows, :]
        live["hseq"] = hseq

    def gated_out(bi, live):
        hseq = live["hseq"]
        yl = get(bi, slice(yl_off, yl_off + LRU_WIDTH))
        half = 0.5 * yl
        gelu = half + half * jnp.tanh(yl * (GELU_C + (0.044715 * GELU_C) * (yl * yl)))
        put(bi, slice(RET_HEADS * RET_DV, RET_HEADS * RET_DV + LRU_WIDTH), hseq * gelu)

    def finish():
        sret_ref[...] = s_scr[...]

    stages = []
    for bi in range(nseq):
        live = {}
        stages += [functools.partial(retention, bi, (p,)) for p in range(RET_HEADS // 2)]
        stages += [functools.partial(f, bi, live) for f in (conv_gates, scan, gated_out)]
    return stages + [finish]


def _rec_mix_kernel(proj_ref, s0_ref, h0_ref, c0_ref, dec_ref, qdec_ref, kdec_ref, cdec_ref,
                    retg_ref, convw_ref, convb_ref, wrg_ref, brg_ref, wig_ref, big_ref, lam_ref,
                    mix_ref, sret_ref, hlast_ref, cnew_ref,
                    s_scr, h_scr, xe_scr, *, rows):
    @pl.when(pl.program_id(1) == 0)
    def _():
        _mixer_init(s0_ref, h0_ref, c0_ref, s_scr, h_scr, xe_scr)

    def put(bi, cols, val):
        mix_ref[bi, :, cols] = val.astype(mix_ref.dtype)

    _mixer(lambda bi, cols: proj_ref[bi, :, cols], put, proj_ref.shape[0], rows,
           (dec_ref, qdec_ref, kdec_ref, cdec_ref),
           (retg_ref, convw_ref, convb_ref, wrg_ref, brg_ref, wig_ref, big_ref, lam_ref),
           s_scr, h_scr, xe_scr, sret_ref, hlast_ref, cnew_ref)


def _ret_tables(t_eff):
    cp = RET_CHUNK
    f32 = np.float32
    log_g = np.log1p(-np.power(f32(2.0), f32(-5.0) - np.arange(RET_HEADS, dtype=f32)))
    idx = np.arange(cp, dtype=f32)
    live = idx < t_eff
    diff = idx[:, None] - idx[None, :]
    ok = (diff >= 0) & live[:, None] & live[None, :]
    decay = np.where(ok, np.exp(log_g[:, None, None] * np.maximum(diff, f32(0.0))), f32(0.0))
    q_dec = np.where(live, np.exp(log_g[:, None] * (idx + f32(1.0))), f32(0.0))
    k_dec = np.where(live, np.exp(log_g[:, None] * (f32(t_eff - 1.0) - idx)), f32(0.0))
    c_dec = np.exp(log_g * f32(t_eff))

    def pair_lanes(t):
        t = t.reshape(RET_HEADS // 2, 2, cp)
        return np.repeat(np.transpose(t, (0, 2, 1)), RET_DK, axis=2)

    c_rows = np.repeat(c_dec.reshape(RET_HEADS // 2, 2), RET_DK, axis=1)
    c_rows = np.broadcast_to(c_rows[:, :, None], (RET_HEADS // 2, 2 * RET_DK, RET_DV))
    as32 = lambda t: np.ascontiguousarray(t, dtype=f32)
    return as32(decay), as32(pair_lanes(q_dec)), as32(pair_lanes(k_dec)), as32(c_rows)


def _rec_mix(proj, s0, h0, c0, ret_g, conv_w, conv_b, w_rg, b_rg, w_ig, b_ig, lam):
    b, t, _ = proj.shape
    rows = RET_CHUNK if t % RET_CHUNK == 0 else t
    assert t % rows == 0 and rows % SUBLANES == 0 and (t == rows or rows == RET_CHUNK)
    n = t // rows
    nb = REC_SEQS if b % REC_SEQS == 0 else 1
    dec, qdec, kdec, cdec = _ret_tables(rows)
    hp = RET_HEADS // 2
    w = LRU_WIDTH
    per_b3 = lambda i, c: (i, 0, 0)
    outs = pl.pallas_call(
        functools.partial(_rec_mix_kernel, rows=rows),
        grid=(b // nb, n),
        in_specs=[
            pl.BlockSpec((nb, rows, REC_IN), lambda i, c: (i, c, 0)),
            pl.BlockSpec((nb, hp, 2 * RET_DK, RET_DV), lambda i, c: (i, 0, 0, 0)),
            pl.BlockSpec((nb, 1, w), per_b3),
            pl.BlockSpec((nb, CONV_WIDTH - 1, w), per_b3),
            _const_spec(dec.shape), _const_spec(qdec.shape), _const_spec(kdec.shape), _const_spec(cdec.shape),
            _const_spec((RET_HEADS, RET_DV)), _const_spec((CONV_WIDTH, w)), _const_spec((1, w)),
            _const_spec((LRU_BLOCKS, LRU_BLOCK, LRU_BLOCK)), _const_spec((1, w)),
            _const_spec((LRU_BLOCKS, LRU_BLOCK, LRU_BLOCK)), _const_spec((1, w)), _const_spec((1, w)),
        ],
        out_specs=[
            pl.BlockSpec((nb, rows, D_MODEL), lambda i, c: (i, c, 0)),
            pl.BlockSpec((nb, hp, 2 * RET_DK, RET_DV), lambda i, c: (i, 0, 0, 0)),
            pl.BlockSpec((nb, 1, w), per_b3),
            pl.BlockSpec((nb, CONV_WIDTH - 1, w), per_b3),
        ],
        out_shape=[
            jax.ShapeDtypeStruct((b, t, D_MODEL), BF16),
            jax.ShapeDtypeStruct((b, hp, 2 * RET_DK, RET_DV), F32),
            jax.ShapeDtypeStruct((b, 1, w), F32),
            jax.ShapeDtypeStruct((b, CONV_WIDTH - 1, w), F32),
        ],
        scratch_shapes=[
            pltpu.VMEM((nb, hp, 2 * RET_DK, RET_DV), F32),
            pltpu.VMEM((nb, 1, w), F32),
            pltpu.VMEM((nb, SUBLANES, w), F32),
        ],
        compiler_params=_cparams(("arbitrary", "arbitrary")),
        name="rec_mix",
    )(proj, s0.reshape(b, hp, 2 * RET_DK, RET_DV), h0.reshape(b, 1, w), c0,
      dec, qdec, kdec, cdec, ret_g, conv_w, conv_b.reshape(1, w),
      w_rg.astype(BF16), b_rg.reshape(1, w), w_ig.astype(BF16), b_ig.reshape(1, w), lam.reshape(1, w))
    mix, s_ret, h_last, c_new = outs
    return mix, s_ret.reshape(b, RET_HEADS, RET_DK, RET_DV), h_last.reshape(b, w), c_new


def _ffn_stages(load_mix, load_x, store, ffn_refs, act_ref, rs):
    wo_ref, g_ref, wg_ref, wu_ref, wd_ref = ffn_refs
    live = {}

    def prologue():
        live["y"] = load_x() + jnp.dot(load_mix(), wo_ref[...], preferred_element_type=F32)
        live["h"] = _rmsnorm(live["y"], g_ref[...]).astype(BF16)

    def gate(c):
        live["gate"] = jnp.dot(live["h"], wg_ref[:, c * FF_CHUNK:(c + 1) * FF_CHUNK], preferred_element_type=F32)

    def up(c):
        cols = slice(c * FF_CHUNK, (c + 1) * FF_CHUNK)
        g = live.pop("gate")
        act_ref[rs, cols] = (g * jax.nn.sigmoid(g) * jnp.dot(live["h"], wu_ref[:, cols],
                                                                preferred_element_type=F32)).astype(BF16)

    def down(c):
        cols = slice(c * MXU_COLS, (c + 1) * MXU_COLS)
        store(cols, live["y"][:, cols] + jnp.dot(act_ref[rs, :], wd_ref[:, cols], preferred_element_type=F32))

    stages = [prologue]
    for c in range(wg_ref.shape[1] // FF_CHUNK):
        stages += [functools.partial(gate, c), functools.partial(up, c)]
    return stages + [functools.partial(down, c) for c in range(wd_ref.shape[1] // MXU_COLS)]


def _out_ffn_kernel(mix_ref, x_ref, wo_ref, g_ref, wg_ref, wu_ref, wd_ref, o_ref, act_ref):
    def store_rows(rs, cols, val):
        o_ref[rs, cols] = val

    for rs in _row_parts(x_ref.shape[0]):
        _run_stages(_ffn_stages(lambda rs=rs: mix_ref[rs, :].astype(BF16), lambda rs=rs: x_ref[rs, :],
                                functools.partial(store_rows, rs),
                                (wo_ref, g_ref, wg_ref, wu_ref, wd_ref), act_ref, rs))


def _layer_spec(stacked, layer):
    shape = stacked.shape[1:]
    return pl.BlockSpec((None,) + shape, lambda *_: (layer,) + (0,) * len(shape), pipeline_mode=pl.Buffered(1))


def _out_ffn(mix2d, x2d, w_out, g, ffn_stacked, layer):
    m, d = x2d.shape
    tm = min(WIDE_TILE, m)
    row = lambda i: (i, 0)
    return pl.pallas_call(
        _out_ffn_kernel,
        grid=(m // tm,),
        in_specs=[pl.BlockSpec((tm, d), row), pl.BlockSpec((tm, d), row), _const_spec(w_out.shape),
                  _const_spec((1, d))] + [_layer_spec(w, layer) for w in ffn_stacked],
        out_specs=pl.BlockSpec((tm, d), row),
        out_shape=jax.ShapeDtypeStruct((m, d), F32),
        scratch_shapes=[pltpu.VMEM((tm, ffn_stacked[0].shape[2]), BF16)],
        compiler_params=_cparams(("arbitrary",)),
        name="out_ffn",
    )(mix2d, x2d, w_out, g.reshape(1, d), *ffn_stacked)


def _layer0_kernel(xc_ref, xp_ref, gm_ref, win_ref, s0_ref, h0_ref, c0_ref,
                   dec_ref, qdec_ref, kdec_ref, cdec_ref,
                   retg_ref, convw_ref, convb_ref, wrg_ref, brg_ref, wig_ref, big_ref, lam_ref,
                   wo_ref, g_ref, wg_ref, wu_ref, wd_ref,
                   y_ref, sret_ref, hlast_ref, cnew_ref,
                   s_scr, h_scr, xe_scr, mix_scr, act_scr):
    c = pl.program_id(1)
    last = pl.num_programs(1) - 1
    nseq, rows, d = xc_ref.shape
    tables = (dec_ref, qdec_ref, kdec_ref, cdec_ref)
    params = (retg_ref, convw_ref, convb_ref, wrg_ref, brg_ref, wig_ref, big_ref, lam_ref)
    ffn_refs = (wo_ref, g_ref, wg_ref, wu_ref, wd_ref)

    def mix_stages(slot):
        live = {}

        def project():
            x = xc_ref[...].reshape(nseq * rows, d)
            live["proj"] = jnp.dot(_rmsnorm(x, gm_ref[...]).astype(BF16), win_ref[...], preferred_element_type=F32)

        def put(bi, cols, val):
            mix_scr[slot, bi * rows:(bi + 1) * rows, cols] = val.astype(mix_scr.dtype)

        return [project] + _mixer_stages(lambda bi, cols: live["proj"][bi * rows:(bi + 1) * rows, cols], put,
                                         nseq, rows, tables, params, s_scr, h_scr, xe_scr,
                                         sret_ref, hlast_ref, cnew_ref)

    def ffn_stages(slot):
        def store(cols, val):
            y_ref[:, :, cols] = val.reshape(nseq, rows, val.shape[1])

        return _ffn_stages(lambda: mix_scr[slot], lambda: xp_ref[...].reshape(nseq * rows, d), store,
                           ffn_refs, act_scr, slice(0, nseq * rows))

    @pl.when(c == 0)
    def _():
        _mixer_init(s0_ref, h0_ref, c0_ref, s_scr, h_scr, xe_scr)
        _run_stages(mix_stages(0))

    @pl.when(jnp.logical_and(c > 0, c < last))
    def _():
        _run_alternating(mix_stages(c % 2), ffn_stages(1 - c % 2))

    @pl.when(c == last)
    def _():
        _run_stages(ffn_stages(1 - last % 2))


def _layer0_prompt(x, g_mix, w_in, s0, h0, c0, rec_prm, w_out, g_ffn, ffn_stacked, layer):
    ret_g, conv_w, conv_b, w_rg, b_rg, w_ig, b_ig, lam = rec_prm
    b, t, d = x.shape
    rows = RET_CHUNK
    assert t % rows == 0 and b % REC_SEQS == 0
    n = t // rows
    nb = REC_SEQS
    dec, qdec, kdec, cdec = _ret_tables(rows)
    hp = RET_HEADS // 2
    w = LRU_WIDTH
    per_b3 = lambda i, c: (i, 0, 0)
    chunk = (nb, rows, d)
    outs = pl.pallas_call(
        _layer0_kernel,
        grid=(b // nb, n + 1),
        in_specs=[
            pl.BlockSpec(chunk, lambda i, c: (i, jnp.minimum(c, n - 1), 0)),
            pl.BlockSpec(chunk, lambda i, c: (i, jnp.maximum(c - 1, 0), 0)),
            _const_spec((1, d)), _const_spec(w_in.shape),
            pl.BlockSpec((nb, hp, 2 * RET_DK, RET_DV), lambda i, c: (i, 0, 0, 0)),
            pl.BlockSpec((nb, 1, w), per_b3),
            pl.BlockSpec((nb, CONV_WIDTH - 1, w), per_b3),
            _const_spec(dec.shape), _const_spec(qdec.shape), _const_spec(kdec.shape), _const_spec(cdec.shape),
            _const_spec((RET_HEADS, RET_DV)), _const_spec((CONV_WIDTH, w)), _const_spec((1, w)),
            _const_spec((LRU_BLOCKS, LRU_BLOCK, LRU_BLOCK)), _const_spec((1, w)),
            _const_spec((LRU_BLOCKS, LRU_BLOCK, LRU_BLOCK)), _const_spec((1, w)), _const_spec((1, w)),
            _const_spec(w_out.shape), _const_spec((1, d)),
        ] + [_layer_spec(wt, layer) for wt in ffn_stacked],
        out_specs=[
            pl.BlockSpec(chunk, lambda i, c: (i, jnp.maximum(c - 1, 0), 0)),
            pl.BlockSpec((nb, hp, 2 * RET_DK, RET_DV), lambda i, c: (i, 0, 0, 0)),
            pl.BlockSpec((nb, 1, w), per_b3),
            pl.BlockSpec((nb, CONV_WIDTH - 1, w), per_b3),
        ],
        out_shape=[
            jax.ShapeDtypeStruct((b, t, d), F32),
            jax.ShapeDtypeStruct((b, hp, 2 * RET_DK, RET_DV), F32),
            jax.ShapeDtypeStruct((b, 1, w), F32),
            jax.ShapeDtypeStruct((b, CONV_WIDTH - 1, w), F32),
        ],
        scratch_shapes=[
            pltpu.VMEM((nb, hp, 2 * RET_DK, RET_DV), F32),
            pltpu.VMEM((nb, 1, w), F32),
            pltpu.VMEM((nb, SUBLANES, w), F32),
            pltpu.VMEM((2, nb * rows, d), BF16),
            pltpu.VMEM((nb * rows, ffn_stacked[0].shape[2]), BF16),
        ],
        compiler_params=_cparams(("arbitrary", "arbitrary")),
        name="layer0",
    )(x, x, g_mix.reshape(1, d), w_in, s0.reshape(b, hp, 2 * RET_DK, RET_DV), h0.reshape(b, 1, w), c0,
      dec, qdec, kdec, cdec, ret_g, conv_w, conv_b.reshape(1, w),
      w_rg.astype(BF16), b_rg.reshape(1, w), w_ig.astype(BF16), b_ig.reshape(1, w), lam.reshape(1, w),
      w_out, g_ffn.reshape(1, d), *ffn_stacked)
    y, s_ret, h_last, c_new = outs
    return y, s_ret.reshape(b, RET_HEADS, RET_DK, RET_DV), h_last.reshape(b, w), c_new


def _pair_rms(a, g2, first):
    sq = a * a
    s_a = jnp.sum(jnp.where(first, sq, 0.0), axis=-1, keepdims=True)
    s_b = jnp.sum(jnp.where(first, 0.0, sq), axis=-1, keepdims=True)
    ms = jnp.where(first, s_a, s_b) * (1.0 / ATT_DH)
    return a * lax.rsqrt(ms + EPS) * g2


def _qkv_slabs(x, g_ref, w_ref, qg_ref, kg_ref):
    h = _rmsnorm(x, g_ref[...]).astype(BF16)
    first = lax.broadcasted_iota(jnp.int32, (x.shape[0], LANES), 1) < ATT_DH
    slabs = []
    for c in range(w_ref.shape[1] // MXU_COLS):
        part = jnp.dot(h, w_ref[:, c * MXU_COLS:(c + 1) * MXU_COLS], preferred_element_type=F32)
        slabs += [part[:, i * LANES:(i + 1) * LANES] for i in range(MXU_COLS // LANES)]
    q = [_pair_rms(slabs[p], qg_ref[...], first) for p in range(NPAIR)]
    k = [_pair_rms(slabs[NPAIR + p], kg_ref[...], first) for p in range(NPAIR)]
    v = slabs[2 * NPAIR:]
    return q, k, v


def _qkv_flat_kernel(x_ref, g_ref, w_ref, qg_ref, kg_ref, q_ref, k_ref, v_ref):
    q, k, v = _qkv_slabs(x_ref[...], g_ref, w_ref, qg_ref, kg_ref)
    q_ref[...] = jnp.concatenate(q, axis=1)
    k_ref[...] = jnp.concatenate(k, axis=1)
    v_ref[...] = jnp.concatenate(v, axis=1)


def _qkv_perm_kernel(x_ref, g_ref, w_ref, qg_ref, kg_ref, q_ref, k_ref, v_ref, xs_ref):
    for p in range(NPAIR):
        xs_ref[p] = x_ref[:, p * LANES:(p + 1) * LANES]
    for rs in _row_parts(x_ref.shape[0]):
        n = (rs.stop - rs.start) // MAX_DIL
        x = jnp.concatenate(
            [jnp.concatenate([xs_ref[p, pl.ds(rs.start + r, n, stride=MAX_DIL), :] for p in range(NPAIR)], axis=1)
             for r in range(MAX_DIL)], axis=0)
        q, k, v = _qkv_slabs(x, g_ref, w_ref, qg_ref, kg_ref)
        dst = slice(rs.start // MAX_DIL, rs.stop // MAX_DIL)
        for ref, slabs in ((q_ref, q), (k_ref, k), (v_ref, v)):
            for p in range(NPAIR):
                for r in range(MAX_DIL):
                    ref[p, r, dst] = slabs[p][r * n:(r + 1) * n]


def _head_gain(g):
    return jnp.tile(g.reshape(1, ATT_DH), (1, LANES // ATT_DH))


def _qkv_flat(x2d, g, w_bf, qg, kg):
    m, d = x2d.shape
    tm = min(TOKEN_TILE, m)
    row = lambda i: (i, 0)
    return pl.pallas_call(
        _qkv_flat_kernel,
        grid=(m // tm,),
        in_specs=[pl.BlockSpec((tm, d), row), _const_spec((1, d)), _const_spec(w_bf.shape),
                  _const_spec((1, LANES)), _const_spec((1, LANES))],
        out_specs=[pl.BlockSpec((tm, d), row)] * 3,
        out_shape=[jax.ShapeDtypeStruct((m, d), F32)] * 3,
        compiler_params=_cparams(("arbitrary",)),
        name="qkv_flat",
    )(x2d, g.reshape(1, d), w_bf, _head_gain(qg), _head_gain(kg))


def _qkv_perm(x, g, w_bf, qg, kg):
    b, t, d = x.shape
    ns = t // SUPER
    tm = WIDE_TILE
    nt = SUPER // tm
    n = tm // MAX_DIL
    out_spec = pl.BlockSpec((None, NPAIR, None, MAX_DIL, None, n, LANES), lambda i, s, j: (i, 0, s, 0, j, 0, 0))
    out_shape = jax.ShapeDtypeStruct((b, NPAIR, ns, MAX_DIL, nt, n, LANES), F32)
    q, k, v = pl.pallas_call(
        _qkv_perm_kernel,
        grid=(b, ns, nt),
        in_specs=[pl.BlockSpec((None, tm, d), lambda i, s, j: (i, s * nt + j, 0)),
                  _const_spec((1, d)), _const_spec(w_bf.shape), _const_spec((1, LANES)), _const_spec((1, LANES))],
        out_specs=[out_spec] * 3,
        out_shape=[out_shape] * 3,
        scratch_shapes=[pltpu.VMEM((NPAIR, tm, LANES), F32)],
        compiler_params=_cparams(("arbitrary",) * 3),
        name="qkv_perm",
    )(x, g.reshape(1, d), w_bf, _head_gain(qg), _head_gain(kg))
    return tuple(a.reshape(b, NPAIR, t, LANES) for a in (q, k, v))


def _alibi_slopes():
    return np.power(np.float32(2.0), np.float32(-8.0) * np.arange(1, ATT_HEADS + 1, dtype=np.float32) / ATT_HEADS)


LOG2E = 1.4426950408889634


def _band_bias():
    i = np.arange(ATT_BLOCK)
    c = np.arange(2 * ATT_BLOCK)
    q_idx = {1: MAX_DIL * (i % 8) + i // 8,
             4: 4 * (i % 32) + i // 32,
             16: i}
    k_idx = {1: (MAX_DIL * (c % 8) + c // 16, (c // 8) % 2 == 0),
             4: (4 * (c % 32) + c // 64, (c // 32) % 2 == 0),
             16: (c % ATT_BLOCK, c < ATT_BLOCK)}
    slopes = _alibi_slopes().reshape(ATT_HEADS // 2, 2, 1, 1)
    neg = np.float32(NEG)
    out = []
    for (w, d) in PATTERNS:
        span = w // d
        km, is_prev = k_idx[d]
        diff = q_idx[d][:, None] - (km - np.where(is_prev, ATT_BLOCK, 0))[None, :]
        valid = (diff >= 0) & (diff <= span)
        bias = -slopes * (d * diff).astype(np.float32)[None, None]
        bias = bias * np.float32(LOG2E)
        variants = [np.where(valid, bias, neg), np.where(valid & ~is_prev[None, :], bias, neg)]
        out.append(np.stack([v.reshape(ATT_HEADS // 2, 2 * ATT_BLOCK, 2 * ATT_BLOCK) for v in variants]))
    return np.stack(out).astype(np.float32)


def _attend_pair(q, k, v, bias, first):
    qs = jnp.concatenate([jnp.where(first, q, 0.0), jnp.where(first, 0.0, q)], axis=0).astype(BF16)
    s = lax.dot_general(qs, k.astype(BF16), (((1,), (1,)), ((), ())), preferred_element_type=F32) + bias
    m = jnp.max(s, axis=-1, keepdims=True)
    e = jnp.exp2(s - m).astype(BF16)
    vb = v.astype(BF16)
    r = jnp.dot(e, jnp.concatenate([vb, jnp.ones_like(vb)], axis=1), preferred_element_type=F32)
    halves = lambda a: jnp.where(first, a[:ATT_BLOCK], a[ATT_BLOCK:])
    return halves(r[:, :LANES]), halves(m), halves(r[:, LANES:])


def _attn_prompt_kernel(q_ref, kc_ref, kp_ref, vc_ref, vp_ref, bias_ref, o_ref, ko_ref, vo_ref,
                        oacc, macc, dacc, nat_ref):
    sb = pl.program_id(2)
    first_sb = (sb == 0).astype(jnp.int32)
    first = lax.broadcasted_iota(jnp.int32, (ATT_BLOCK, LANES), 1) < ATT_DH
    scale = (ATT_DH ** -0.5) * LOG2E

    def attend_store(p, variant, q_chunks, k, v):
        n = ATT_BLOCK // len(q_chunks)
        q = jnp.concatenate([q_ref[sl, :] for sl in q_chunks], axis=0) * scale
        o, m, den = _attend_pair(q, k, v, bias_ref[p, variant], first)
        for ci, sl in enumerate(q_chunks):
            oacc[p, sl, :] = o[ci * n:(ci + 1) * n]
            macc[p, sl, :] = m[ci * n:(ci + 1) * n]
            dacc[p, sl, :] = den[ci * n:(ci + 1) * n]

    def keys(cur_ref, prev_ref, row0, n, blk):
        if blk > 0:
            return cur_ref[row0 + (blk - 1) * n:row0 + (blk + 1) * n, :]
        return jnp.concatenate([prev_ref[row0 + ATT_BLOCK - n:row0 + ATT_BLOCK, :], cur_ref[row0:row0 + n, :]], axis=0)

    n0 = ATT_BLOCK // MAX_DIL
    for bi in range(MAX_DIL):
        rows0 = [r * ATT_BLOCK for r in range(MAX_DIL)]
        k = jnp.concatenate([keys(kc_ref, kp_ref, r0, n0, bi) for r0 in rows0], axis=0)
        v = jnp.concatenate([keys(vc_ref, vp_ref, r0, n0, bi) for r0 in rows0], axis=0)
        attend_store(0, first_sb if bi == 0 else 0, [pl.ds(r0 + bi * n0, n0) for r0 in rows0], k, v)

    d1 = PATTERNS[1][1]
    n1 = ATT_BLOCK // (MAX_DIL // d1)
    for r4 in range(d1):
        rows0 = [(r4 + d1 * c) * ATT_BLOCK for c in range(MAX_DIL // d1)]
        for mb in range(ATT_BLOCK // n1):
            k = jnp.concatenate([keys(kc_ref, kp_ref, r0, n1, mb) for r0 in rows0], axis=0)
            v = jnp.concatenate([keys(vc_ref, vp_ref, r0, n1, mb) for r0 in rows0], axis=0)
            attend_store(1, first_sb if mb == 0 else 0, [pl.ds(r0 + mb * n1, n1) for r0 in rows0], k, v)

    for r in range(MAX_DIL):
        sl = pl.ds(r * ATT_BLOCK, ATT_BLOCK)
        k = jnp.concatenate([kp_ref[sl, :], kc_ref[sl, :]], axis=0)
        v = jnp.concatenate([vp_ref[sl, :], vc_ref[sl, :]], axis=0)
        attend_store(2, first_sb, [sl], k, v)

    for r in range(SUPER // ATT_BLOCK):
        sl = pl.ds(r * ATT_BLOCK, ATT_BLOCK)
        m0, m1, m2 = macc[0, sl, :], macc[1, sl, :], macc[2, sl, :]
        mx = jnp.maximum(jnp.maximum(m0, m1), m2)
        w0, w1, w2 = jnp.exp2(m0 - mx), jnp.exp2(m1 - mx), jnp.exp2(m2 - mx)
        num = w0 * oacc[0, sl, :] + w1 * oacc[1, sl, :] + w2 * oacc[2, sl, :]
        den = w0 * dacc[0, sl, :] + w1 * dacc[1, sl, :] + w2 * dacc[2, sl, :]
        o_ref[pl.ds(r, ATT_BLOCK, stride=MAX_DIL), :] = num / den

    @pl.when(sb == pl.num_programs(2) - 1)
    def _():
        for src, dst in ((kc_ref, ko_ref), (vc_ref, vo_ref)):
            for r in range(MAX_DIL):
                nat_ref[pl.ds(r, ATT_BLOCK, stride=MAX_DIL), :] = src[r * ATT_BLOCK:(r + 1) * ATT_BLOCK, :]
            dst[...] = jnp.transpose(nat_ref[...])


def _attn_prompt(q, k, v):
    b, npair, t, _ = q.shape
    ns = t // SUPER
    bias = _band_bias()
    cur = pl.BlockSpec((None, None, SUPER, LANES), lambda i, p, s: (i, p, s, 0))
    prev = pl.BlockSpec((None, None, SUPER, LANES), lambda i, p, s: (i, p, jnp.maximum(s - 1, 0), 0))
    cache = pl.BlockSpec((None, LANES, SUPER), lambda i, p, s: (i, p, 0))
    cache_shape = jax.ShapeDtypeStruct((b, npair * LANES, SUPER), F32)
    return pl.pallas_call(
        _attn_prompt_kernel,
        grid=(b, npair, ns),
        in_specs=[cur, cur, prev, cur, prev,
                  pl.BlockSpec((len(PATTERNS), 2, None, 2 * ATT_BLOCK, 2 * ATT_BLOCK),
                               lambda i, p, s: (0, 0, p, 0, 0))],
        out_specs=[pl.BlockSpec((None, SUPER, LANES), lambda i, p, s: (i, s, p)), cache, cache],
        out_shape=[jax.ShapeDtypeStruct((b, t, npair * LANES), F32), cache_shape, cache_shape],
        scratch_shapes=[pltpu.VMEM((len(PATTERNS), SUPER, LANES), F32),
                        pltpu.VMEM((len(PATTERNS), SUPER, LANES), F32),
                        pltpu.VMEM((len(PATTERNS), SUPER, LANES), F32),
                        pltpu.VMEM((SUPER, LANES), F32)],
        compiler_params=_cparams(("arbitrary",) * 3),
        name="attn_prompt",
    )(q, k, k, v, v, bias)


QUAD = 4
NEW_PAD = 128


def _sample_bias(s_len, n_past):
    s = np.arange(s_len)
    col = np.arange(n_past + NEW_PAD)
    dist = (n_past + s)[:, None] - col[None, :]
    real = (col < n_past + s_len)[None, :]
    slopes = _alibi_slopes().reshape(ATT_HEADS // QUAD, QUAD, 1, 1)
    out = []
    for (w, d) in PATTERNS:
        valid = (dist >= 0) & (dist % d == 0) & (dist <= w) & real
        bias = np.where(valid[None, None], -slopes * dist.astype(np.float32)[None, None], np.float32(NEG))
        out.append(bias.reshape(ATT_HEADS // QUAD, QUAD * s_len, n_past + NEW_PAD))
    return np.stack(out).astype(np.float32)


def _sample_stages(q_ref, kn_ref, vn_ref, ck_ref, cv_ref, bias_ref, o_ref, quad0, s_len):
    n_past = ck_ref.shape[1]
    width = QUAD * ATT_DH
    rows = QUAD * s_len
    nt = (((1,), (1,)), ((), ()))

    def own():
        return (lax.broadcasted_iota(jnp.int32, (rows, width), 0) // s_len
                == lax.broadcasted_iota(jnp.int32, (rows, width), 1) // ATT_DH)

    def scores(ql, live):
        cols = slice(ql * width, (ql + 1) * width)
        q = q_ref[:, cols] * (ATT_DH ** -0.5)
        qm = jnp.where(own(), jnp.concatenate([q] * QUAD, axis=0), 0.0).astype(BF16)
        kn = _pad_rows(kn_ref[:, cols], NEW_PAD).astype(BF16)
        live["s_c"] = jnp.dot(qm, ck_ref[cols, :].astype(BF16), preferred_element_type=F32)
        live["s_n"] = lax.dot_general(qm, kn, nt, preferred_element_type=F32)
        live["parts"] = []

    def pattern(ql, p, live):
        sc = live["s_c"] + bias_ref[p, quad0 + ql, :, 0:n_past]
        sn = live["s_n"] + bias_ref[p, quad0 + ql, :, n_past:n_past + NEW_PAD]
        m = jnp.maximum(jnp.max(sc, axis=-1, keepdims=True), jnp.max(sn, axis=-1, keepdims=True))
        ec, en = jnp.exp(sc - m), jnp.exp(sn - m)
        den = jnp.sum(ec, axis=-1, keepdims=True) + jnp.sum(en, axis=-1, keepdims=True)
        live["parts"].append((ec, en, den, m + jnp.log(den)))

    def combine(ql, live):
        cols = slice(ql * width, (ql + 1) * width)
        es_c, es_n, dens, lses = zip(*live["parts"])
        mx = jnp.maximum(jnp.maximum(lses[0], lses[1]), lses[2])
        ws = [jnp.exp(l - mx) for l in lses]
        wsum = ws[0] + ws[1] + ws[2]
        coef = [w / (wsum * d) for w, d in zip(ws, dens)]
        c_c = coef[0] * es_c[0] + coef[1] * es_c[1] + coef[2] * es_c[2]
        c_n = coef[0] * es_n[0] + coef[1] * es_n[1] + coef[2] * es_n[2]
        vn = _pad_rows(vn_ref[:, cols], NEW_PAD).astype(BF16)
        o = (lax.dot_general(c_c.astype(BF16), cv_ref[cols, :].astype(BF16), nt, preferred_element_type=F32)
             + jnp.dot(c_n.astype(BF16), vn, preferred_element_type=F32))
        o = jnp.where(own(), o, 0.0)
        acc = o[0:s_len]
        for hh in range(1, QUAD):
            acc = acc + o[hh * s_len:(hh + 1) * s_len]
        o_ref[:, cols] = acc.astype(o_ref.dtype)

    stages = []
    for ql in range(q_ref.shape[1] // width):
        live = {}
        stages.append(functools.partial(scores, ql, live))
        stages += [functools.partial(pattern, ql, p, live) for p in range(len(PATTERNS))]
        stages.append(functools.partial(combine, ql, live))
    return stages


SAMPLE_QUADS = 2


def _ffn_sample_kernel(mix_ref, x_ref, wo_ref, g_ref, wg_ref, wu_ref, wd_ref,
                       q_ref, kn_ref, vn_ref, ck_ref, cv_ref, bias_ref, o_ref, os_ref, act_ref, *, s_len):
    def store(cols, val):
        o_ref[:, cols] = val

    steps_per_seq = bias_ref.shape[1] // SAMPLE_QUADS
    _run_alternating(
        _ffn_stages(lambda: mix_ref[...].astype(BF16), lambda: x_ref[...], store,
                    (wo_ref, g_ref, wg_ref, wu_ref, wd_ref), act_ref, slice(0, x_ref.shape[0])),
        _sample_stages(q_ref, kn_ref, vn_ref, ck_ref, cv_ref, bias_ref, os_ref,
                       (pl.program_id(0) % steps_per_seq) * SAMPLE_QUADS, s_len))


def _out_ffn_and_sample_attn(mix2d, x2d, w_out, g, ffn_stacked, layer, q, kn, vn, ck, cv):
    m, d = x2d.shape
    b, s_len, _ = q.shape
    n_past = ck.shape[2]
    width = SAMPLE_QUADS * QUAD * ATT_DH
    sps = d // width
    steps = b * sps
    assert m % steps == 0 and (m // steps) % SUBLANES == 0
    tm = m // steps
    bias = _sample_bias(s_len, n_past)
    row = lambda i: (i, 0)
    new = pl.BlockSpec((None, s_len, width), lambda i: (i // sps, 0, i % sps))
    old = pl.BlockSpec((None, width, n_past), lambda i: (i // sps, i % sps, 0))
    return pl.pallas_call(
        functools.partial(_ffn_sample_kernel, s_len=s_len),
        grid=(steps,),
        in_specs=[pl.BlockSpec((tm, d), row), pl.BlockSpec((tm, d), row), _const_spec(w_out.shape),
                  _const_spec((1, d))] + [_layer_spec(w, layer) for w in ffn_stacked]
                 + [new, new, new, old, old, _const_spec(bias.shape)],
        out_specs=[pl.BlockSpec((tm, d), row), new],
        out_shape=[jax.ShapeDtypeStruct((m, d), F32), jax.ShapeDtypeStruct((b, s_len, d), BF16)],
        scratch_shapes=[pltpu.VMEM((tm, ffn_stacked[0].shape[2]), BF16)],
        compiler_params=_cparams(("arbitrary",)),
        name="ffn_sample_attn",
    )(mix2d, x2d, w_out, g.reshape(1, d), *ffn_stacked, q, kn, vn, ck, cv, bias)


def kernel(x_prompt, x_sample, state_ret, state_lru, state_conv, cache_k, cache_v, norm_mix, norm_ffn, w_in_rec, w_out_rec, ret_norm_g, conv_w, conv_b, w_rgate, b_rgate, w_igate, b_igate, lru_lambda, w_in_att, w_out_att, q_norm_g, k_norm_g, w_ffn_gate, w_ffn_up, w_ffn_down):
    bp, tp, d = x_prompt.shape
    bs, ts, _ = x_sample.shape
    depth = norm_mix.shape[0]
    assert tp % SUPER == 0 and d == D_MODEL
    yp, ys = x_prompt, x_sample
    ffn_stacked = (w_ffn_gate.astype(BF16), w_ffn_up.astype(BF16), w_ffn_down.astype(BF16))
    ret_p, lru_p, conv_p, kp, vp = [], [], [], [], []
    ret_s, lru_s, conv_s, ksl, vsl = [], [], [], [], []
    for l in range(depth):
        i = l // 2
        if l % 2 == 0:
            ffn = (w_out_rec[i].astype(BF16), norm_ffn[l], ffn_stacked, l)
            w_in = w_in_rec[i].astype(BF16)
            rec_prm = (ret_norm_g[i], conv_w[i], conv_b[i], w_rgate[i], b_rgate[i], w_igate[i], b_igate[i],
                       lru_lambda[i])
            yp, sr, sl, sc = _layer0_prompt(yp, norm_mix[l], w_in, jnp.zeros((bp, RET_HEADS, RET_DK, RET_DV), F32),
                                            jnp.zeros((bp, LRU_WIDTH), F32),
                                            jnp.zeros((bp, CONV_WIDTH - 1, LRU_WIDTH), F32), rec_prm, *ffn)
            ret_p.append(sr); lru_p.append(sl); conv_p.append(sc)
            proj = _norm_proj(ys.reshape(bs * ts, d), norm_mix[l], w_in).reshape(bs, ts, REC_IN)
            mix, sr, sl, sc = _rec_mix(proj, state_ret[i], state_lru[i], state_conv[i], *rec_prm)
            ys = _out_ffn(mix.reshape(bs * ts, d), ys.reshape(bs * ts, d), *ffn).reshape(bs, ts, d)
            ret_s.append(sr); lru_s.append(sl); conv_s.append(sc)
        else:
            ffn = (w_out_att[i].astype(BF16), norm_ffn[l], ffn_stacked, l)
            w_in = w_in_att[i].astype(BF16)
            q, k, v = _qkv_perm(yp, norm_mix[l], w_in, q_norm_g[i], k_norm_g[i])
            o, kc, vc = _attn_prompt(q, k, v)
            from_slabs = lambda c: jnp.transpose(c.reshape(bp, ATT_HEADS, ATT_DH, SUPER), (0, 3, 1, 2))
            kp.append(from_slabs(kc)); vp.append(from_slabs(vc))
            qs, kn, vn = _qkv_flat(ys.reshape(bs * ts, d), norm_mix[l], w_in, q_norm_g[i], k_norm_g[i])
            n_past = cache_k.shape[2]
            to_slabs = lambda c: jnp.transpose(c, (0, 2, 3, 1)).reshape(bs, d, n_past)
            yp, os_ = _out_ffn_and_sample_attn(
                o.reshape(bp * tp, d), yp.reshape(bp * tp, d), *ffn,
                qs.reshape(bs, ts, d), kn.reshape(bs, ts, d), vn.reshape(bs, ts, d),
                to_slabs(cache_k[i]), to_slabs(cache_v[i]))
            yp = yp.reshape(bp, tp, d)
            ys = _out_ffn(os_.reshape(bs * ts, d), ys.reshape(bs * ts, d), *ffn).reshape(bs, ts, d)
            ksl.append(kn.reshape(bs, ts, ATT_HEADS, ATT_DH)); vsl.append(vn.reshape(bs, ts, ATT_HEADS, ATT_DH))
    return (yp, ys,
            jnp.stack(ret_p), jnp.stack(lru_p), jnp.stack(conv_p), jnp.stack(kp), jnp.stack(vp),
            jnp.stack(ret_s), jnp.stack(lru_s), jnp.stack(conv_s), jnp.stack(ksl), jnp.stack(vsl))
```
